```python
import math
import jax, jax.numpy as jnp
from jax import lax
import numpy as np

D_MODEL = 1024
BATCH = 4
SEQ = 8192
DEPTH = 2

HEAD_DIM = 64
A_HEADS = 6
A_PATTERNS = ((128, 1), (512, 4), (2048, 16))
B_HEADS = 6
C_Q_HEADS = 6
C_KV_HEADS = 2
C_GROUP = C_Q_HEADS // C_KV_HEADS
C_WINDOW = 128
N_BRANCHES = 3
A_WIDTH = A_HEADS * HEAD_DIM
B_WIDTH = B_HEADS * HEAD_DIM
C_Q_WIDTH = C_Q_HEADS * HEAD_DIM
C_KV_WIDTH = C_KV_HEADS * HEAD_DIM
IN_SIZES = (A_WIDTH, A_WIDTH, A_WIDTH, B_WIDTH, B_WIDTH, B_WIDTH, B_HEADS,
            C_Q_WIDTH, C_KV_WIDTH, C_KV_WIDTH, N_BRANCHES * D_MODEL)
IN_COLS = sum(IN_SIZES)
BLOCK = 128
ROPE_THETA = 10000.0
RMS_EPS = 1e-6
NEG_INF = -1e30
PEER_HEADS = 8
PEER_DK = 256
PEER_DK_HALF = PEER_DK // 2
N_KEYS = 128
N_EXPERTS = N_KEYS * N_KEYS
PEER_TOPK = 16
PEER_CHUNK = 128

kernel_name = "hybrid_dilated_fox_swasink_peer"


def rms_norm(x, g):
    x32 = x.astype(jnp.float32)
    y = x32 * lax.rsqrt(jnp.mean(x32 * x32, axis=-1, keepdims=True) + RMS_EPS)
    return (y * g.astype(jnp.float32)).astype(x.dtype)


def rope_tables(positions):
    inv_freq = ROPE_THETA ** (-jnp.arange(0, HEAD_DIM, 2, dtype=jnp.float32) / HEAD_DIM)
    ang = positions.astype(jnp.float32)[..., None] * inv_freq
    return jnp.cos(ang), jnp.sin(ang)


def apply_rope(t, cos, sin):
    t32 = t.astype(jnp.float32)
    t1, t2 = jnp.split(t32, 2, axis=-1)
    c = cos[:, :, None, :]
    s = sin[:, :, None, :]
    return jnp.concatenate([t1 * c - t2 * s, t2 * c + t1 * s], axis=-1).astype(t.dtype)


def split_columns(t):
    parts, start = [], 0
    for size in IN_SIZES:
        parts.append(t[..., start:start + size])
        start += size
    return parts


def banded_attention(q, k, v, max_dist, sink=None):
    n, length, hk, g, dh = q.shape
    nb = -(-length // BLOCK)
    pad = nb * BLOCK - length
    q = jnp.pad(q, ((0, 0), (0, pad), (0, 0), (0, 0), (0, 0)))
    kv_pad = ((0, 0), (BLOCK, pad), (0, 0), (0, 0))

    def windows(t):
        tb = jnp.pad(t, kv_pad).reshape(n, nb + 1, BLOCK, hk, dh)
        return jnp.concatenate([tb[:, :-1], tb[:, 1:]], axis=2)

    kw, vw = windows(k), windows(v)
    qb = q.reshape(n, nb, BLOCK, hk, g, dh)
    logits = jnp.einsum('nbqhgd,nbkhd->nbhgqk', qb, kw,
                        preferred_element_type=jnp.float32) * (dh ** -0.5)
    dist = jnp.arange(BLOCK)[:, None] + BLOCK - jnp.arange(2 * BLOCK)[None, :]
    band = (dist >= 0) & (dist <= max_dist)
    key_pos = jnp.arange(nb)[:, None] * BLOCK - BLOCK + jnp.arange(2 * BLOCK)[None, :]
    valid = band[None, :, :] & (key_pos >= 0)[:, None, :]
    logits = jnp.where(valid[None, :, None, None], logits, NEG_INF)
    m = jnp.max(logits, axis=-1, keepdims=True)
    if sink is not None:
        s = sink.astype(jnp.float32)[None, None, :, :, None, None]
        m = jnp.maximum(m, s)
    p = jnp.exp(logits - m)
    denom = jnp.sum(p, axis=-1, keepdims=True)
    if sink is not None:
        denom = denom + jnp.exp(s - m)
    out = jnp.einsum('nbhgqk,nbkhd->nbqhgd', p.astype(v.dtype), vw,
                     preferred_element_type=jnp.float32)
    out = out / jnp.transpose(denom, (0, 1, 4, 2, 3, 5))
    out = out.reshape(n, nb * BLOCK, hk, g, dh)[:, :length].astype(q.dtype)
    lse = jnp.transpose((m + jnp.log(denom))[..., 0], (0, 1, 4, 2, 3))
    lse = lse.reshape(n, nb * BLOCK, hk, g)[:, :length]
    return out, lse


def dilated_mixture_attention(q, k, v):
    b, s, h, dh = q.shape
    outs, lses = [], []
    for window, dil in A_PATTERNS:
        length = s // dil

        def to_res(t):
            return t.reshape(b, length, dil, h, dh).transpose(0, 2, 1, 3, 4).reshape(b * dil, length, h, dh)

        o, lse = banded_attention(to_res(q)[:, :, :, None, :], to_res(k), to_res(v), window // dil)
        outs.append(o[:, :, :, 0].reshape(b, dil, length, h, dh).transpose(0, 2, 1, 3, 4).reshape(b, s, h, dh))
        lses.append(lse[..., 0].reshape(b, dil, length, h).transpose(0, 2, 1, 3).reshape(b, s, h))
    wts = jax.nn.softmax(jnp.stack(lses, axis=0), axis=0)
    out = jnp.einsum('pbsh,pbshd->bshd', wts, jnp.stack(outs, axis=0).astype(jnp.float32))
    return out.astype(q.dtype)


def forgetting_attention(q, k, v, f_logit, b_f):
    b, s, h, dh = q.shape
    nq = s // BLOCK
    log_f = jax.nn.log_sigmoid(f_logit.astype(jnp.float32) + b_f.astype(jnp.float32))
    cum = jnp.cumsum(log_f, axis=1)
    cum_t = cum.transpose(0, 2, 1)
    key_pos = jnp.arange(s)
    qb = q.reshape(b, nq, BLOCK, h, dh).transpose(1, 0, 2, 3, 4)
    cb = cum_t.reshape(b, h, nq, BLOCK).transpose(2, 0, 1, 3)

    def block(args):
        q_blk, c_blk, i = args
        logits = jnp.einsum('bqhd,bkhd->bhqk', q_blk, k,
                            preferred_element_type=jnp.float32) * (dh ** -0.5)
        decay = c_blk[..., None] - cum_t[:, :, None, :]
        q_pos = i * BLOCK + jnp.arange(BLOCK)
        causal = key_pos[None, :] <= q_pos[:, None]
        p = jax.nn.softmax(jnp.where(causal, logits + decay, NEG_INF), axis=-1)
        return jnp.einsum('bhqk,bkhd->bqhd', p.astype(v.dtype), v,
                          preferred_element_type=jnp.float32).astype(q.dtype)

    out = lax.map(block, (qb, cb, jnp.arange(nq)))
    return out.transpose(1, 0, 2, 3, 4).reshape(b, s, h, dh)


def hybrid_mixer(xn, cos, sin, w_in, b_forget, sinks, b_gate, w_br_a, w_br_b, w_br_c, w_out):
    b, s, _ = xn.shape
    proj = jnp.einsum('bsd,dc->bsc', xn, w_in)
    qa, ka, va, qb, kb, vb, fb, qc, kc, vc, gate_logits = split_columns(proj)

    def heads(t, n):
        return t.reshape(b, s, n, HEAD_DIM)

    oa = dilated_mixture_attention(apply_rope(heads(qa, A_HEADS), cos, sin),
                                   apply_rope(heads(ka, A_HEADS), cos, sin),
                                   heads(va, A_HEADS)).reshape(b, s, A_WIDTH)
    ob = forgetting_attention(heads(qb, B_HEADS), heads(kb, B_HEADS), heads(vb, B_HEADS),
                              fb, b_forget).reshape(b, s, B_WIDTH)
    qc_r = apply_rope(heads(qc, C_Q_HEADS), cos, sin).reshape(b, s, C_KV_HEADS, C_GROUP, HEAD_DIM)
    kc_r = apply_rope(heads(kc, C_KV_HEADS), cos, sin)
    oc, _ = banded_attention(qc_r, kc_r, heads(vc, C_KV_HEADS), C_WINDOW - 1,
                             sink=sinks.reshape(C_KV_HEADS, C_GROUP))
    oc = oc.reshape(b, s, C_Q_WIDTH)
    gates = jax.nn.sigmoid(gate_logits.reshape(b, s, N_BRANCHES, D_MODEL) + b_gate)
    merged = (gates[:, :, 0] * (oa @ w_br_a)
              + gates[:, :, 1] * (ob @ w_br_b)
              + gates[:, :, 2] * (oc @ w_br_c))
    return merged @ w_out


def peer_ffn(xn, w_q, k1, k2, u, v):
    b, s, d = xn.shape
    xt = xn.reshape(-1, PEER_CHUNK, d)
    k1f = k1.astype(jnp.float32)
    k2f = k2.astype(jnp.float32)

    def chunk(xc):
        q = (xc @ w_q).reshape(PEER_CHUNK, PEER_HEADS, 2, PEER_DK_HALF).astype(jnp.float32)
        s1 = jnp.einsum('thd,nd->thn', q[:, :, 0], k1f)
        s2 = jnp.einsum('thd,nd->thn', q[:, :, 1], k2f)
        v1, i1 = lax.top_k(s1, PEER_TOPK)
        v2, i2 = lax.top_k(s2, PEER_TOPK)
        cand = (v1[..., :, None] + v2[..., None, :]).reshape(PEER_CHUNK, PEER_HEADS, PEER_TOPK * PEER_TOPK)
        cand_id = (i1[..., :, None] * N_KEYS + i2[..., None, :]).reshape(PEER_CHUNK, PEER_HEADS, PEER_TOPK * PEER_TOPK)
        top_s, top_pos = lax.top_k(cand, PEER_TOPK)
        ids = jnp.take_along_axis(cand_id, top_pos, axis=-1)
        g = jax.nn.softmax(top_s, axis=-1)
        act = jnp.einsum('thkd,td->thk', u[ids], xc, preferred_element_type=jnp.float32)
        w = (g * jax.nn.gelu(act, approximate=False)).astype(v.dtype)
        return jnp.einsum('thk,thkd->td', w, v[ids], preferred_element_type=jnp.float32).astype(xn.dtype)

    return lax.map(chunk, xt).reshape(b, s, d)


def setup_inputs(seed: int = 0) -> dict:
    key = jax.random.key(seed)
    ks = jax.random.split(key, 18)
    f32 = jnp.float32

    def nrm(k, shape, scale):
        return jax.random.normal(k, shape, f32) * scale

    return {
        "x": nrm(ks[0], (BATCH, SEQ, D_MODEL), 1.0),
        "positions": jnp.broadcast_to(jnp.arange(SEQ, dtype=jnp.int32), (BATCH, SEQ)),
        "norm1_g": 1.0 + nrm(ks[1], (DEPTH, D_MODEL), 0.05),
        "w_in": nrm(ks[2], (DEPTH, D_MODEL, IN_COLS), D_MODEL ** -0.5),
        "b_forget": 2.0 + nrm(ks[3], (DEPTH, B_HEADS), 0.5),
        "sinks": nrm(ks[4], (DEPTH, C_Q_HEADS), 0.5),
        "b_gate": nrm(ks[5], (DEPTH, N_BRANCHES, D_MODEL), 0.1),
        "w_br_a": nrm(ks[6], (DEPTH, A_WIDTH, D_MODEL), A_WIDTH ** -0.5),
        "w_br_b": nrm(ks[7], (DEPTH, B_WIDTH, D_MODEL), B_WIDTH ** -0.5),
        "w_br_c": nrm(ks[8], (DEPTH, C_Q_WIDTH, D_MODEL), C_Q_WIDTH ** -0.5),
        "w_out": nrm(ks[9], (DEPTH, D_MODEL, D_MODEL), D_MODEL ** -0.5),
        "norm2_g": 1.0 + nrm(ks[10], (DEPTH, D_MODEL), 0.05),
        "peer_wq": nrm(ks[11], (DEPTH, D_MODEL, PEER_HEADS * PEER_DK), D_MODEL ** -0.5),
        "peer_k1": nrm(ks[12], (DEPTH, N_KEYS, PEER_DK_HALF), PEER_DK_HALF ** -0.5),
        "peer_k2": nrm(ks[13], (DEPTH, N_KEYS, PEER_DK_HALF), PEER_DK_HALF ** -0.5),
        "peer_u": nrm(ks[14], (DEPTH, N_EXPERTS, D_MODEL), D_MODEL ** -0.5),
        "peer_v": nrm(ks[15], (DEPTH, N_EXPERTS, D_MODEL), PEER_HEADS ** -0.5),
        "final_g": 1.0 + nrm(ks[16], (D_MODEL,), 0.05),
    }


def reference(x, positions, norm1_g, w_in, b_forget, sinks, b_gate, w_br_a, w_br_b, w_br_c,
              w_out, norm2_g, peer_wq, peer_k1, peer_k2, peer_u, peer_v, final_g):
    cos, sin = rope_tables(positions)
    h = x
    for l in range(DEPTH):
        h = h + hybrid_mixer(rms_norm(h, norm1_g[l]), cos, sin, w_in[l], b_forget[l], sinks[l],
                             b_gate[l], w_br_a[l], w_br_b[l], w_br_c[l], w_out[l])
        h = h + peer_ffn(rms_norm(h, norm2_g[l]), peer_wq[l], peer_k1[l], peer_k2[l],
                         peer_u[l], peer_v[l])
    return rms_norm(h, final_g)
```

```python
import functools
import math

import jax
import jax.numpy as jnp
import numpy as np
from jax import lax
from jax.experimental import pallas as pl
from jax.experimental.pallas import tpu as pltpu

F32 = jnp.float32
BF16 = jnp.bfloat16

D_MODEL = 1024
HEAD_DIM = 64
N_HEADS = 6
N_PAIRS = N_HEADS // 2
C_KV_HEADS = 2
A_DILATIONS = (1, 4, 16)
A_MAX_DIST = 128
C_MAX_DIST = 127
BLOCK = 128
LANE = 128
ROPE_THETA = 10000.0
RMS_EPS = 1e-6
NEG_INF = -1e30
PEER_HEADS = 8
N_KEYS = 128
N_EXPERTS = N_KEYS * N_KEYS
PEER_TOPK = 16
WIDTH = N_HEADS * HEAD_DIM

COL_QA, COL_KA, COL_VA = 0, 3, 6
COL_QB, COL_KB, COL_VB = 9, 12, 15
COL_QC, COL_KC, COL_VC = 18, 21, 22
N_QKV_BLOCKS = 23
COL_F = 23
ROPE_BLOCKS = (0, 1, 2, 3, 4, 5, 18, 19, 20, 21)
QUERY_BLOCKS = (0, 1, 2, 9, 10, 11, 18, 19, 20)
QK_SCALE = HEAD_DIM ** -0.5

VMEM_LIMIT = 56 * 1024 * 1024


def _params(sem):
    return pltpu.CompilerParams(dimension_semantics=sem, vmem_limit_bytes=VMEM_LIMIT)


def _dot(a, b):
    return jnp.dot(a, b, preferred_element_type=F32)


def _dot_nt(a, b):
    return lax.dot_general(a, b, (((1,), (1,)), ((), ())), preferred_element_type=F32)


def _rms(x, g):
    var = jnp.mean(x * x, axis=-1, keepdims=True)
    return x * lax.rsqrt(var + RMS_EPS) * g


def _in_proj_kernel(h_ref, g_ref, w_ref, cos_ref, sin_ref, qkv_ref, f_ref, proj_s):
    xn = _rms(h_ref[...], g_ref[...]).astype(BF16)
    proj_s[...] = _dot(xn, w_ref[...])
    cos = cos_ref[...]
    sin = sin_ref[...]
    lane = lax.broadcasted_iota(jnp.int32, cos.shape, 1)
    first_half = (lane % HEAD_DIM) < (HEAD_DIM // 2)
    for c in range(N_QKV_BLOCKS):
        t = proj_s[:, c * LANE:(c + 1) * LANE]
        if c in ROPE_BLOCKS:
            rot = jnp.where(first_half, pltpu.roll(t, LANE - HEAD_DIM // 2, 1), pltpu.roll(t, HEAD_DIM // 2, 1))
            t = t * cos + rot * sin
        if c in QUERY_BLOCKS:
            t = t * QK_SCALE
        qkv_ref[:, c * LANE:(c + 1) * LANE] = t.astype(BF16)
    f_ref[...] = proj_s[:, COL_F * LANE:(COL_F + 1) * LANE]


def _in_proj(h, g, w_att, cos4, sin4, tm=512):
    t_tokens = h.shape[0]
    n_cols = w_att.shape[1]
    return pl.pallas_call(
        _in_proj_kernel,
        out_shape=(jax.ShapeDtypeStruct((t_tokens, N_QKV_BLOCKS * LANE), BF16),
                   jax.ShapeDtypeStruct((t_tokens, LANE), F32)),
        grid=(t_tokens // tm,),
        in_specs=[pl.BlockSpec((tm, D_MODEL), lambda i: (i, 0)),
                  pl.BlockSpec((1, D_MODEL), lambda i: (0, 0)),
                  pl.BlockSpec((D_MODEL, n_cols), lambda i: (0, 0)),
                  pl.BlockSpec((tm, LANE), lambda i: (i, 0)),
                  pl.BlockSpec((tm, LANE), lambda i: (i, 0))],
        out_specs=(pl.BlockSpec((tm, N_QKV_BLOCKS * LANE), lambda i: (i, 0)),
                   pl.BlockSpec((tm, LANE), lambda i: (i, 0))),
        scratch_shapes=[pltpu.VMEM((tm, n_cols), F32)],
        compiler_params=_params(("parallel",)),
        name="in_proj",
    )(h, g, w_att, cos4, sin4)


def _banded_kernel(*refs, max_dist, qb, kv_shared, has_sink, want_lse):
    refs = list(refs)
    q_ref, kp_ref, kc_ref, vp_ref, vc_ref = refs[:5]
    pos = 5
    sink_ref = None
    if has_sink:
        sink_ref = refs[pos]
        pos += 1
    o_ref = refs[pos]
    pos += 1
    lse_ref = None
    if want_lse:
        lse_ref = refs[pos]
        pos += 1
    kwin, vwin = refs[pos], refs[pos + 1]

    i = pl.program_id(1)
    kwin[0:BLOCK, :] = kp_ref[...]
    kwin[BLOCK:, :] = kc_ref[...]
    vwin[0:BLOCK, :] = vp_ref[...]
    vwin[BLOCK:, :] = vc_ref[...]

    row = lax.broadcasted_iota(jnp.int32, (BLOCK, 2 * BLOCK), 0)
    col = lax.broadcasted_iota(jnp.int32, (BLOCK, 2 * BLOCK), 1)
    dist = row + BLOCK - col
    band = (dist >= 0) & (dist <= max_dist)
    lane = lax.broadcasted_iota(jnp.int32, (1, LANE), 1)
    lower = lane < HEAD_DIM

    def body(sb, carry):
        off = pl.multiple_of(sb * BLOCK, BLOCK)
        valid = band & ((col >= BLOCK) | (i * qb + sb > 0))
        for p in range(N_PAIRS):
            q = q_ref[pl.ds(off, BLOCK), p * LANE:(p + 1) * LANE]
            kv_cols = slice(0, LANE) if kv_shared else slice(p * LANE, (p + 1) * LANE)
            k = kwin[pl.ds(off, 2 * BLOCK), kv_cols]
            v = vwin[pl.ds(off, 2 * BLOCK), kv_cols]
            o_pair = jnp.zeros((BLOCK, LANE), F32)
            lse_pair = jnp.zeros((BLOCK, LANE), F32)
            for hf in range(2):
                sel = lower if hf == 0 else jnp.logical_not(lower)
                qm = jnp.where(sel, q, jnp.zeros_like(q))
                s = _dot_nt(qm, k)
                s = jnp.where(valid, s, NEG_INF)
                m = jnp.max(s, axis=-1, keepdims=True)
                if has_sink:
                    sk = sink_ref[2 * p + hf]
                    m = jnp.maximum(m, sk)
                pr = jnp.exp(s - m)
                den = jnp.sum(pr, axis=-1, keepdims=True)
                if has_sink:
                    den = den + jnp.exp(sk - m)
                o = _dot(pr.astype(BF16), v) / den
                o_pair = jnp.where(sel, o, o_pair)
                if want_lse:
                    lse_pair = jnp.where(sel, m + jnp.log(den), lse_pair)
            o_ref[pl.ds(off, BLOCK), p * LANE:(p + 1) * LANE] = o_pair.astype(BF16)
            if want_lse:
                lse_ref[pl.ds(off, BLOCK), p * LANE:(p + 1) * LANE] = lse_pair
        return carry

    lax.fori_loop(0, qb, body, 0)


def _banded(x, qcol, kcol, vcol, max_dist, kv_shared, sink=None, want_lse=True, qb=4):
    n, length, _ = x.shape
    qb = min(qb, length // BLOCK)
    tq = qb * BLOCK
    kvw = LANE if kv_shared else WIDTH
    kblk = kcol if kv_shared else kcol // N_PAIRS
    vblk = vcol if kv_shared else vcol // N_PAIRS
    in_specs = [
        pl.BlockSpec((None, tq, WIDTH), lambda b, i: (b, i, qcol // N_PAIRS)),
        pl.BlockSpec((None, BLOCK, kvw), lambda b, i: (b, jnp.maximum(i * qb - 1, 0), kblk)),
        pl.BlockSpec((None, tq, kvw), lambda b, i: (b, i, kblk)),
        pl.BlockSpec((None, BLOCK, kvw), lambda b, i: (b, jnp.maximum(i * qb - 1, 0), vblk)),
        pl.BlockSpec((None, tq, kvw), lambda b, i: (b, i, vblk)),
    ]
    args = [x, x, x, x, x]
    if sink is not None:
        in_specs.append(pl.BlockSpec(memory_space=pltpu.SMEM))
        args.append(sink)
    out_shape = [jax.ShapeDtypeStruct((n, length, WIDTH), BF16)]
    out_specs = [pl.BlockSpec((None, tq, WIDTH), lambda b, i: (b, i, 0))]
    if want_lse:
        out_shape.append(jax.ShapeDtypeStruct((n, length, WIDTH), F32))
        out_specs.append(pl.BlockSpec((None, tq, WIDTH), lambda b, i: (b, i, 0)))
    kern = functools.partial(_banded_kernel, max_dist=max_dist, qb=qb, kv_shared=kv_shared,
                             has_sink=sink is not None, want_lse=want_lse)
    return pl.pallas_call(
        kern,
        out_shape=tuple(out_shape),
        grid=(n, length // tq),
        in_specs=in_specs,
        out_specs=tuple(out_specs),
        scratch_shapes=[pltpu.VMEM((tq + BLOCK, kvw), BF16), pltpu.VMEM((tq + BLOCK, kvw), BF16)],
        compiler_params=_params(("parallel", "parallel")),
        name="banded_attention",
    )(*args)


def _combine_kernel(o1_ref, o2_ref, o3_ref, l1_ref, l2_ref, l3_ref, out_ref):
    l1, l2, l3 = l1_ref[...], l2_ref[...], l3_ref[...]
    m = jnp.maximum(jnp.maximum(l1, l2), l3)
    w1, w2, w3 = jnp.exp(l1 - m), jnp.exp(l2 - m), jnp.exp(l3 - m)
    num = w1 * o1_ref[...].astype(F32) + w2 * o2_ref[...].astype(F32) + w3 * o3_ref[...].astype(F32)
    out_ref[...] = (num / (w1 + w2 + w3)).astype(BF16)


def _combine(outs, lses, tm=1024):
    t_tokens = outs[0].shape[0]
    spec = pl.BlockSpec((tm, WIDTH), lambda i: (i, 0))
    return pl.pallas_call(
        _combine_kernel,
        out_shape=jax.ShapeDtypeStruct((t_tokens, WIDTH), BF16),
        grid=(t_tokens // tm,),
        in_specs=[spec] * 6,
        out_specs=spec,
        compiler_params=_params(("parallel",)),
        name="combine_patterns",
    )(*outs, *lses)


CUM_BLOCK = 256


def _cum_kernel(f_ref, bf_ref, augq_ref, augk_ref):
    head = pl.program_id(1)
    seq = f_ref.shape[0]
    r = lax.broadcasted_iota(jnp.int32, (LANE, LANE), 0)
    sel = jnp.where(r == head, 1.0, 0.0).astype(F32)
    rr = lax.broadcasted_iota(jnp.int32, (CUM_BLOCK, CUM_BLOCK), 0)
    cc = lax.broadcasted_iota(jnp.int32, (CUM_BLOCK, CUM_BLOCK), 1)
    tril = jnp.where(rr >= cc, 1.0, 0.0).astype(F32)
    lane = lax.broadcasted_iota(jnp.int32, (CUM_BLOCK, LANE), 1) % HEAD_DIM
    bias = bf_ref[...]

    def body(blk, carry):
        off = pl.multiple_of(blk * CUM_BLOCK, CUM_BLOCK)
        x = f_ref[pl.ds(off, CUM_BLOCK), :] + bias
        log_f = jnp.minimum(x, 0.0) - jnp.log1p(jnp.exp(-jnp.abs(x)))
        col = jnp.dot(log_f, sel, preferred_element_type=F32, precision=lax.Precision.HIGHEST)
        cum = jnp.dot(tril, col, preferred_element_type=F32, precision=lax.Precision.HIGHEST) + carry
        c1 = cum.astype(BF16).astype(F32)
        c2 = (cum - c1).astype(BF16).astype(F32)
        c3 = (cum - c1 - c2).astype(BF16).astype(F32)
        piece = jnp.where(lane % 3 == 0, c1, jnp.where(lane % 3 == 1, c2, c3))
        one = jnp.ones_like(cum)
        zero = jnp.zeros_like(cum)
        aq = jnp.where(lane < 3, piece, jnp.where(lane < 6, one, zero))
        ak = jnp.where(lane < 3, one, jnp.where(lane < 6, -piece, zero))
        augq_ref[pl.ds(off, CUM_BLOCK), :] = aq.astype(BF16)
        augk_ref[pl.ds(off, CUM_BLOCK), :] = ak.astype(BF16)
        return cum[CUM_BLOCK - 1:CUM_BLOCK, :]

    lax.fori_loop(0, seq // CUM_BLOCK, body, jnp.zeros((1, LANE), F32))


def _cum_aug(f, b_forget_row):
    b, seq, _ = f.shape
    out = jax.ShapeDtypeStruct((b, N_HEADS, seq, LANE), BF16)
    ospec = pl.BlockSpec((None, None, seq, LANE), lambda bi, hi: (bi, hi, 0, 0))
    return pl.pallas_call(
        _cum_kernel,
        out_shape=(out, out),
        grid=(b, N_HEADS),
        in_specs=[pl.BlockSpec((None, seq, LANE), lambda bi, hi: (bi, 0, 0)),
                  pl.BlockSpec((1, LANE), lambda bi, hi: (0, 0))],
        out_specs=(ospec, ospec),
        compiler_params=_params(("parallel", "parallel")),
        name="forget_cumsum",
    )(f, b_forget_row)


def _fox_kernel(q_ref, k_ref, v_ref, aq_ref, ak_ref, o_ref, kaug, *, tq):
    qi = pl.program_id(2)
    lane = lax.broadcasted_iota(jnp.int32, (1, LANE), 1)
    lower = lane < HEAD_DIM

    @pl.when(qi == 0)
    def _():
        k = k_ref[...]
        kaug[0] = jnp.where(lower, k, ak_ref[0])
        kaug[1] = jnp.where(lower, ak_ref[1], k)

    q = q_ref[...]
    row = lax.broadcasted_iota(jnp.int32, (tq, tq), 0)
    col = lax.broadcasted_iota(jnp.int32, (tq, tq), 1)
    causal = col <= row
    o_pair = jnp.zeros((tq, LANE), F32)
    for hf in range(2):
        sel = lower if hf == 0 else jnp.logical_not(lower)
        qh = jnp.where(sel, q, aq_ref[hf])

        def step(kb, carry, masked):
            m, l, acc = carry
            off = pl.multiple_of(kb * tq, tq)
            s = _dot_nt(qh, kaug[hf, pl.ds(off, tq), :])
            if masked:
                s = jnp.where(causal, s, NEG_INF)
            m_new = jnp.maximum(m, jnp.max(s, axis=-1, keepdims=True))
            alpha = jnp.exp(m - m_new)
            pr = jnp.exp(s - m_new)
            l = alpha * l + jnp.sum(pr, axis=-1, keepdims=True)
            acc = alpha * acc + _dot(pr.astype(BF16), v_ref[pl.ds(off, tq), :])
            return m_new, l, acc

        init = (jnp.full((tq, 1), NEG_INF, F32), jnp.zeros((tq, 1), F32), jnp.zeros((tq, LANE), F32))
        carry = lax.fori_loop(0, qi, functools.partial(step, masked=False), init)
        m, l, acc = step(qi, carry, True)
        o_pair = jnp.where(sel, acc / l, o_pair)
    o_ref[...] = o_pair.astype(BF16)


def _fox(qkv, augq, augk, tq=256):
    b, seq, _ = qkv.shape
    return pl.pallas_call(
        functools.partial(_fox_kernel, tq=tq),
        out_shape=jax.ShapeDtypeStruct((b, seq, WIDTH), BF16),
        grid=(b, N_PAIRS, seq // tq),
        in_specs=[pl.BlockSpec((None, tq, LANE), lambda bi, p, i: (bi, i, COL_QB + p)),
                  pl.BlockSpec((None, seq, LANE), lambda bi, p, i: (bi, 0, COL_KB + p)),
                  pl.BlockSpec((None, seq, LANE), lambda bi, p, i: (bi, 0, COL_VB + p)),
                  pl.BlockSpec((None, 2, tq, LANE), lambda bi, p, i: (bi, p, i, 0)),
                  pl.BlockSpec((None, 2, seq, LANE), lambda bi, p, i: (bi, p, 0, 0))],
        out_specs=pl.BlockSpec((None, tq, LANE), lambda bi, p, i: (bi, i, p)),
        scratch_shapes=[pltpu.VMEM((2, seq, LANE), BF16)],
        compiler_params=_params(("parallel", "parallel", "arbitrary")),
        name="forgetting_attention",
    )(qkv, qkv, qkv, augq, augk)


def _merge_kernel(h_ref, g_ref, oa_ref, ob_ref, oc_ref, wg_ref, bg_ref, wa_ref, wb_ref, wc_ref, wo_ref, out_ref):
    h = h_ref[...]
    xn = _rms(h, g_ref[...]).astype(BF16)
    merged = jnp.zeros(h.shape, F32)
    for idx, (o_ref, w_ref) in enumerate(((oa_ref, wa_ref), (ob_ref, wb_ref), (oc_ref, wc_ref))):
        cols = slice(idx * D_MODEL, (idx + 1) * D_MODEL)
        gate = jax.nn.sigmoid(_dot(xn, wg_ref[:, cols]) + bg_ref[:, cols])
        merged = merged + gate * _dot(o_ref[...], w_ref[...])
    out_ref[...] = h + _dot(merged.astype(BF16), wo_ref[...])


def _merge(h, g, oa, ob, oc, w_gate, b_gate, w_a, w_b, w_c, w_out, tm=256):
    t_tokens = h.shape[0]
    full = lambda shape: pl.BlockSpec(shape, lambda i: (0, 0))
    ospec = pl.BlockSpec((tm, WIDTH), lambda i: (i, 0))
    return pl.pallas_call(
        _merge_kernel,
        out_shape=jax.ShapeDtypeStruct((t_tokens, D_MODEL), F32),
        grid=(t_tokens // tm,),
        in_specs=[pl.BlockSpec((tm, D_MODEL), lambda i: (i, 0)), full((1, D_MODEL)),
                  ospec, ospec, ospec,
                  full((D_MODEL, 3 * D_MODEL)), full((1, 3 * D_MODEL)),
                  full((WIDTH, D_MODEL)), full((WIDTH, D_MODEL)), full((WIDTH, D_MODEL)),
                  full((D_MODEL, D_MODEL))],
        out_specs=pl.BlockSpec((tm, D_MODEL), lambda i: (i, 0)),
        compiler_params=_params(("parallel",)),
        name="gated_merge",
    )(h, g, oa, ob, oc, w_gate, b_gate, w_a, w_b, w_c, w_out)


_CAND_COUNTS = tuple(PEER_TOPK // (i + 1) for i in range(PEER_TOPK))
_N_CAND = sum(_CAND_COUNTS)
_CAND_ROWS = 56


def _top16_rows(src, dst_ref):
    cur = src
    for r in range(PEER_TOPK):
        mx = jnp.max(cur, axis=0, keepdims=True)
        dst_ref[r:r + 1, :] = mx
        if r + 1 < PEER_TOPK:
            cur = jnp.where(cur == mx, NEG_INF, cur)


def _peer_kernel_plain(h_ref, g_ref, wq_ref, k1_ref, k2_ref, u_ref, vt_ref, out_ref, *scratch, ce):
    _peer_body(h_ref, g_ref, wq_ref, k1_ref, k2_ref, u_ref, vt_ref, None, out_ref, *scratch, ce=ce)


def _peer_kernel_final(h_ref, g_ref, wq_ref, k1_ref, k2_ref, u_ref, vt_ref, fg_ref, out_ref, *scratch, ce):
    _peer_body(h_ref, g_ref, wq_ref, k1_ref, k2_ref, u_ref, vt_ref, fg_ref, out_ref, *scratch, ce=ce)


def _peer_body(h_ref, g_ref, wq_ref, k1_ref, k2_ref, u_ref, vt_ref, fg_ref, out_ref,
               xn_s, s1_s, e1_s, s2_s, e2_s, tau_s, v1_s, v2_s, cand_s, top_s, y_s, w_s, *, ce):
    j = pl.program_id(1)
    nj = pl.num_programs(1)
    tm = h_ref.shape[0]
    a_per_step = ce // N_KEYS

    @pl.when(j == 0)
    def _route():
        xn = _rms(h_ref[...], g_ref[...]).astype(BF16)
        xn_s[...] = xn
        y_s[...] = jnp.zeros_like(y_s)
        cand_s[...] = jnp.full(cand_s.shape, NEG_INF, F32)
        for hd in range(PEER_HEADS):
            q1 = _dot(xn, wq_ref[:, (2 * hd) * N_KEYS:(2 * hd + 1) * N_KEYS]).astype(BF16)
            q2 = _dot(xn, wq_ref[:, (2 * hd + 1) * N_KEYS:(2 * hd + 2) * N_KEYS]).astype(BF16)
            s1 = _dot_nt(k1_ref[...], q1)
            s2 = _dot_nt(k2_ref[...], q2)
            _top16_rows(s1, v1_s)
            _top16_rows(s2, v2_s)
            off = 0
            for i, cnt in enumerate(_CAND_COUNTS):
                cand_s[off:off + cnt, :] = v1_s[i:i + 1, :] + v2_s[0:cnt, :]
                off += cnt
            _top16_rows(cand_s[...], top_s)
            top = top_s[...]
            mx = top[0:1, :]
            z = jnp.sum(jnp.exp(top - mx), axis=0, keepdims=True)
            s1_s[hd] = s1
            s2_s[hd] = s2
            e1_s[hd] = jnp.exp(s1 - v1_s[0:1, :])
            e2_s[hd] = jnp.exp(s2 - v2_s[0:1, :]) / z
            tau_s[hd:hd + 1, :] = top[PEER_TOPK - 1:PEER_TOPK, :]

    act = _dot_nt(u_ref[...], xn_s[...])
    for aa in range(a_per_step):
        a = j * a_per_step + aa
        coef = jnp.zeros((N_KEYS, tm), F32)
        for hd in range(PEER_HEADS):
            s = s2_s[hd] + s1_s[hd, pl.ds(a, 1), :]
            e = e2_s[hd] * e1_s[hd, pl.ds(a, 1), :]
            coef = coef + jnp.where(s >= tau_s[hd:hd + 1, :], e, 0.0)
        x = act[aa * N_KEYS:(aa + 1) * N_KEYS, :]
        gelu = 0.5 * x * (1.0 + lax.erf(x * math.sqrt(0.5)))
        w_s[aa * N_KEYS:(aa + 1) * N_KEYS, :] = (coef * gelu).astype(BF16)
    y_s[...] += _dot(vt_ref[...], w_s[...])

    @pl.when(j == nj - 1)
    def _finish():
        y = h_ref[...] + y_s[...].T
        if fg_ref is not None:
            y = _rms(y, fg_ref[...])
        out_ref[...] = y


def _peer(h, g, wq, k1, k2, u, vt, final_g=None, tm=256, ce=1024):
    t_tokens = h.shape[0]
    full = lambda shape: pl.BlockSpec(shape, lambda i, j: (0, 0))
    in_specs = [pl.BlockSpec((tm, D_MODEL), lambda i, j: (i, 0)), full((1, D_MODEL)),
                full((D_MODEL, 2 * PEER_HEADS * N_KEYS)), full((N_KEYS, N_KEYS)), full((N_KEYS, N_KEYS)),
                pl.BlockSpec((ce, D_MODEL), lambda i, j: (j, 0)),
                pl.BlockSpec((D_MODEL, ce), lambda i, j: (0, j))]
    args = [h, g, wq, k1, k2, u, vt]
    kern = _peer_kernel_plain
    if final_g is not None:
        in_specs.append(full((1, D_MODEL)))
        args.append(final_g)
        kern = _peer_kernel_final
    head_buf = pltpu.VMEM((PEER_HEADS, N_KEYS, tm), F32)
    return pl.pallas_call(
        functools.partial(kern, ce=ce),
        out_shape=jax.ShapeDtypeStruct((t_tokens, D_MODEL), F32),
        grid=(t_tokens // tm, N_EXPERTS // ce),
        in_specs=in_specs,
        out_specs=pl.BlockSpec((tm, D_MODEL), lambda i, j: (i, 0)),
        scratch_shapes=[pltpu.VMEM((tm, D_MODEL), BF16), head_buf, head_buf, head_buf, head_buf,
                        pltpu.VMEM((PEER_HEADS, tm), F32),
                        pltpu.VMEM((PEER_TOPK, tm), F32), pltpu.VMEM((PEER_TOPK, tm), F32),
                        pltpu.VMEM((_CAND_ROWS, tm), F32), pltpu.VMEM((PEER_TOPK, tm), F32),
                        pltpu.VMEM((D_MODEL, tm), F32), pltpu.VMEM((ce, tm), BF16)],
        compiler_params=_params(("parallel", "arbitrary")),
        name="peer_ffn",
    )(*args)


def _rope_tables(positions):
    inv_freq = ROPE_THETA ** (-jnp.arange(0, HEAD_DIM, 2, dtype=F32) / HEAD_DIM)
    ang = positions.astype(F32).reshape(-1, 1) * inv_freq
    cos, sin = jnp.cos(ang), jnp.sin(ang)
    return jnp.concatenate([cos] * 4, axis=-1), jnp.concatenate([-sin, sin, -sin, sin], axis=-1)


def _split_w_in(w_in):
    sizes = (WIDTH,) * 6 + (N_HEADS, WIDTH, C_KV_HEADS * HEAD_DIM, C_KV_HEADS * HEAD_DIM, 3 * D_MODEL)
    parts, start = [], 0
    for size in sizes:
        parts.append(w_in[:, start:start + size])
        start += size
    return parts


_C_HEAD_ORDER = (0, 3, 1, 4, 2, 5)


def _permute_heads(w, axis):
    shape = w.shape
    w = w.reshape(shape[:axis] + (N_HEADS, HEAD_DIM) + shape[axis + 1:])
    w = jnp.take(w, jnp.array(_C_HEAD_ORDER), axis=axis)
    return w.reshape(shape)


def _to_classes(x, batch, dil):
    c = x.shape[-1]
    seq = x.shape[0] // batch
    return x.reshape(batch, seq // dil, dil, c).transpose(0, 2, 1, 3).reshape(batch * dil, seq // dil, c)


def _from_classes(x, batch, dil):
    _, length, c = x.shape
    return x.reshape(batch, dil, length, c).transpose(0, 2, 1, 3).reshape(batch * length * dil, c)


def _mixer(h, cos4, sin4, batch, norm_g, w_in, b_forget, sinks, b_gate, w_br_a, w_br_b, w_br_c, w_out):
    t_tokens = h.shape[0]
    seq = t_tokens // batch
    qa, ka, va, qb, kb, vb, fb, qc, kc, vc, wg = _split_w_in(w_in)
    w_f = jnp.pad(fb, ((0, 0), (0, LANE - N_HEADS)))
    w_att = jnp.concatenate([qa, ka, va, qb, kb, vb, _permute_heads(qc, 1), kc, vc, w_f], axis=1).astype(BF16)
    qkv, f = _in_proj(h, norm_g.reshape(1, -1), w_att, cos4, sin4)

    outs, lses = [], []
    qkv_a = qkv[:, :COL_QB * LANE]
    for dil in A_DILATIONS:
        x = qkv_a.reshape(batch, seq, -1) if dil == 1 else _to_classes(qkv_a, batch, dil)
        o, lse = _banded(x, COL_QA, COL_KA, COL_VA, A_MAX_DIST, kv_shared=False)
        if dil == 1:
            outs.append(o.reshape(t_tokens, WIDTH))
            lses.append(lse.reshape(t_tokens, WIDTH))
        else:
            outs.append(_from_classes(o, batch, dil))
            lses.append(_from_classes(lse, batch, dil))
    oa = _combine(outs, lses)

    qkv3 = qkv.reshape(batch, seq, -1)
    sink_tab = jnp.take(sinks.astype(F32), jnp.array(_C_HEAD_ORDER))
    (oc,) = _banded(qkv3, COL_QC, COL_KC, COL_VC, C_MAX_DIST, kv_shared=True, sink=sink_tab, want_lse=False)
    oc = oc.reshape(t_tokens, WIDTH)

    bf_row = jnp.pad(b_forget.astype(F32), (0, LANE - N_HEADS)).reshape(1, LANE)
    augq, augk = _cum_aug(f.reshape(batch, seq, LANE), bf_row)
    ob = _fox(qkv3, augq, augk).reshape(t_tokens, WIDTH)

    return _merge(h, norm_g.reshape(1, -1), oa, ob, oc, wg.astype(BF16), b_gate.reshape(1, -1).astype(F32),
                  w_br_a.astype(BF16), w_br_b.astype(BF16), _permute_heads(w_br_c, 0).astype(BF16),
                  w_out.astype(BF16))


def kernel(x, positions, norm1_g, w_in, b_forget, sinks, b_gate, w_br_a, w_br_b, w_br_c, w_out, norm2_g,
           peer_wq, peer_k1, peer_k2, peer_u, peer_v, final_g):
    batch, seq, _ = x.shape
    depth = w_in.shape[0]
    cos4, sin4 = _rope_tables(positions)
    h = x.reshape(batch * seq, D_MODEL)
    for l in range(depth):
        h = _mixer(h, cos4, sin4, batch, norm1_g[l], w_in[l], b_forget[l], sinks[l], b_gate[l],
                   w_br_a[l], w_br_b[l], w_br_c[l], w_out[l])
        h = _peer(h, norm2_g[l].reshape(1, -1), peer_wq[l].astype(BF16), peer_k1[l].astype(BF16),
                  peer_k2[l].astype(BF16), peer_u[l].astype(BF16), peer_v[l].T.astype(BF16),
                  final_g=final_g.reshape(1, -1) if l == depth - 1 else None)
    return h.reshape(batch, seq, D_MODEL)
```

```python
import functools
import math

import jax
import jax.numpy as jnp
import numpy as np
from jax import lax
from jax.experimental import pallas as pl
from jax.experimental.pallas import tpu as pltpu

F32 = jnp.float32
BF16 = jnp.bfloat16

D_MODEL = 1024
HEAD_DIM = 64
N_HEADS = 6
N_PAIRS = N_HEADS // 2
C_KV_HEADS = 2
A_DILATIONS = (1, 4, 16)
A_MAX_DIST = 128
C_MAX_DIST = 127
BLOCK = 128
LANE = 128
ROPE_THETA = 10000.0
RMS_EPS = 1e-6
NEG_INF = -1e30
PEER_HEADS = 8
N_KEYS = 128
N_EXPERTS = N_KEYS * N_KEYS
PEER_TOPK = 16
WIDTH = N_HEADS * HEAD_DIM

COL_QA, COL_KA, COL_VA = 0, 3, 6
COL_QB, COL_KB, COL_VB = 9, 12, 15
COL_QC, COL_KC, COL_VC = 18, 21, 22
N_QKV_BLOCKS = 23
COL_F = 23
ROPE_BLOCKS = (0, 1, 2, 3, 4, 5, 18, 19, 20, 21)
QUERY_BLOCKS = (0, 1, 2, 9, 10, 11, 18, 19, 20)
QK_SCALE = HEAD_DIM ** -0.5
LOG2E = math.log2(math.e)

VMEM_LIMIT = 56 * 1024 * 1024


def _params(sem):
    return pltpu.CompilerParams(dimension_semantics=sem, vmem_limit_bytes=VMEM_LIMIT)


def _dot(a, b):
    return jnp.dot(a, b, preferred_element_type=F32)


def _dot_nt(a, b):
    return lax.dot_general(a, b, (((1,), (1,)), ((), ())), preferred_element_type=F32)


def _rms(x, g):
    var = jnp.mean(x * x, axis=-1, keepdims=True)
    return x * lax.rsqrt(var + RMS_EPS) * g


def _in_proj_kernel(h_ref, g_ref, w_ref, cos_ref, sin_ref, qkv_ref, f_ref, proj_s):
    xn = _rms(h_ref[...], g_ref[...]).astype(BF16)
    proj_s[...] = _dot(xn, w_ref[...])
    cos = cos_ref[...]
    sin = sin_ref[...]
    lane = lax.broadcasted_iota(jnp.int32, cos.shape, 1)
    first_half = (lane % HEAD_DIM) < (HEAD_DIM // 2)
    for c in range(N_QKV_BLOCKS):
        t = proj_s[:, c * LANE:(c + 1) * LANE]
        if c in ROPE_BLOCKS:
            rot = jnp.where(first_half, pltpu.roll(t, LANE - HEAD_DIM // 2, 1), pltpu.roll(t, HEAD_DIM // 2, 1))
            t = t * cos + rot * sin
        if c in QUERY_BLOCKS:
            t = t * QK_SCALE
        qkv_ref[:, c * LANE:(c + 1) * LANE] = t.astype(BF16)
    f_ref[...] = proj_s[:, COL_F * LANE:(COL_F + 1) * LANE]


def _in_proj(h, g, w_att, cos4, sin4, tm=512):
    t_tokens = h.shape[0]
    n_cols = w_att.shape[1]
    return pl.pallas_call(
        _in_proj_kernel,
        out_shape=(jax.ShapeDtypeStruct((t_tokens, N_QKV_BLOCKS * LANE), BF16),
                   jax.ShapeDtypeStruct((t_tokens, LANE), F32)),
        grid=(t_tokens // tm,),
        in_specs=[pl.BlockSpec((tm, D_MODEL), lambda i: (i, 0)),
                  pl.BlockSpec((1, D_MODEL), lambda i: (0, 0)),
                  pl.BlockSpec((D_MODEL, n_cols), lambda i: (0, 0)),
                  pl.BlockSpec((tm, LANE), lambda i: (i, 0)),
                  pl.BlockSpec((tm, LANE), lambda i: (i, 0))],
        out_specs=(pl.BlockSpec((tm, N_QKV_BLOCKS * LANE), lambda i: (i, 0)),
                   pl.BlockSpec((tm, LANE), lambda i: (i, 0))),
        scratch_shapes=[pltpu.VMEM((tm, n_cols), F32)],
        compiler_params=_params(("parallel",)),
        name="in_proj",
    )(h, g, w_att, cos4, sin4)


def _banded_kernel(*refs, max_dist, qb, kv_shared, has_sink, want_lse):
    refs = list(refs)
    q_ref, kp_ref, kc_ref, vp_ref, vc_ref = refs[:5]
    pos = 5
    sink_ref = None
    if has_sink:
        sink_ref = refs[pos]
        pos += 1
    o_ref = refs[pos]
    pos += 1
    lse_ref = None
    if want_lse:
        lse_ref = refs[pos]
        pos += 1
    kwin, vwin = refs[pos], refs[pos + 1]

    i = pl.program_id(1)
    kwin[0:BLOCK, :] = kp_ref[...]
    kwin[BLOCK:, :] = kc_ref[...]
    vwin[0:BLOCK, :] = vp_ref[...]
    vwin[BLOCK:, :] = vc_ref[...]

    row = lax.broadcasted_iota(jnp.int32, (BLOCK, 2 * BLOCK), 0)
    col = lax.broadcasted_iota(jnp.int32, (BLOCK, 2 * BLOCK), 1)
    dist = row + BLOCK - col
    band = (dist >= 0) & (dist <= max_dist)
    lane = lax.broadcasted_iota(jnp.int32, (1, LANE), 1)
    lower = lane < HEAD_DIM

    def body(sb, carry):
        off = pl.multiple_of(sb * BLOCK, BLOCK)
        valid = band & ((col >= BLOCK) | (i * qb + sb > 0))
        for p in range(N_PAIRS):
            q = q_ref[pl.ds(off, BLOCK), p * LANE:(p + 1) * LANE]
            kv_cols = slice(0, LANE) if kv_shared else slice(p * LANE, (p + 1) * LANE)
            k = kwin[pl.ds(off, 2 * BLOCK), kv_cols]
            v = vwin[pl.ds(off, 2 * BLOCK), kv_cols]
            o_pair = jnp.zeros((BLOCK, LANE), F32)
            lse_pair = jnp.zeros((BLOCK, LANE), F32)
            for hf in range(2):
                sel = lower if hf == 0 else jnp.logical_not(lower)
                qm = jnp.where(sel, q, jnp.zeros_like(q))
                s = _dot_nt(qm, k)
                s = jnp.where(valid, s, NEG_INF)
                m = jnp.max(s, axis=-1, keepdims=True)
                if has_sink:
                    sk = sink_ref[2 * p + hf]
                    m = jnp.maximum(m, sk)
                pr = jnp.exp(s - m)
                den = jnp.sum(pr, axis=-1, keepdims=True)
                if has_sink:
                    den = den + jnp.exp(sk - m)
                o = _dot(pr.astype(BF16), v) / den
                o_pair = jnp.where(sel, o, o_pair)
                if want_lse:
                    lse_pair = jnp.where(sel, m + jnp.log(den), lse_pair)
            o_ref[pl.ds(off, BLOCK), p * LANE:(p + 1) * LANE] = o_pair.astype(BF16)
            if want_lse:
                lse_ref[pl.ds(off, BLOCK), p * LANE:(p + 1) * LANE] = lse_pair
        return carry

    lax.fori_loop(0, qb, body, 0)


def _banded(x, qcol, kcol, vcol, max_dist, kv_shared, sink=None, want_lse=True, qb=4):
    n, length, _ = x.shape
    qb = min(qb, length // BLOCK)
    tq = qb * BLOCK
    kvw = LANE if kv_shared else WIDTH
    kblk = kcol if kv_shared else kcol // N_PAIRS
    vblk = vcol if kv_shared else vcol // N_PAIRS
    in_specs = [
        pl.BlockSpec((None, tq, WIDTH), lambda b, i: (b, i, qcol // N_PAIRS)),
        pl.BlockSpec((None, BLOCK, kvw), lambda b, i: (b, jnp.maximum(i * qb - 1, 0), kblk)),
        pl.BlockSpec((None, tq, kvw), lambda b, i: (b, i, kblk)),
        pl.BlockSpec((None, BLOCK, kvw), lambda b, i: (b, jnp.maximum(i * qb - 1, 0), vblk)),
        pl.BlockSpec((None, tq, kvw), lambda b, i: (b, i, vblk)),
    ]
    args = [x, x, x, x, x]
    if sink is not None:
        in_specs.append(pl.BlockSpec(memory_space=pltpu.SMEM))
        args.append(sink)
    out_shape = [jax.ShapeDtypeStruct((n, length, WIDTH), BF16)]
    out_specs = [pl.BlockSpec((None, tq, WIDTH), lambda b, i: (b, i, 0))]
    if want_lse:
        out_shape.append(jax.ShapeDtypeStruct((n, length, WIDTH), F32))
        out_specs.append(pl.BlockSpec((None, tq, WIDTH), lambda b, i: (b, i, 0)))
    kern = functools.partial(_banded_kernel, max_dist=max_dist, qb=qb, kv_shared=kv_shared,
                             has_sink=sink is not None, want_lse=want_lse)
    return pl.pallas_call(
        kern,
        out_shape=tuple(out_shape),
        grid=(n, length // tq),
        in_specs=in_specs,
        out_specs=tuple(out_specs),
        scratch_shapes=[pltpu.VMEM((tq + BLOCK, kvw), BF16), pltpu.VMEM((tq + BLOCK, kvw), BF16)],
        compiler_params=_params(("parallel", "parallel")),
        name="banded_attention",
    )(*args)


def _combine_kernel(o1_ref, o2_ref, o3_ref, l1_ref, l2_ref, l3_ref, out_ref):
    l1, l2, l3 = l1_ref[...], l2_ref[...], l3_ref[...]
    m = jnp.maximum(jnp.maximum(l1, l2), l3)
    w1, w2, w3 = jnp.exp(l1 - m), jnp.exp(l2 - m), jnp.exp(l3 - m)
    num = w1 * o1_ref[...].astype(F32) + w2 * o2_ref[...].astype(F32) + w3 * o3_ref[...].astype(F32)
    out_ref[...] = (num / (w1 + w2 + w3)).astype(BF16)


def _combine(outs, lses, tm=1024):
    t_tokens = outs[0].shape[0]
    spec = pl.BlockSpec((tm, WIDTH), lambda i: (i, 0))
    return pl.pallas_call(
        _combine_kernel,
        out_shape=jax.ShapeDtypeStruct((t_tokens, WIDTH), BF16),
        grid=(t_tokens // tm,),
        in_specs=[spec] * 6,
        out_specs=spec,
        compiler_params=_params(("parallel",)),
        name="combine_patterns",
    )(*outs, *lses)


CUM_BLOCK = 256


def _cum_kernel(f_ref, bf_ref, augq_ref, augk_ref, carry_s):
    seq = f_ref.shape[0]

    @pl.when(pl.program_id(1) == 0)
    def _():
        carry_s[...] = jnp.zeros_like(carry_s)

    r = lax.broadcasted_iota(jnp.int32, (LANE, LANE), 0)
    rr = lax.broadcasted_iota(jnp.int32, (CUM_BLOCK, CUM_BLOCK), 0)
    cc = lax.broadcasted_iota(jnp.int32, (CUM_BLOCK, CUM_BLOCK), 1)
    tril = jnp.where(rr >= cc, 1.0, 0.0).astype(F32)
    lane = lax.broadcasted_iota(jnp.int32, (CUM_BLOCK, LANE), 1) % HEAD_DIM
    bias = bf_ref[...]

    def body(blk, carry):
        off = pl.multiple_of(blk * CUM_BLOCK, CUM_BLOCK)
        x = f_ref[pl.ds(off, CUM_BLOCK), :] + bias
        log_f = jnp.minimum(x, 0.0) - jnp.log1p(jnp.exp(-jnp.abs(x)))
        cum_all = jnp.dot(tril, log_f, preferred_element_type=F32, precision=lax.Precision.HIGHEST) + carry
        for head in range(N_HEADS):
            sel = jnp.where(r == head, 1.0, 0.0).astype(F32)
            cum = jnp.dot(cum_all, sel, preferred_element_type=F32, precision=lax.Precision.HIGHEST)
            c1 = cum.astype(BF16).astype(F32)
            c2 = (cum - c1).astype(BF16).astype(F32)
            c3 = (cum - c1 - c2).astype(BF16).astype(F32)
            piece = jnp.where(lane % 3 == 0, c1, jnp.where(lane % 3 == 1, c2, c3))
            one = jnp.ones_like(cum)
            zero = jnp.zeros_like(cum)
            aq = jnp.where(lane < 3, piece, jnp.where(lane < 6, one, zero))
            ak = jnp.where(lane < 3, one, jnp.where(lane < 6, -piece, zero))
            augq_ref[head, pl.ds(off, CUM_BLOCK), :] = aq.astype(BF16)
            augk_ref[head, pl.ds(off, CUM_BLOCK), :] = ak.astype(BF16)
        return cum_all[CUM_BLOCK - 1:CUM_BLOCK, :]

    carry_s[...] = lax.fori_loop(0, seq // CUM_BLOCK, body, carry_s[...])


def _cum_aug(f, b_forget_row, ts=2048):
    b, seq, _ = f.shape
    ts = min(ts, seq)
    out = jax.ShapeDtypeStruct((b, N_HEADS, seq, LANE), BF16)
    ospec = pl.BlockSpec((None, N_HEADS, ts, LANE), lambda bi, si: (bi, 0, si, 0))
    return pl.pallas_call(
        _cum_kernel,
        out_shape=(out, out),
        grid=(b, seq // ts),
        in_specs=[pl.BlockSpec((None, ts, LANE), lambda bi, si: (bi, si, 0)),
                  pl.BlockSpec((1, LANE), lambda bi, si: (0, 0))],
        out_specs=(ospec, ospec),
        scratch_shapes=[pltpu.VMEM((1, LANE), F32)],
        compiler_params=_params(("parallel", "arbitrary")),
        name="forget_cumsum",
    )(f, b_forget_row)


def _fox_kernel(q_ref, k_ref, v_ref, aq_ref, ak_ref, o_ref, kaug, qaug, *, tq, tk):
    qi = pl.program_id(2)
    lane = lax.broadcasted_iota(jnp.int32, (1, LANE), 1)
    lower = lane < HEAD_DIM

    @pl.when(qi == 0)
    def _():
        k = k_ref[...]
        kaug[0] = jnp.where(lower, k, ak_ref[0])
        kaug[1] = jnp.where(lower, ak_ref[1], k)

    q = q_ref[...]
    qaug[0] = jnp.where(lower, q, aq_ref[0])
    qaug[1] = jnp.where(lower, aq_ref[1], q)
    row = lax.broadcasted_iota(jnp.int32, (tq, tk), 0)
    col = lax.broadcasted_iota(jnp.int32, (tq, tk), 1)
    n_sub = tq // tk

    def step(kb, carry, mask_shift):
        off = pl.multiple_of(kb * tk, tk)
        v = v_ref[pl.ds(off, tk), :]
        new = []
        for hf in range(2):
            m, l, acc = carry[hf]
            s = _dot_nt(qaug[hf], kaug[hf, pl.ds(off, tk), :])
            if mask_shift is not None:
                s = jnp.where(col + mask_shift <= row, s, NEG_INF)
            m_new = jnp.maximum(m, jnp.max(s, axis=-1, keepdims=True))
            alpha = jnp.exp(m - m_new)
            pr = jnp.exp(s - m_new)
            l = alpha * l + jnp.sum(pr, axis=-1, keepdims=True)
            acc = alpha * acc + _dot(pr.astype(BF16), v)
            new.append((m_new, l, acc))
        return tuple(new)

    one = (jnp.full((tq, 1), NEG_INF, F32), jnp.zeros((tq, 1), F32), jnp.zeros((tq, LANE), F32))
    carry = lax.fori_loop(0, qi * n_sub, functools.partial(step, mask_shift=None), (one, one))
    for d in range(n_sub):
        carry = step(qi * n_sub + d, carry, d * tk)
    (_, l0, acc0), (_, l1, acc1) = carry
    o_ref[...] = jnp.where(lower, acc0 / l0, acc1 / l1).astype(BF16)


def _fox(qkv, augq, augk, tq=512, tk=256):
    b, seq, _ = qkv.shape
    return pl.pallas_call(
        functools.partial(_fox_kernel, tq=tq, tk=tk),
        out_shape=jax.ShapeDtypeStruct((b, seq, WIDTH), BF16),
        grid=(b, N_PAIRS, seq // tq),
        in_specs=[pl.BlockSpec((None, tq, LANE), lambda bi, p, i: (bi, i, COL_QB + p)),
                  pl.BlockSpec((None, seq, LANE), lambda bi, p, i: (bi, 0, COL_KB + p)),
                  pl.BlockSpec((None, seq, LANE), lambda bi, p, i: (bi, 0, COL_VB + p)),
                  pl.BlockSpec((None, 2, tq, LANE), lambda bi, p, i: (bi, p, i, 0)),
                  pl.BlockSpec((None, 2, seq, LANE), lambda bi, p, i: (bi, p, 0, 0))],
        out_specs=pl.BlockSpec((None, tq, LANE), lambda bi, p, i: (bi, i, p)),
        scratch_shapes=[pltpu.VMEM((2, seq, LANE), BF16), pltpu.VMEM((2, tq, LANE), BF16)],
        compiler_params=_params(("parallel", "parallel", "arbitrary")),
        name="forgetting_attention",
    )(qkv, qkv, qkv, augq, augk)


def _merge_kernel(h_ref, g_ref, oa_ref, ob_ref, oc_ref, wg_ref, bg_ref, wa_ref, wb_ref, wc_ref, wo_ref, out_ref):
    h = h_ref[...]
    xn = _rms(h, g_ref[...]).astype(BF16)
    merged = jnp.zeros(h.shape, F32)
    for idx, (o_ref, w_ref) in enumerate(((oa_ref, wa_ref), (ob_ref, wb_ref), (oc_ref, wc_ref))):
        cols = slice(idx * D_MODEL, (idx + 1) * D_MODEL)
        gate = jax.nn.sigmoid(_dot(xn, wg_ref[:, cols]) + bg_ref[:, cols])
        merged = merged + gate * _dot(o_ref[...], w_ref[...])
    out_ref[...] = h + _dot(merged.astype(BF16), wo_ref[...])


def _merge(h, g, oa, ob, oc, w_gate, b_gate, w_a, w_b, w_c, w_out, tm=256):
    t_tokens = h.shape[0]
    full = lambda shape: pl.BlockSpec(shape, lambda i: (0, 0))
    ospec = pl.BlockSpec((tm, WIDTH), lambda i: (i, 0))
    return pl.pallas_call(
        _merge_kernel,
        out_shape=jax.ShapeDtypeStruct((t_tokens, D_MODEL), F32),
        grid=(t_tokens // tm,),
        in_specs=[pl.BlockSpec((tm, D_MODEL), lambda i: (i, 0)), full((1, D_MODEL)),
                  ospec, ospec, ospec,
                  full((D_MODEL, 3 * D_MODEL)), full((1, 3 * D_MODEL)),
                  full((WIDTH, D_MODEL)), full((WIDTH, D_MODEL)), full((WIDTH, D_MODEL)),
                  full((D_MODEL, D_MODEL))],
        out_specs=pl.BlockSpec((tm, D_MODEL), lambda i: (i, 0)),
        compiler_params=_params(("parallel",)),
        name="gated_merge",
    )(h, g, oa, ob, oc, w_gate, b_gate, w_a, w_b, w_c, w_out)


_CAND_COUNTS = tuple(PEER_TOPK // (i + 1) for i in range(PEER_TOPK))
_N_CAND = sum(_CAND_COUNTS)
_CAND_ROWS = 56
PIECE = 256


def _top16_rows(src, dst_ref):
    cur = src
    for r in range(PEER_TOPK):
        mx = jnp.max(cur, axis=0, keepdims=True)
        dst_ref[r:r + 1, :] = mx
        if r + 1 < PEER_TOPK:
            cur = jnp.where(cur == mx, NEG_INF, cur)


def _peer_kernel_plain(h_ref, g_ref, wq_ref, k1_ref, k2_ref, u_ref, vt_ref, out_ref, *scratch, ce, th):
    _peer_body(h_ref, g_ref, wq_ref, k1_ref, k2_ref, u_ref, vt_ref, None, out_ref, *scratch, ce=ce, th=th)


def _peer_kernel_final(h_ref, g_ref, wq_ref, k1_ref, k2_ref, u_ref, vt_ref, fg_ref, out_ref, *scratch, ce, th):
    _peer_body(h_ref, g_ref, wq_ref, k1_ref, k2_ref, u_ref, vt_ref, fg_ref, out_ref, *scratch, ce=ce, th=th)


def _peer_body(h_ref, g_ref, wq_ref, k1_ref, k2_ref, u_ref, vt_ref, fg_ref, out_ref,
               xn_s, s1_s, s2_s, tau_s, v1_s, v2_s, cand_s, top_s, y_s, w_s, *, ce, th):
    j = pl.program_id(1)
    nj = pl.num_programs(1)
    tm = h_ref.shape[0]
    a_per_step = ce // N_KEYS
    halves = [slice(t0, t0 + th) for t0 in range(0, tm, th)]

    def fill_candidates():
        off = 0
        for i, cnt in enumerate(_CAND_COUNTS):
            cand_s[off:off + cnt, :] = v1_s[i:i + 1, :] + v2_s[0:cnt, :]
            off += cnt

    @pl.when(j == 0)
    def _route():
        xn_s[...] = _rms(h_ref[...], g_ref[...]).astype(BF16)
        y_s[...] = jnp.zeros_like(y_s)
        cand_s[...] = jnp.full(cand_s.shape, NEG_INF, F32)
        for tok in halves:
            xn = xn_s[tok, :]
            for hd in range(PEER_HEADS):
                q1 = _dot(xn, wq_ref[:, (2 * hd) * N_KEYS:(2 * hd + 1) * N_KEYS]).astype(BF16)
                q2 = _dot(xn, wq_ref[:, (2 * hd + 1) * N_KEYS:(2 * hd + 2) * N_KEYS]).astype(BF16)
                s1 = _dot_nt(k1_ref[...], q1)
                s2 = _dot_nt(k2_ref[...], q2)
                _top16_rows(s1, v1_s)
                _top16_rows(s2, v2_s)
                fill_candidates()
                _top16_rows(cand_s[...], top_s)
                top = top_s[...]
                mx = top[0:1, :]
                shift = mx + jnp.log(jnp.sum(jnp.exp(top - mx), axis=0, keepdims=True))
                s1_s[hd, :, tok] = (s1 - shift) * LOG2E - 1.0
                s2_s[hd, :, tok] = s2 * LOG2E
                v1_s[...] = (v1_s[...] - shift) * LOG2E - 1.0
                v2_s[...] = v2_s[...] * LOG2E
                fill_candidates()
                _top16_rows(cand_s[...], top_s)
                tau_s[hd:hd + 1, tok] = top_s[PEER_TOPK - 1:PEER_TOPK, :]

    n_pieces = ce // PIECE

    def act_piece(tok, p):
        return _dot_nt(u_ref[p * PIECE:(p + 1) * PIECE, :], xn_s[tok, :])

    def mix_piece(tok, p, act):
        for aa in range(PIECE // N_KEYS):
            a = j * a_per_step + p * (PIECE // N_KEYS) + aa
            coef = jnp.zeros((N_KEYS, th), F32)
            for hd in range(PEER_HEADS):
                s = s2_s[hd, :, tok] + s1_s[hd, pl.ds(a, 1), tok]
                coef = coef + jnp.where(s >= tau_s[hd:hd + 1, tok], jnp.exp2(s), 0.0)
            x = act[aa * N_KEYS:(aa + 1) * N_KEYS, :]
            gelu2 = x * (1.0 + lax.erf(x * math.sqrt(0.5)))
            rows = slice(p * PIECE + aa * N_KEYS, p * PIECE + (aa + 1) * N_KEYS)
            w_s[rows, tok] = (coef * gelu2).astype(BF16)

    def out_piece(tok, p):
        return _dot(vt_ref[:, p * PIECE:(p + 1) * PIECE], w_s[p * PIECE:(p + 1) * PIECE, tok])

    acts = [act_piece(halves[0], p) for p in range(n_pieces)]
    for hi, tok in enumerate(halves):
        nxt = halves[hi + 1] if hi + 1 < len(halves) else None
        prv = halves[hi - 1] if hi > 0 else None
        nxt_acts, y_prev = [], None
        for p in range(n_pieces):
            mix_piece(tok, p, acts[p])
            if nxt is not None:
                nxt_acts.append(act_piece(nxt, p))
            if prv is not None:
                yp = out_piece(prv, p)
                y_prev = yp if y_prev is None else y_prev + yp
        if prv is not None:
            y_s[:, prv] += y_prev
        acts = nxt_acts
    y_last = out_piece(halves[-1], 0)
    for p in range(1, n_pieces):
        y_last = y_last + out_piece(halves[-1], p)
    y_s[:, halves[-1]] += y_last

    @pl.when(j == nj - 1)
    def _finish():
        y = h_ref[...] + y_s[...].T
        if fg_ref is not None:
            y = _rms(y, fg_ref[...])
        out_ref[...] = y


def _peer(h, g, wq, k1, k2, u, vt, final_g=None, tm=512, ce=1024, th=256):
    t_tokens = h.shape[0]
    tm = min(tm, t_tokens)
    full = lambda shape: pl.BlockSpec(shape, lambda i, j: (0, 0))
    in_specs = [pl.BlockSpec((tm, D_MODEL), lambda i, j: (i, 0)), full((1, D_MODEL)),
                full((D_MODEL, 2 * PEER_HEADS * N_KEYS)), full((N_KEYS, N_KEYS)), full((N_KEYS, N_KEYS)),
                pl.BlockSpec((ce, D_MODEL), lambda i, j: (j, 0)),
                pl.BlockSpec((D_MODEL, ce), lambda i, j: (0, j))]
    args = [h, g, wq, k1, k2, u, vt]
    kern = _peer_kernel_plain
    if final_g is not None:
        in_specs.append(full((1, D_MODEL)))
        args.append(final_g)
        kern = _peer_kernel_final
    head_buf = pltpu.VMEM((PEER_HEADS, N_KEYS, tm), F32)
    return pl.pallas_call(
        functools.partial(kern, ce=ce, th=th),
        out_shape=jax.ShapeDtypeStruct((t_tokens, D_MODEL), F32),
        grid=(t_tokens // tm, N_EXPERTS // ce),
        in_specs=in_specs,
        out_specs=pl.BlockSpec((tm, D_MODEL), lambda i, j: (i, 0)),
        scratch_shapes=[pltpu.VMEM((tm, D_MODEL), BF16), head_buf, head_buf,
                        pltpu.VMEM((PEER_HEADS, tm), F32),
                        pltpu.VMEM((PEER_TOPK, th), F32), pltpu.VMEM((PEER_TOPK, th), F32),
                        pltpu.VMEM((_CAND_ROWS, th), F32), pltpu.VMEM((PEER_TOPK, th), F32),
                        pltpu.VMEM((D_MODEL, tm), F32), pltpu.VMEM((ce, tm), BF16)],
        compiler_params=_params(("parallel", "arbitrary")),
        name="peer_ffn",
    )(*args)


def _rope_tables(positions):
    inv_freq = ROPE_THETA ** (-jnp.arange(0, HEAD_DIM, 2, dtype=F32) / HEAD_DIM)
    ang = positions.astype(F32).reshape(-1, 1) * inv_freq
    cos, sin = jnp.cos(ang), jnp.sin(ang)
    return jnp.concatenate([cos] * 4, axis=-1), jnp.concatenate([-sin, sin, -sin, sin], axis=-1)


def _split_w_in(w_in):
    sizes = (WIDTH,) * 6 + (N_HEADS, WIDTH, C_KV_HEADS * HEAD_DIM, C_KV_HEADS * HEAD_DIM, 3 * D_MODEL)
    parts, start = [], 0
    for size in sizes:
        parts.append(w_in[:, start:start + size])
        start += size
    return parts


_C_HEAD_ORDER = (0, 3, 1, 4, 2, 5)


def _permute_heads(w, axis):
    shape = w.shape
    w = w.reshape(shape[:axis] + (N_HEADS, HEAD_DIM) + shape[axis + 1:])
    w = jnp.take(w, jnp.array(_C_HEAD_ORDER), axis=axis)
    return w.reshape(shape)


def _to_classes(x, batch, dil):
    c = x.shape[-1]
    seq = x.shape[0] // batch
    return x.reshape(batch, seq // dil, dil, c).transpose(0, 2, 1, 3).reshape(batch * dil, seq // dil, c)


def _from_classes(x, batch, dil):
    _, length, c = x.shape
    return x.reshape(batch, dil, length, c).transpose(0, 2, 1, 3).reshape(batch * length * dil, c)


def _mixer(h, cos4, sin4, batch, norm_g, w_in, b_forget, sinks, b_gate, w_br_a, w_br_b, w_br_c, w_out):
    t_tokens = h.shape[0]
    seq = t_tokens // batch
    qa, ka, va, qb, kb, vb, fb, qc, kc, vc, wg = _split_w_in(w_in)
    w_f = jnp.pad(fb, ((0, 0), (0, LANE - N_HEADS)))
    w_att = jnp.concatenate([qa, ka, va, qb, kb, vb, _permute_heads(qc, 1), kc, vc, w_f], axis=1).astype(BF16)
    qkv, f = _in_proj(h, norm_g.reshape(1, -1), w_att, cos4, sin4)

    outs, lses = [], []
    qkv_a = qkv[:, :COL_QB * LANE]
    for dil in A_DILATIONS:
        x = qkv_a.reshape(batch, seq, -1) if dil == 1 else _to_classes(qkv_a, batch, dil)
        o, lse = _banded(x, COL_QA, COL_KA, COL_VA, A_MAX_DIST, kv_shared=False)
        if dil == 1:
            outs.append(o.reshape(t_tokens, WIDTH))
            lses.append(lse.reshape(t_tokens, WIDTH))
        else:
            outs.append(_from_classes(o, batch, dil))
            lses.append(_from_classes(lse, batch, dil))
    oa = _combine(outs, lses)

    qkv3 = qkv.reshape(batch, seq, -1)
    sink_tab = jnp.take(sinks.astype(F32), jnp.array(_C_HEAD_ORDER))
    (oc,) = _banded(qkv3, COL_QC, COL_KC, COL_VC, C_MAX_DIST, kv_shared=True, sink=sink_tab, want_lse=False)
    oc = oc.reshape(t_tokens, WIDTH)

    bf_row = jnp.pad(b_forget.astype(F32), (0, LANE - N_HEADS)).reshape(1, LANE)
    augq, augk = _cum_aug(f.reshape(batch, seq, LANE), bf_row)
    ob = _fox(qkv3, augq, augk).reshape(t_tokens, WIDTH)

    return _merge(h, norm_g.reshape(1, -1), oa, ob, oc, wg.astype(BF16), b_gate.reshape(1, -1).astype(F32),
                  w_br_a.astype(BF16), w_br_b.astype(BF16), _permute_heads(w_br_c, 0).astype(BF16),
                  w_out.astype(BF16))


def kernel(x, positions, norm1_g, w_in, b_forget, sinks, b_gate, w_br_a, w_br_b, w_br_c, w_out, norm2_g,
           peer_wq, peer_k1, peer_k2, peer_u, peer_v, final_g):
    batch, seq, _ = x.shape
    depth = w_in.shape[0]
    cos4, sin4 = _rope_tables(positions)
    h = x.reshape(batch * seq, D_MODEL)
    for l in range(depth):
        h = _mixer(h, cos4, sin4, batch, norm1_g[l], w_in[l], b_forget[l], sinks[l], b_gate[l],
                   w_br_a[l], w_br_b[l], w_br_c[l], w_out[l])
        h = _peer(h, norm2_g[l].reshape(1, -1), peer_wq[l].astype(BF16), peer_k1[l].astype(BF16),
                  peer_k2[l].astype(BF16), peer_u[l].astype(BF16), peer_v[l].T.astype(BF16),
                  final_g=final_g.reshape(1, -1) if l == depth - 1 else None)
    return h.reshape(batch, seq, D_MODEL)
```

```python
import functools
import math

import jax
import jax.numpy as jnp
import numpy as np
from jax import lax
from jax.experimental import pallas as pl
from jax.experimental.pallas import tpu as pltpu

F32 = jnp.float32
BF16 = jnp.bfloat16

D_MODEL = 1024
HEAD_DIM = 64
N_HEADS = 6
N_PAIRS = N_HEADS // 2
C_KV_HEADS = 2
A_DILATIONS = (1, 4, 16)
A_MAX_DIST = 128
C_MAX_DIST = 127
BLOCK = 128
LANE = 128
ROPE_THETA = 10000.0
RMS_EPS = 1e-6
NEG_INF = -1e30
PEER_HEADS = 8
N_KEYS = 128
N_EXPERTS = N_KEYS * N_KEYS
PEER_TOPK = 16
WIDTH = N_HEADS * HEAD_DIM

COL_QA, COL_KA, COL_VA = 0, 3, 6
COL_QB, COL_KB, COL_VB = 9, 12, 15
COL_QC, COL_KC, COL_VC = 18, 21, 22
N_QKV_BLOCKS = 23
COL_F = 23
ROPE_BLOCKS = (0, 1, 2, 3, 4, 5, 18, 19, 20, 21)
QUERY_BLOCKS = (0, 1, 2, 9, 10, 11, 18, 19, 20)
QK_SCALE = HEAD_DIM ** -0.5
LOG2E = math.log2(math.e)

VMEM_LIMIT = 56 * 1024 * 1024


def _params(sem):
    return pltpu.CompilerParams(dimension_semantics=sem, vmem_limit_bytes=VMEM_LIMIT)


def _dot(a, b):
    return jnp.dot(a, b, preferred_element_type=F32)


def _dot_nt(a, b):
    return lax.dot_general(a, b, (((1,), (1,)), ((), ())), preferred_element_type=F32)


def _rms(x, g):
    var = jnp.mean(x * x, axis=-1, keepdims=True)
    return x * lax.rsqrt(var + RMS_EPS) * g


def _in_proj_kernel(h_ref, g_ref, w_ref, cos_ref, sin_ref, qkv_ref, f_ref, proj_s):
    xn = _rms(h_ref[...], g_ref[...]).astype(BF16)
    proj_s[...] = _dot(xn, w_ref[...])
    cos = cos_ref[...]
    sin = sin_ref[...]
    lane = lax.broadcasted_iota(jnp.int32, cos.shape, 1)
    first_half = (lane % HEAD_DIM) < (HEAD_DIM // 2)
    for c in range(N_QKV_BLOCKS):
        t = proj_s[:, c * LANE:(c + 1) * LANE]
        if c in ROPE_BLOCKS:
            rot = jnp.where(first_half, pltpu.roll(t, LANE - HEAD_DIM // 2, 1), pltpu.roll(t, HEAD_DIM // 2, 1))
            t = t * cos + rot * sin
        if c in QUERY_BLOCKS:
            t = t * QK_SCALE
        qkv_ref[:, c * LANE:(c + 1) * LANE] = t.astype(BF16)
    f_ref[...] = proj_s[:, COL_F * LANE:(COL_F + 1) * LANE]


def _in_proj(h, g, w_att, cos4, sin4, tm=512):
    t_tokens = h.shape[0]
    n_cols = w_att.shape[1]
    return pl.pallas_call(
        _in_proj_kernel,
        out_shape=(jax.ShapeDtypeStruct((t_tokens, N_QKV_BLOCKS * LANE), BF16),
                   jax.ShapeDtypeStruct((t_tokens, LANE), F32)),
        grid=(t_tokens // tm,),
        in_specs=[pl.BlockSpec((tm, D_MODEL), lambda i: (i, 0)),
                  pl.BlockSpec((1, D_MODEL), lambda i: (0, 0)),
                  pl.BlockSpec((D_MODEL, n_cols), lambda i: (0, 0)),
                  pl.BlockSpec((tm, LANE), lambda i: (i, 0)),
                  pl.BlockSpec((tm, LANE), lambda i: (i, 0))],
        out_specs=(pl.BlockSpec((tm, N_QKV_BLOCKS * LANE), lambda i: (i, 0)),
                   pl.BlockSpec((tm, LANE), lambda i: (i, 0))),
        scratch_shapes=[pltpu.VMEM((tm, n_cols), F32)],
        compiler_params=_params(("parallel",)),
        name="in_proj",
    )(h, g, w_att, cos4, sin4)


def _banded_kernel(*refs, max_dist, qb, kv_shared, has_sink, want_lse):
    refs = list(refs)
    q_ref, kp_ref, kc_ref, vp_ref, vc_ref = refs[:5]
    pos = 5
    sink_ref = None
    if has_sink:
        sink_ref = refs[pos]
        pos += 1
    o_ref = refs[pos]
    pos += 1
    lse_ref = None
    if want_lse:
        lse_ref = refs[pos]
        pos += 1
    kwin, vwin = refs[pos], refs[pos + 1]

    i = pl.program_id(1)
    kwin[0:BLOCK, :] = kp_ref[...]
    kwin[BLOCK:, :] = kc_ref[...]
    vwin[0:BLOCK, :] = vp_ref[...]
    vwin[BLOCK:, :] = vc_ref[...]

    row = lax.broadcasted_iota(jnp.int32, (BLOCK, 2 * BLOCK), 0)
    col = lax.broadcasted_iota(jnp.int32, (BLOCK, 2 * BLOCK), 1)
    dist = row + BLOCK - col
    band = (dist >= 0) & (dist <= max_dist)
    lane = lax.broadcasted_iota(jnp.int32, (1, LANE), 1)
    lower = lane < HEAD_DIM

    def body(sb, carry):
        off = pl.multiple_of(sb * BLOCK, BLOCK)
        valid = band & ((col >= BLOCK) | (i * qb + sb > 0))
        for p in range(N_PAIRS):
            q = q_ref[pl.ds(off, BLOCK), p * LANE:(p + 1) * LANE]
            kv_cols = slice(0, LANE) if kv_shared else slice(p * LANE, (p + 1) * LANE)
            k = kwin[pl.ds(off, 2 * BLOCK), kv_cols]
            v = vwin[pl.ds(off, 2 * BLOCK), kv_cols]
            o_pair = jnp.zeros((BLOCK, LANE), F32)
            lse_pair = jnp.zeros((BLOCK, LANE), F32)
            for hf in range(2):
                sel = lower if hf == 0 else jnp.logical_not(lower)
                qm = jnp.where(sel, q, jnp.zeros_like(q))
                s = _dot_nt(qm, k)
                s = jnp.where(valid, s, NEG_INF)
                m = jnp.max(s, axis=-1, keepdims=True)
                if has_sink:
                    sk = sink_ref[2 * p + hf]
                    m = jnp.maximum(m, sk)
                pr = jnp.exp(s - m)
                den = jnp.sum(pr, axis=-1, keepdims=True)
                if has_sink:
                    den = den + jnp.exp(sk - m)
                o = _dot(pr.astype(BF16), v) / den
                o_pair = jnp.where(sel, o, o_pair)
                if want_lse:
                    lse_pair = jnp.where(sel, m + jnp.log(den), lse_pair)
            o_ref[pl.ds(off, BLOCK), p * LANE:(p + 1) * LANE] = o_pair.astype(BF16)
            if want_lse:
                lse_ref[pl.ds(off, BLOCK), p * LANE:(p + 1) * LANE] = lse_pair
        return carry

    lax.fori_loop(0, qb, body, 0)


def _banded(x, qcol, kcol, vcol, max_dist, kv_shared, sink=None, want_lse=True, qb=4):
    n, length, _ = x.shape
    qb = min(qb, length // BLOCK)
    tq = qb * BLOCK
    kvw = LANE if kv_shared else WIDTH
    kblk = kcol if kv_shared else kcol // N_PAIRS
    vblk = vcol if kv_shared else vcol // N_PAIRS
    in_specs = [
        pl.BlockSpec((None, tq, WIDTH), lambda b, i: (b, i, qcol // N_PAIRS)),
        pl.BlockSpec((None, BLOCK, kvw), lambda b, i: (b, jnp.maximum(i * qb - 1, 0), kblk)),
        pl.BlockSpec((None, tq, kvw), lambda b, i: (b, i, kblk)),
        pl.BlockSpec((None, BLOCK, kvw), lambda b, i: (b, jnp.maximum(i * qb - 1, 0), vblk)),
        pl.BlockSpec((None, tq, kvw), lambda b, i: (b, i, vblk)),
    ]
    args = [x, x, x, x, x]
    if sink is not None:
        in_specs.append(pl.BlockSpec(memory_space=pltpu.SMEM))
        args.append(sink)
    out_shape = [jax.ShapeDtypeStruct((n, length, WIDTH), BF16)]
    out_specs = [pl.BlockSpec((None, tq, WIDTH), lambda b, i: (b, i, 0))]
    if want_lse:
        out_shape.append(jax.ShapeDtypeStruct((n, length, WIDTH), F32))
        out_specs.append(pl.BlockSpec((None, tq, WIDTH), lambda b, i: (b, i, 0)))
    kern = functools.partial(_banded_kernel, max_dist=max_dist, qb=qb, kv_shared=kv_shared,
                             has_sink=sink is not None, want_lse=want_lse)
    return pl.pallas_call(
        kern,
        out_shape=tuple(out_shape),
        grid=(n, length // tq),
        in_specs=in_specs,
        out_specs=tuple(out_specs),
        scratch_shapes=[pltpu.VMEM((tq + BLOCK, kvw), BF16), pltpu.VMEM((tq + BLOCK, kvw), BF16)],
        compiler_params=_params(("parallel", "parallel")),
        name="banded_attention",
    )(*args)


def _combine_kernel(o1_ref, o2_ref, o3_ref, l1_ref, l2_ref, l3_ref, out_ref):
    l1, l2, l3 = l1_ref[...], l2_ref[...], l3_ref[...]
    m = jnp.maximum(jnp.maximum(l1, l2), l3)
    w1, w2, w3 = jnp.exp(l1 - m), jnp.exp(l2 - m), jnp.exp(l3 - m)
    num = w1 * o1_ref[...].astype(F32) + w2 * o2_ref[...].astype(F32) + w3 * o3_ref[...].astype(F32)
    out_ref[...] = (num / (w1 + w2 + w3)).astype(BF16)


def _combine(outs, lses, tm=1024):
    t_tokens = outs[0].shape[0]
    spec = pl.BlockSpec((tm, WIDTH), lambda i: (i, 0))
    return pl.pallas_call(
        _combine_kernel,
        out_shape=jax.ShapeDtypeStruct((t_tokens, WIDTH), BF16),
        grid=(t_tokens // tm,),
        in_specs=[spec] * 6,
        out_specs=spec,
        compiler_params=_params(("parallel",)),
        name="combine_patterns",
    )(*outs, *lses)


CUM_BLOCK = 256
BF16_ROWS = 16
UNDERFLOW = 104.0
NORM_SLACK = 1.02


def _cum_kernel(f_ref, bf_ref, augq_ref, augk_ref, carry_s):
    seq = f_ref.shape[0]

    @pl.when(pl.program_id(1) == 0)
    def _():
        carry_s[...] = jnp.zeros_like(carry_s)

    r = lax.broadcasted_iota(jnp.int32, (LANE, LANE), 0)
    rr = lax.broadcasted_iota(jnp.int32, (CUM_BLOCK, CUM_BLOCK), 0)
    cc = lax.broadcasted_iota(jnp.int32, (CUM_BLOCK, CUM_BLOCK), 1)
    tril = jnp.where(rr >= cc, 1.0, 0.0).astype(F32)
    lane = lax.broadcasted_iota(jnp.int32, (CUM_BLOCK, LANE), 1) % HEAD_DIM
    bias = bf_ref[...]

    def body(blk, carry):
        off = pl.multiple_of(blk * CUM_BLOCK, CUM_BLOCK)
        x = f_ref[pl.ds(off, CUM_BLOCK), :] + bias
        log_f = jnp.minimum(x, 0.0) - jnp.log1p(jnp.exp(-jnp.abs(x)))
        cum_all = jnp.dot(tril, log_f, preferred_element_type=F32, precision=lax.Precision.HIGHEST) + carry
        for head in range(N_HEADS):
            sel = jnp.where(r == head, 1.0, 0.0).astype(F32)
            cum = jnp.dot(cum_all, sel, preferred_element_type=F32, precision=lax.Precision.HIGHEST)
            c1 = cum.astype(BF16).astype(F32)
            c2 = (cum - c1).astype(BF16).astype(F32)
            c3 = (cum - c1 - c2).astype(BF16).astype(F32)
            piece = jnp.where(lane % 3 == 0, c1, jnp.where(lane % 3 == 1, c2, c3))
            one = jnp.ones_like(cum)
            zero = jnp.zeros_like(cum)
            aq = jnp.where(lane < 3, piece, jnp.where(lane < 6, one, zero))
            ak = jnp.where(lane < 3, one, jnp.where(lane < 6, -piece, zero))
            augq_ref[head, pl.ds(off, CUM_BLOCK), :] = aq.astype(BF16)
            augk_ref[head, pl.ds(off, CUM_BLOCK), :] = ak.astype(BF16)
        return cum_all[CUM_BLOCK - 1:CUM_BLOCK, :]

    carry_s[...] = lax.fori_loop(0, seq // CUM_BLOCK, body, carry_s[...])


def _cum_aug(f, b_forget_row, ts=2048):
    b, seq, _ = f.shape
    ts = min(ts, seq)
    out = jax.ShapeDtypeStruct((b, N_HEADS, seq, LANE), BF16)
    ospec = pl.BlockSpec((None, N_HEADS, ts, LANE), lambda bi, si: (bi, 0, si, 0))
    return pl.pallas_call(
        _cum_kernel,
        out_shape=(out, out),
        grid=(b, seq // ts),
        in_specs=[pl.BlockSpec((None, ts, LANE), lambda bi, si: (bi, si, 0)),
                  pl.BlockSpec((1, LANE), lambda bi, si: (0, 0))],
        out_specs=(ospec, ospec),
        scratch_shapes=[pltpu.VMEM((1, LANE), F32)],
        compiler_params=_params(("parallel", "arbitrary")),
        name="forget_cumsum",
    )(f, b_forget_row)


def _fox_kernel(q_ref, k_ref, v_ref, aq_ref, ak_ref, o_ref, kaug, qaug, kstat, *, tq, tk):
    qi = pl.program_id(2)
    n_kb = k_ref.shape[0] // tk
    n_sub = tq // tk
    lane = lax.broadcasted_iota(jnp.int32, (1, LANE), 1)
    lower = lane < HEAD_DIM
    hr = lax.broadcasted_iota(jnp.int32, (LANE, LANE), 0) // HEAD_DIM
    hc = lax.broadcasted_iota(jnp.int32, (LANE, LANE), 1) // HEAD_DIM
    same_head = jnp.where(hr == hc, 1.0, 0.0).astype(BF16)

    def max_sq_norm(x):
        xf = x.astype(F32)
        return jnp.max(_dot((xf * xf).astype(BF16), same_head), axis=0, keepdims=True)

    def lane_sum(row_vec, lo, hi):
        return jnp.sum(jnp.where((lane >= lo) & (lane < hi), row_vec.astype(F32), 0.0), axis=1, keepdims=True)

    @pl.when(qi == 0)
    def _():
        k = k_ref[...]
        kaug[0] = jnp.where(lower, k, ak_ref[0])
        kaug[1] = jnp.where(lower, ak_ref[1], k)

        def stats(kb, carry):
            kn0, kn1, nc0, nc1 = carry
            off = pl.multiple_of(kb * tk, tk)
            n2 = max_sq_norm(k_ref[pl.ds(off, tk), :])
            tail = pl.multiple_of(off + tk - BF16_ROWS, BF16_ROWS)
            here = lane == kb
            kn0 = jnp.where(here, n2[:, 0:1], kn0)
            kn1 = jnp.where(here, n2[:, HEAD_DIM:HEAD_DIM + 1], kn1)
            last0 = ak_ref[0, pl.ds(tail, BF16_ROWS), :][BF16_ROWS - 1:BF16_ROWS, :]
            last1 = ak_ref[1, pl.ds(tail, BF16_ROWS), :][BF16_ROWS - 1:BF16_ROWS, :]
            nc0 = jnp.where(here, lane_sum(last0, 3, 6), nc0)
            nc1 = jnp.where(here, lane_sum(last1, 3, 6), nc1)
            return kn0, kn1, nc0, nc1

        zero = jnp.zeros((1, LANE), F32)
        kn0, kn1, nc0, nc1 = lax.fori_loop(0, n_kb, stats, (zero, zero, zero, zero))
        kstat[0:1, :] = kn0
        kstat[1:2, :] = kn1
        kstat[2:3, :] = nc0
        kstat[3:4, :] = nc1

    q = q_ref[...]
    qaug[0] = jnp.where(lower, q, aq_ref[0])
    qaug[1] = jnp.where(lower, aq_ref[1], q)
    row = lax.broadcasted_iota(jnp.int32, (tq, tk), 0)
    col = lax.broadcasted_iota(jnp.int32, (tq, tk), 1)

    qn = max_sq_norm(q)
    first = None
    for hf in range(2):
        qn2 = qn[:, hf * HEAD_DIM:hf * HEAD_DIM + 1]
        kn2 = kstat[hf:hf + 1, :]
        kn2_all = jnp.max(kn2, axis=1, keepdims=True)
        cum_q = lane_sum(aq_ref[hf, 0:1, :], 0, 3)
        upper = jnp.sqrt(qn2 * kn2) * NORM_SLACK + cum_q + kstat[2 + hf:3 + hf, :] + 1.0
        floor = -jnp.sqrt(qn2 * kn2_all) * NORM_SLACK
        needed = (upper >= floor - UNDERFLOW) & (lane < n_kb)
        first_h = jnp.min(jnp.where(needed, lane.astype(F32), float(n_kb)))
        first = first_h if first is None else jnp.minimum(first, first_h)
    kb_start = jnp.minimum(first.astype(jnp.int32), qi * n_sub)

    def step(kb, carry, mask_shift):
        off = pl.multiple_of(kb * tk, tk)
        v = v_ref[pl.ds(off, tk), :]
        new = []
        for hf in range(2):
            m, l, acc = carry[hf]
            s = _dot_nt(qaug[hf], kaug[hf, pl.ds(off, tk), :])
            if mask_shift is not None:
                s = jnp.where(col + mask_shift <= row, s, NEG_INF)
            m_new = jnp.maximum(m, jnp.max(s, axis=-1, keepdims=True))
            alpha = jnp.exp(m - m_new)
            pr = jnp.exp(s - m_new)
            l = alpha * l + jnp.sum(pr, axis=-1, keepdims=True)
            acc = alpha * acc + _dot(pr.astype(BF16), v)
            new.append((m_new, l, acc))
        return tuple(new)

    one = (jnp.full((tq, 1), NEG_INF, F32), jnp.zeros((tq, 1), F32), jnp.zeros((tq, LANE), F32))
    carry = lax.fori_loop(kb_start, qi * n_sub, functools.partial(step, mask_shift=None), (one, one))
    for d in range(n_sub):
        carry = step(qi * n_sub + d, carry, d * tk)
    (_, l0, acc0), (_, l1, acc1) = carry
    o_ref[...] = jnp.where(lower, acc0 / l0, acc1 / l1).astype(BF16)


def _fox(qkv, augq, augk, tq=512, tk=256):
    b, seq, _ = qkv.shape
    return pl.pallas_call(
        functools.partial(_fox_kernel, tq=tq, tk=tk),
        out_shape=jax.ShapeDtypeStruct((b, seq, WIDTH), BF16),
        grid=(b, N_PAIRS, seq // tq),
        in_specs=[pl.BlockSpec((None, tq, LANE), lambda bi, p, i: (bi, i, COL_QB + p)),
                  pl.BlockSpec((None, seq, LANE), lambda bi, p, i: (bi, 0, COL_KB + p)),
                  pl.BlockSpec((None, seq, LANE), lambda bi, p, i: (bi, 0, COL_VB + p)),
                  pl.BlockSpec((None, 2, tq, LANE), lambda bi, p, i: (bi, p, i, 0)),
                  pl.BlockSpec((None, 2, seq, LANE), lambda bi, p, i: (bi, p, 0, 0))],
        out_specs=pl.BlockSpec((None, tq, LANE), lambda bi, p, i: (bi, i, p)),
        scratch_shapes=[pltpu.VMEM((2, seq, LANE), BF16), pltpu.VMEM((2, tq, LANE), BF16),
                        pltpu.VMEM((8, LANE), F32)],
        compiler_params=_params(("parallel", "parallel", "arbitrary")),
        name="forgetting_attention",
    )(qkv, qkv, qkv, augq, augk)


def _merge_kernel(h_ref, g_ref, oa_ref, ob_ref, oc_ref, wg_ref, bg_ref, wa_ref, wb_ref, wc_ref, wo_ref, out_ref):
    h = h_ref[...]
    xn = _rms(h, g_ref[...]).astype(BF16)
    merged = jnp.zeros(h.shape, F32)
    for idx, (o_ref, w_ref) in enumerate(((oa_ref, wa_ref), (ob_ref, wb_ref), (oc_ref, wc_ref))):
        cols = slice(idx * D_MODEL, (idx + 1) * D_MODEL)
        gate = jax.nn.sigmoid(_dot(xn, wg_ref[:, cols]) + bg_ref[:, cols])
        merged = merged + gate * _dot(o_ref[...], w_ref[...])
    out_ref[...] = h + _dot(merged.astype(BF16), wo_ref[...])


def _merge(h, g, oa, ob, oc, w_gate, b_gate, w_a, w_b, w_c, w_out, tm=256):
    t_tokens = h.shape[0]
    full = lambda shape: pl.BlockSpec(shape, lambda i: (0, 0))
    ospec = pl.BlockSpec((tm, WIDTH), lambda i: (i, 0))
    return pl.pallas_call(
        _merge_kernel,
        out_shape=jax.ShapeDtypeStruct((t_tokens, D_MODEL), F32),
        grid=(t_tokens // tm,),
        in_specs=[pl.BlockSpec((tm, D_MODEL), lambda i: (i, 0)), full((1, D_MODEL)),
                  ospec, ospec, ospec,
                  full((D_MODEL, 3 * D_MODEL)), full((1, 3 * D_MODEL)),
                  full((WIDTH, D_MODEL)), full((WIDTH, D_MODEL)), full((WIDTH, D_MODEL)),
                  full((D_MODEL, D_MODEL))],
        out_specs=pl.BlockSpec((tm, D_MODEL), lambda i: (i, 0)),
        compiler_params=_params(("parallel",)),
        name="gated_merge",
    )(h, g, oa, ob, oc, w_gate, b_gate, w_a, w_b, w_c, w_out)


_CAND_COUNTS = tuple(PEER_TOPK // (i + 1) for i in range(PEER_TOPK))
_N_CAND = sum(_CAND_COUNTS)
_CAND_ROWS = 56
PIECE = 256


def _top16_rows(src, dst_ref):
    cur = src
    for r in range(PEER_TOPK):
        mx = jnp.max(cur, axis=0, keepdims=True)
        dst_ref[r:r + 1, :] = mx
        if r + 1 < PEER_TOPK:
            cur = jnp.where(cur == mx, NEG_INF, cur)


def _sort16_pairs():
    n, out, p = PEER_TOPK, [], 1
    while p < n:
        k = p
        while k >= 1:
            for j in range(k % p, n - k, 2 * k):
                for i in range(min(k, n - j - k)):
                    if (i + j) // (2 * p) == (i + j + k) // (2 * p):
                        out.append((i + j, i + j + k))
            k //= 2
        p *= 2
    return tuple(out)


_SORT16 = _sort16_pairs()
SUBLANES = 8


def _top16_of_keys(src, dst_ref):
    rows = [src[SUBLANES * k:SUBLANES * (k + 1), :] for k in range(N_KEYS // SUBLANES)]

    def exchange(i, j):
        rows[i], rows[j] = jnp.maximum(rows[i], rows[j]), jnp.minimum(rows[i], rows[j])

    for i, j in _SORT16:
        exchange(i, j)
    for shift in (4, 2, 1):
        other = [pltpu.roll(r, shift, 0) for r in rows]
        rows = [jnp.maximum(rows[k], other[PEER_TOPK - 1 - k]) for k in range(PEER_TOPK)]
        for d in (8, 4, 2, 1):
            for k in range(PEER_TOPK):
                if k & d == 0:
                    exchange(k, k + d)
    for r in range(PEER_TOPK):
        dst_ref[r:r + 1, :] = rows[r][0:1, :]


def _peer_kernel_plain(h_ref, g_ref, wq_ref, k1_ref, k2_ref, u0_ref, u_ref, vt_ref, out_ref, *scratch, ce, th):
    _peer_body(h_ref, g_ref, wq_ref, k1_ref, k2_ref, u0_ref, u_ref, vt_ref, None, out_ref, *scratch, ce=ce, th=th)


def _peer_kernel_final(h_ref, g_ref, wq_ref, k1_ref, k2_ref, u0_ref, u_ref, vt_ref, fg_ref, out_ref, *scratch, ce, th):
    _peer_body(h_ref, g_ref, wq_ref, k1_ref, k2_ref, u0_ref, u_ref, vt_ref, fg_ref, out_ref, *scratch, ce=ce, th=th)


def _peer_body(h_ref, g_ref, wq_ref, k1_ref, k2_ref, u0_ref, u_ref, vt_ref, fg_ref, out_ref,
               xn_s, s1_s, s2_s, tau_s, v1_s, v2_s, cand_s, top_s, y_s, w_s, act_s, *, ce, th):
    j = pl.program_id(1)
    nj = pl.num_programs(1)
    tm = h_ref.shape[0]
    a_per_step = ce // N_KEYS
    halves = [slice(t0, t0 + th) for t0 in range(0, tm, th)]

    def fill_candidates():
        off = 0
        for i, cnt in enumerate(_CAND_COUNTS):
            cand_s[off:off + cnt, :] = v1_s[i:i + 1, :] + v2_s[0:cnt, :]
            off += cnt

    @pl.when(j == 0)
    def _route():
        xn_s[...] = _rms(h_ref[...], g_ref[...]).astype(BF16)
        y_s[...] = jnp.zeros_like(y_s)
        cand_s[...] = jnp.full(cand_s.shape, NEG_INF, F32)
        for tok in halves:
            xn = xn_s[tok, :]
            for hd in range(PEER_HEADS):
                q1 = _dot(xn, wq_ref[:, (2 * hd) * N_KEYS:(2 * hd + 1) * N_KEYS]).astype(BF16)
                q2 = _dot(xn, wq_ref[:, (2 * hd + 1) * N_KEYS:(2 * hd + 2) * N_KEYS]).astype(BF16)
                s1 = _dot_nt(k1_ref[...], q1)
                s2 = _dot_nt(k2_ref[...], q2)
                _top16_of_keys(s1, v1_s)
                _top16_of_keys(s2, v2_s)
                fill_candidates()
                _top16_rows(cand_s[...], top_s)
                top = top_s[...]
                mx = top[0:1, :]
                shift = mx + jnp.log(jnp.sum(jnp.exp(top - mx), axis=0, keepdims=True))
                s1_s[hd, :, tok] = (s1 - shift) * LOG2E - 1.0
                s2_s[hd, :, tok] = s2 * LOG2E
                v1_s[...] = (v1_s[...] - shift) * LOG2E - 1.0
                v2_s[...] = v2_s[...] * LOG2E
                fill_candidates()
                _top16_rows(cand_s[...], top_s)
                tau_s[hd:hd + 1, tok] = top_s[PEER_TOPK - 1:PEER_TOPK, :]
            act_s[:, tok] = _dot_nt(u0_ref[...], xn)

    n_pieces = ce // PIECE

    def act_piece(tok, p):
        return _dot_nt(u_ref[p * PIECE:(p + 1) * PIECE, :], xn_s[tok, :])

    def mix_piece(tok, p, act):
        for aa in range(PIECE // N_KEYS):
            a = j * a_per_step + p * (PIECE // N_KEYS) + aa
            coef = jnp.zeros((N_KEYS, th), F32)
            for hd in range(PEER_HEADS):
                s = s2_s[hd, :, tok] + s1_s[hd, pl.ds(a, 1), tok]
                coef = coef + jnp.where(s >= tau_s[hd:hd + 1, tok], jnp.exp2(s), 0.0)
            x = act[aa * N_KEYS:(aa + 1) * N_KEYS, :]
            gelu2 = x * (1.0 + lax.erf(x * math.sqrt(0.5)))
            rows = slice(p * PIECE + aa * N_KEYS, p * PIECE + (aa + 1) * N_KEYS)
            w_s[rows, tok] = (coef * gelu2).astype(BF16)

    def out_piece(tok, p):
        return _dot(vt_ref[:, p * PIECE:(p + 1) * PIECE], w_s[p * PIECE:(p + 1) * PIECE, tok])

    for tok in halves:
        y_half = None
        for p in range(n_pieces):
            rows = slice(p * PIECE, (p + 1) * PIECE)
            mix_piece(tok, p, act_s[rows, tok])
            act_s[rows, tok] = act_piece(tok, p)
            yp = out_piece(tok, p)
            y_half = yp if y_half is None else y_half + yp
        y_s[:, tok] += y_half

    @pl.when(j == nj - 1)
    def _finish():
        y = h_ref[...] + y_s[...].T
        if fg_ref is not None:
            y = _rms(y, fg_ref[...])
        out_ref[...] = y


def _peer(h, g, wq, k1, k2, u, vt, final_g=None, tm=512, ce=1024, th=256):
    t_tokens = h.shape[0]
    tm = min(tm, t_tokens)
    n_chunks = N_EXPERTS // ce
    full = lambda shape: pl.BlockSpec(shape, lambda i, j: (0, 0))
    in_specs = [pl.BlockSpec((tm, D_MODEL), lambda i, j: (i, 0)), full((1, D_MODEL)),
                full((D_MODEL, 2 * PEER_HEADS * N_KEYS)), full((N_KEYS, N_KEYS)), full((N_KEYS, N_KEYS)),
                pl.BlockSpec((ce, D_MODEL), lambda i, j: (0, 0)),
                pl.BlockSpec((ce, D_MODEL), lambda i, j: (jnp.minimum(j + 1, n_chunks - 1), 0)),
                pl.BlockSpec((D_MODEL, ce), lambda i, j: (0, j))]
    args = [h, g, wq, k1, k2, u, u, vt]
    kern = _peer_kernel_plain
    if final_g is not None:
        in_specs.append(full((1, D_MODEL)))
        args.append(final_g)
        kern = _peer_kernel_final
    head_buf = pltpu.VMEM((PEER_HEADS, N_KEYS, tm), F32)
    return pl.pallas_call(
        functools.partial(kern, ce=ce, th=th),
        out_shape=jax.ShapeDtypeStruct((t_tokens, D_MODEL), F32),
        grid=(t_tokens // tm, N_EXPERTS // ce),
        in_specs=in_specs,
        out_specs=pl.BlockSpec((tm, D_MODEL), lambda i, j: (i, 0)),
        scratch_shapes=[pltpu.VMEM((tm, D_MODEL), BF16), head_buf, head_buf,
                        pltpu.VMEM((PEER_HEADS, tm), F32),
                        pltpu.VMEM((PEER_TOPK, th), F32), pltpu.VMEM((PEER_TOPK, th), F32),
                        pltpu.VMEM((_CAND_ROWS, th), F32), pltpu.VMEM((PEER_TOPK, th), F32),
                        pltpu.VMEM((D_MODEL, tm), F32), pltpu.VMEM((ce, tm), BF16),
                        pltpu.VMEM((ce, tm), F32)],
        compiler_params=_params(("parallel", "arbitrary")),
        name="peer_ffn",
    )(*args)


def _rope_tables(positions):
    inv_freq = ROPE_THETA ** (-jnp.arange(0, HEAD_DIM, 2, dtype=F32) / HEAD_DIM)
    ang = positions.astype(F32).reshape(-1, 1) * inv_freq
    cos, sin = jnp.cos(ang), jnp.sin(ang)
    return jnp.concatenate([cos] * 4, axis=-1), jnp.concatenate([-sin, sin, -sin, sin], axis=-1)


def _split_w_in(w_in):
    sizes = (WIDTH,) * 6 + (N_HEADS, WIDTH, C_KV_HEADS * HEAD_DIM, C_KV_HEADS * HEAD_DIM, 3 * D_MODEL)
    parts, start = [], 0
    for size in sizes:
        parts.append(w_in[:, start:start + size])
        start += size
    return parts


_C_HEAD_ORDER = (0, 3, 1, 4, 2, 5)


def _permute_heads(w, axis):
    shape = w.shape
    w = w.reshape(shape[:axis] + (N_HEADS, HEAD_DIM) + shape[axis + 1:])
    w = jnp.take(w, jnp.array(_C_HEAD_ORDER), axis=axis)
    return w.reshape(shape)


def _to_classes(x, batch, dil):
    c = x.shape[-1]
    seq = x.shape[0] // batch
    return x.reshape(batch, seq // dil, dil, c).transpose(0, 2, 1, 3).reshape(batch * dil, seq // dil, c)


def _from_classes(x, batch, dil):
    _, length, c = x.shape
    return x.reshape(batch, dil, length, c).transpose(0, 2, 1, 3).reshape(batch * length * dil, c)


def _mixer(h, cos4, sin4, batch, norm_g, w_in, b_forget, sinks, b_gate, w_br_a, w_br_b, w_br_c, w_out):
    t_tokens = h.shape[0]
    seq = t_tokens // batch
    qa, ka, va, qb, kb, vb, fb, qc, kc, vc, wg = _split_w_in(w_in)
    w_f = jnp.pad(fb, ((0, 0), (0, LANE - N_HEADS)))
    w_att = jnp.concatenate([qa, ka, va, qb, kb, vb, _permute_heads(qc, 1), kc, vc, w_f], axis=1).astype(BF16)
    qkv, f = _in_proj(h, norm_g.reshape(1, -1), w_att, cos4, sin4)

    outs, lses = [], []
    qkv_a = qkv[:, :COL_QB * LANE]
    for dil in A_DILATIONS:
        x = qkv_a.reshape(batch, seq, -1) if dil == 1 else _to_classes(qkv_a, batch, dil)
        o, lse = _banded(x, COL_QA, COL_KA, COL_VA, A_MAX_DIST, kv_shared=False)
        if dil == 1:
            outs.append(o.reshape(t_tokens, WIDTH))
            lses.append(lse.reshape(t_tokens, WIDTH))
        else:
            outs.append(_from_classes(o, batch, dil))
            lses.append(_from_classes(lse, batch, dil))
    oa = _combine(outs, lses)

    qkv3 = qkv.reshape(batch, seq, -1)
    sink_tab = jnp.take(sinks.astype(F32), jnp.array(_C_HEAD_ORDER))
    (oc,) = _banded(qkv3, COL_QC, COL_KC, COL_VC, C_MAX_DIST, kv_shared=True, sink=sink_tab, want_lse=False)
    oc = oc.reshape(t_tokens, WIDTH)

    bf_row = jnp.pad(b_forget.astype(F32), (0, LANE - N_HEADS)).reshape(1, LANE)
    augq, augk = _cum_aug(f.reshape(batch, seq, LANE), bf_row)
    ob = _fox(qkv3, augq, augk).reshape(t_tokens, WIDTH)

    return _merge(h, norm_g.reshape(1, -1), oa, ob, oc, wg.astype(BF16), b_gate.reshape(1, -1).astype(F32),
                  w_br_a.astype(BF16), w_br_b.astype(BF16), _permute_heads(w_br_c, 0).astype(BF16),
                  w_out.astype(BF16))


def kernel(x, positions, norm1_g, w_in, b_forget, sinks, b_gate, w_br_a, w_br_b, w_br_c, w_out, norm2_g,
           peer_wq, peer_k1, peer_k2, peer_u, peer_v, final_g):
    batch, seq, _ = x.shape
    depth = w_in.shape[0]
    cos4, sin4 = _rope_tables(positions)
    h = x.reshape(batch * seq, D_MODEL)
    for l in range(depth):
        h = _mixer(h, cos4, sin4, batch, norm1_g[l], w_in[l], b_forget[l], sinks[l], b_gate[l],
                   w_br_a[l], w_br_b[l], w_br_c[l], w_out[l])
        h = _peer(h, norm2_g[l].reshape(1, -1), peer_wq[l].astype(BF16), peer_k1[l].astype(BF16),
                  peer_k2[l].astype(BF16), peer_u[l].astype(BF16), peer_v[l].T.astype(BF16),
                  final_g=final_g.reshape(1, -1) if l == depth - 1 else None)
    return h.reshape(batch, seq, D_MODEL)
```

```python
import functools
import math

import jax
import jax.numpy as jnp
import numpy as np
from jax import lax
from jax.experimental import pallas as pl
from jax.experimental.pallas import tpu as pltpu

F32 = jnp.float32
BF16 = jnp.bfloat16

D_MODEL = 1024
HEAD_DIM = 64
N_HEADS = 6
N_PAIRS = N_HEADS // 2
C_KV_HEADS = 2
A_DILATIONS = (1, 4, 16)
A_MAX_DIST = 128
C_MAX_DIST = 127
BLOCK = 128
LANE = 128
ROPE_THETA = 10000.0
RMS_EPS = 1e-6
NEG_INF = -1e30
PEER_HEADS = 8
N_KEYS = 128
N_EXPERTS = N_KEYS * N_KEYS
PEER_TOPK = 16
WIDTH = N_HEADS * HEAD_DIM

COL_QA, COL_KA, COL_VA = 0, 3, 6
COL_QB, COL_KB, COL_VB = 9, 12, 15
COL_QC, COL_KC, COL_VC = 18, 21, 22
N_QKV_BLOCKS = 23
QKV_BLOCKS_PADDED = 24
COL_F = 23
ROPE_BLOCKS = (0, 1, 2, 3, 4, 5, 18, 19, 20, 21)
QUERY_BLOCKS = (0, 1, 2, 9, 10, 11, 18, 19, 20)
QK_SCALE = HEAD_DIM ** -0.5
LOG2E = math.log2(math.e)

VMEM_LIMIT = 56 * 1024 * 1024


def _params(sem):
    return pltpu.CompilerParams(dimension_semantics=sem, vmem_limit_bytes=VMEM_LIMIT)


def _dot(a, b):
    return jnp.dot(a, b, preferred_element_type=F32)


def _dot_nt(a, b):
    return lax.dot_general(a, b, (((1,), (1,)), ((), ())), preferred_element_type=F32)


def _rms(x, g):
    var = jnp.mean(x * x, axis=-1, keepdims=True)
    return x * lax.rsqrt(var + RMS_EPS) * g


def _in_proj_kernel(h_ref, g_ref, w_ref, cos_ref, sin_ref, qkv_ref, f_ref, proj_s):
    xn = _rms(h_ref[...], g_ref[...]).astype(BF16)
    proj_s[...] = _dot(xn, w_ref[...])
    cos = cos_ref[...]
    sin = sin_ref[...]
    lane = lax.broadcasted_iota(jnp.int32, cos.shape, 1)
    first_half = (lane % HEAD_DIM) < (HEAD_DIM // 2)
    for c in range(N_QKV_BLOCKS):
        t = proj_s[:, c * LANE:(c + 1) * LANE]
        if c in ROPE_BLOCKS:
            rot = jnp.where(first_half, pltpu.roll(t, LANE - HEAD_DIM // 2, 1), pltpu.roll(t, HEAD_DIM // 2, 1))
            t = t * cos + rot * sin
        if c in QUERY_BLOCKS:
            t = t * QK_SCALE
        qkv_ref[:, c * LANE:(c + 1) * LANE] = t.astype(BF16)
    qkv_ref[:, N_QKV_BLOCKS * LANE:] = jnp.zeros((qkv_ref.shape[0], (QKV_BLOCKS_PADDED - N_QKV_BLOCKS) * LANE), BF16)
    f_ref[...] = proj_s[:, COL_F * LANE:(COL_F + 1) * LANE]


def _in_proj(h, g, w_att, cos4, sin4, tm=512):
    t_tokens = h.shape[0]
    n_cols = w_att.shape[1]
    return pl.pallas_call(
        _in_proj_kernel,
        out_shape=(jax.ShapeDtypeStruct((t_tokens, QKV_BLOCKS_PADDED * LANE), BF16),
                   jax.ShapeDtypeStruct((t_tokens, LANE), F32)),
        grid=(t_tokens // tm,),
        in_specs=[pl.BlockSpec((tm, D_MODEL), lambda i: (i, 0)),
                  pl.BlockSpec((1, D_MODEL), lambda i: (0, 0)),
                  pl.BlockSpec((D_MODEL, n_cols), lambda i: (0, 0)),
                  pl.BlockSpec((tm, LANE), lambda i: (i, 0)),
                  pl.BlockSpec((tm, LANE), lambda i: (i, 0))],
        out_specs=(pl.BlockSpec((tm, QKV_BLOCKS_PADDED * LANE), lambda i: (i, 0)),
                   pl.BlockSpec((tm, LANE), lambda i: (i, 0))),
        scratch_shapes=[pltpu.VMEM((tm, n_cols), F32)],
        compiler_params=_params(("parallel",)),
        name="in_proj",
    )(h, g, w_att, cos4, sin4)


def _banded_kernel(*refs, max_dist, qb, kv_shared, has_sink, want_lse):
    refs = list(refs)
    q_ref, kp_ref, kc_ref, vp_ref, vc_ref = refs[:5]
    pos = 5
    sink_ref = None
    if has_sink:
        sink_ref = refs[pos]
        pos += 1
    o_ref = refs[pos]
    pos += 1
    lse_ref = None
    if want_lse:
        lse_ref = refs[pos]
        pos += 1
    kwin, vwin = refs[pos], refs[pos + 1]

    i = pl.program_id(2)
    kwin[0:BLOCK, :] = kp_ref[...]
    kwin[BLOCK:, :] = kc_ref[...]
    vwin[0:BLOCK, :] = vp_ref[...]
    vwin[BLOCK:, :] = vc_ref[...]

    row = lax.broadcasted_iota(jnp.int32, (BLOCK, 2 * BLOCK), 0)
    col = lax.broadcasted_iota(jnp.int32, (BLOCK, 2 * BLOCK), 1)
    dist = row + BLOCK - col
    band = (dist >= 0) & (dist <= max_dist)
    lane = lax.broadcasted_iota(jnp.int32, (1, LANE), 1)
    lower = lane < HEAD_DIM

    def body(sb, carry):
        off = pl.multiple_of(sb * BLOCK, BLOCK)
        valid = band & ((col >= BLOCK) | (i * qb + sb > 0))
        for p in range(N_PAIRS):
            q = q_ref[pl.ds(off, BLOCK), p * LANE:(p + 1) * LANE]
            kv_cols = slice(0, LANE) if kv_shared else slice(p * LANE, (p + 1) * LANE)
            k = kwin[pl.ds(off, 2 * BLOCK), kv_cols]
            v = vwin[pl.ds(off, 2 * BLOCK), kv_cols]
            o_pair = jnp.zeros((BLOCK, LANE), F32)
            lse_pair = jnp.zeros((BLOCK, LANE), F32)
            for hf in range(2):
                sel = lower if hf == 0 else jnp.logical_not(lower)
                qm = jnp.where(sel, q, jnp.zeros_like(q))
                s = _dot_nt(qm, k)
                s = jnp.where(valid, s, NEG_INF)
                m = jnp.max(s, axis=-1, keepdims=True)
                if has_sink:
                    sk = sink_ref[2 * p + hf]
                    m = jnp.maximum(m, sk)
                pr = jnp.exp(s - m)
                den = jnp.sum(pr, axis=-1, keepdims=True)
                if has_sink:
                    den = den + jnp.exp(sk - m)
                o = _dot(pr.astype(BF16), v) / den
                o_pair = jnp.where(sel, o, o_pair)
                if want_lse:
                    lse_pair = jnp.where(sel, m + jnp.log(den), lse_pair)
            o_ref[pl.ds(off, BLOCK), p * LANE:(p + 1) * LANE] = o_pair.astype(BF16)
            if want_lse:
                lse_ref[pl.ds(off, BLOCK), p * LANE:(p + 1) * LANE] = lse_pair
        return carry

    lax.fori_loop(0, qb, body, 0, unroll=True)


def _banded(x, dil, qcol, kcol, vcol, max_dist, kv_shared, sink=None, want_lse=True, qb=4):
    batch, seq, cols = x.shape
    length = seq // dil
    qb = min(qb, length // BLOCK)
    tq = qb * BLOCK
    kvw = LANE if kv_shared else WIDTH
    assert dil == 1 or (cols % WIDTH == 0 and not kv_shared)
    wide = cols // WIDTH
    kblk = kcol if kv_shared else kcol // N_PAIRS
    vblk = vcol if kv_shared else vcol // N_PAIRS
    view = x.reshape(batch, length, dil * cols)
    in_specs = [
        pl.BlockSpec((None, tq, WIDTH), lambda b, c, i: (b, i, c * wide + qcol // N_PAIRS)),
        pl.BlockSpec((None, BLOCK, kvw), lambda b, c, i: (b, jnp.maximum(i * qb - 1, 0), c * wide + kblk)),
        pl.BlockSpec((None, tq, kvw), lambda b, c, i: (b, i, c * wide + kblk)),
        pl.BlockSpec((None, BLOCK, kvw), lambda b, c, i: (b, jnp.maximum(i * qb - 1, 0), c * wide + vblk)),
        pl.BlockSpec((None, tq, kvw), lambda b, c, i: (b, i, c * wide + vblk)),
    ]
    args = [view] * 5
    if sink is not None:
        in_specs.append(pl.BlockSpec(memory_space=pltpu.SMEM))
        args.append(sink)
    out_shape = [jax.ShapeDtypeStruct((batch, length, dil * WIDTH), BF16)]
    out_specs = [pl.BlockSpec((None, tq, WIDTH), lambda b, c, i: (b, i, c))]
    if want_lse:
        out_shape.append(jax.ShapeDtypeStruct((batch, length, dil * WIDTH), F32))
        out_specs.append(pl.BlockSpec((None, tq, WIDTH), lambda b, c, i: (b, i, c)))
    kern = functools.partial(_banded_kernel, max_dist=max_dist, qb=qb, kv_shared=kv_shared,
                             has_sink=sink is not None, want_lse=want_lse)
    outs = pl.pallas_call(
        kern,
        out_shape=tuple(out_shape),
        grid=(batch, dil, length // tq),
        in_specs=in_specs,
        out_specs=tuple(out_specs),
        scratch_shapes=[pltpu.VMEM((tq + BLOCK, kvw), BF16), pltpu.VMEM((tq + BLOCK, kvw), BF16)],
        compiler_params=_params(("parallel", "parallel", "parallel")),
        name="banded_attention",
    )(*args)
    return tuple(o.reshape(batch * seq, WIDTH) for o in outs)


def _combine_kernel(o1_ref, o2_ref, o3_ref, l1_ref, l2_ref, l3_ref, out_ref):
    l1, l2, l3 = l1_ref[...], l2_ref[...], l3_ref[...]
    m = jnp.maximum(jnp.maximum(l1, l2), l3)
    w1, w2, w3 = jnp.exp(l1 - m), jnp.exp(l2 - m), jnp.exp(l3 - m)
    num = w1 * o1_ref[...].astype(F32) + w2 * o2_ref[...].astype(F32) + w3 * o3_ref[...].astype(F32)
    out_ref[...] = (num / (w1 + w2 + w3)).astype(BF16)


def _combine(outs, lses, tm=1024):
    t_tokens = outs[0].shape[0]
    spec = pl.BlockSpec((tm, WIDTH), lambda i: (i, 0))
    return pl.pallas_call(
        _combine_kernel,
        out_shape=jax.ShapeDtypeStruct((t_tokens, WIDTH), BF16),
        grid=(t_tokens // tm,),
        in_specs=[spec] * 6,
        out_specs=spec,
        compiler_params=_params(("parallel",)),
        name="combine_patterns",
    )(*outs, *lses)


CUM_BLOCK = 256
BF16_ROWS = 16
UNDERFLOW = 104.0
NORM_SLACK = 1.02


def _cum_kernel(f_ref, bf_ref, augq_ref, augk_ref, carry_s):
    seq = f_ref.shape[0]

    @pl.when(pl.program_id(1) == 0)
    def _():
        carry_s[...] = jnp.zeros_like(carry_s)

    r = lax.broadcasted_iota(jnp.int32, (LANE, LANE), 0)
    rr = lax.broadcasted_iota(jnp.int32, (CUM_BLOCK, CUM_BLOCK), 0)
    cc = lax.broadcasted_iota(jnp.int32, (CUM_BLOCK, CUM_BLOCK), 1)
    tril = jnp.where(rr >= cc, 1.0, 0.0).astype(F32)
    lane = lax.broadcasted_iota(jnp.int32, (CUM_BLOCK, LANE), 1) % HEAD_DIM
    bias = bf_ref[...]

    def body(blk, carry):
        off = pl.multiple_of(blk * CUM_BLOCK, CUM_BLOCK)
        x = f_ref[pl.ds(off, CUM_BLOCK), :] + bias
        log_f = jnp.minimum(x, 0.0) - jnp.log1p(jnp.exp(-jnp.abs(x)))
        cum_all = jnp.dot(tril, log_f, preferred_element_type=F32, precision=lax.Precision.HIGHEST) + carry
        for head in range(N_HEADS):
            sel = jnp.where(r == head, 1.0, 0.0).astype(F32)
            cum = jnp.dot(cum_all, sel, preferred_element_type=F32, precision=lax.Precision.HIGHEST)
            c1 = cum.astype(BF16).astype(F32)
            c2 = (cum - c1).astype(BF16).astype(F32)
            c3 = (cum - c1 - c2).astype(BF16).astype(F32)
            piece = jnp.where(lane % 3 == 0, c1, jnp.where(lane % 3 == 1, c2, c3))
            one = jnp.ones_like(cum)
            zero = jnp.zeros_like(cum)
            aq = jnp.where(lane < 3, piece, jnp.where(lane < 6, one, zero))
            ak = jnp.where(lane < 3, one, jnp.where(lane < 6, -piece, zero))
            augq_ref[head, pl.ds(off, CUM_BLOCK), :] = aq.astype(BF16)
            augk_ref[head, pl.ds(off, CUM_BLOCK), :] = ak.astype(BF16)
        return cum_all[CUM_BLOCK - 1:CUM_BLOCK, :]

    carry_s[...] = lax.fori_loop(0, seq // CUM_BLOCK, body, carry_s[...])


def _cum_aug(f, b_forget_row, ts=2048):
    b, seq, _ = f.shape
    ts = min(ts, seq)
    out = jax.ShapeDtypeStruct((b, N_HEADS, seq, LANE), BF16)
    ospec = pl.BlockSpec((None, N_HEADS, ts, LANE), lambda bi, si: (bi, 0, si, 0))
    return pl.pallas_call(
        _cum_kernel,
        out_shape=(out, out),
        grid=(b, seq // ts),
        in_specs=[pl.BlockSpec((None, ts, LANE), lambda bi, si: (bi, si, 0)),
                  pl.BlockSpec((1, LANE), lambda bi, si: (0, 0))],
        out_specs=(ospec, ospec),
        scratch_shapes=[pltpu.VMEM((1, LANE), F32)],
        compiler_params=_params(("parallel", "arbitrary")),
        name="forget_cumsum",
    )(f, b_forget_row)


def _fox_kernel(q_ref, k_ref, v_ref, aq_ref, ak_ref, o_ref, kaug, qaug, kstat, *, tq, tk):
    qi = pl.program_id(2)
    n_kb = k_ref.shape[0] // tk
    n_sub = tq // tk
    lane = lax.broadcasted_iota(jnp.int32, (1, LANE), 1)
    lower = lane < HEAD_DIM
    hr = lax.broadcasted_iota(jnp.int32, (LANE, LANE), 0) // HEAD_DIM
    hc = lax.broadcasted_iota(jnp.int32, (LANE, LANE), 1) // HEAD_DIM
    same_head = jnp.where(hr == hc, 1.0, 0.0).astype(BF16)

    def max_sq_norm(x):
        xf = x.astype(F32)
        return jnp.max(_dot((xf * xf).astype(BF16), same_head), axis=0, keepdims=True)

    def lane_sum(row_vec, lo, hi):
        return jnp.sum(jnp.where((lane >= lo) & (lane < hi), row_vec.astype(F32), 0.0), axis=1, keepdims=True)

    @pl.when(qi == 0)
    def _():
        k = k_ref[...]
        kaug[0] = jnp.where(lower, k, ak_ref[0])
        kaug[1] = jnp.where(lower, ak_ref[1], k)

        def stats(kb, carry):
            kn0, kn1, nc0, nc1 = carry
            off = pl.multiple_of(kb * tk, tk)
            n2 = max_sq_norm(k_ref[pl.ds(off, tk), :])
            tail = pl.multiple_of(off + tk - BF16_ROWS, BF16_ROWS)
            here = lane == kb
            kn0 = jnp.where(here, n2[:, 0:1], kn0)
            kn1 = jnp.where(here, n2[:, HEAD_DIM:HEAD_DIM + 1], kn1)
            last0 = ak_ref[0, pl.ds(tail, BF16_ROWS), :][BF16_ROWS - 1:BF16_ROWS, :]
            last1 = ak_ref[1, pl.ds(tail, BF16_ROWS), :][BF16_ROWS - 1:BF16_ROWS, :]
            nc0 = jnp.where(here, lane_sum(last0, 3, 6), nc0)
            nc1 = jnp.where(here, lane_sum(last1, 3, 6), nc1)
            return kn0, kn1, nc0, nc1

        zero = jnp.zeros((1, LANE), F32)
        kn0, kn1, nc0, nc1 = lax.fori_loop(0, n_kb, stats, (zero, zero, zero, zero))
        kstat[0:1, :] = kn0
        kstat[1:2, :] = kn1
        kstat[2:3, :] = nc0
        kstat[3:4, :] = nc1

    q = q_ref[...]
    qaug[0] = jnp.where(lower, q, aq_ref[0])
    qaug[1] = jnp.where(lower, aq_ref[1], q)
    row = lax.broadcasted_iota(jnp.int32, (tq, tk), 0)
    col = lax.broadcasted_iota(jnp.int32, (tq, tk), 1)

    qn = max_sq_norm(q)
    first = None
    for hf in range(2):
        qn2 = qn[:, hf * HEAD_DIM:hf * HEAD_DIM + 1]
        kn2 = kstat[hf:hf + 1, :]
        kn2_all = jnp.max(kn2, axis=1, keepdims=True)
        cum_q = lane_sum(aq_ref[hf, 0:1, :], 0, 3)
        upper = jnp.sqrt(qn2 * kn2) * NORM_SLACK + cum_q + kstat[2 + hf:3 + hf, :] + 1.0
        floor = -jnp.sqrt(qn2 * kn2_all) * NORM_SLACK
        needed = (upper >= floor - UNDERFLOW) & (lane < n_kb)
        first_h = jnp.min(jnp.where(needed, lane.astype(F32), float(n_kb)))
        first = first_h if first is None else jnp.minimum(first, first_h)
    kb_start = jnp.minimum(first.astype(jnp.int32), qi * n_sub)

    def step(kb, carry, mask_shift):
        off = pl.multiple_of(kb * tk, tk)
        v = v_ref[pl.ds(off, tk), :]
        new = []
        for hf in range(2):
            m, l, acc = carry[hf]
            s = _dot_nt(qaug[hf], kaug[hf, pl.ds(off, tk), :])
            if mask_shift is not None:
                s = jnp.where(col + mask_shift <= row, s, NEG_INF)
            m_new = jnp.maximum(m, jnp.max(s, axis=-1, keepdims=True))
            alpha = jnp.exp(m - m_new)
            pr = jnp.exp(s - m_new)
            l = alpha * l + jnp.sum(pr, axis=-1, keepdims=True)
            acc = alpha * acc + _dot(pr.astype(BF16), v)
            new.append((m_new, l, acc))
        return tuple(new)

    one = (jnp.full((tq, 1), NEG_INF, F32), jnp.zeros((tq, 1), F32), jnp.zeros((tq, LANE), F32))
    carry = lax.fori_loop(kb_start, qi * n_sub, functools.partial(step, mask_shift=None), (one, one))
    for d in range(n_sub):
        carry = step(qi * n_sub + d, carry, d * tk)
    (_, l0, acc0), (_, l1, acc1) = carry
    o_ref[...] = jnp.where(lower, acc0 / l0, acc1 / l1).astype(BF16)


def _fox(qkv, augq, augk, tq=512, tk=256):
    b, seq, _ = qkv.shape
    return pl.pallas_call(
        functools.partial(_fox_kernel, tq=tq, tk=tk),
        out_shape=jax.ShapeDtypeStruct((b, seq, WIDTH), BF16),
        grid=(b, N_PAIRS, seq // tq),
        in_specs=[pl.BlockSpec((None, tq, LANE), lambda bi, p, i: (bi, i, COL_QB + p)),
                  pl.BlockSpec((None, seq, LANE), lambda bi, p, i: (bi, 0, COL_KB + p)),
                  pl.BlockSpec((None, seq, LANE), lambda bi, p, i: (bi, 0, COL_VB + p)),
                  pl.BlockSpec((None, 2, tq, LANE), lambda bi, p, i: (bi, p, i, 0)),
                  pl.BlockSpec((None, 2, seq, LANE), lambda bi, p, i: (bi, p, 0, 0))],
        out_specs=pl.BlockSpec((None, tq, LANE), lambda bi, p, i: (bi, i, p)),
        scratch_shapes=[pltpu.VMEM((2, seq, LANE), BF16), pltpu.VMEM((2, tq, LANE), BF16),
                        pltpu.VMEM((8, LANE), F32)],
        compiler_params=_params(("parallel", "parallel", "arbitrary")),
        name="forgetting_attention",
    )(qkv, qkv, qkv, augq, augk)


def _merge_kernel(h_ref, g_ref, oa_ref, ob_ref, oc_ref, wg_ref, bg_ref, wa_ref, wb_ref, wc_ref, wo_ref, out_ref):
    h = h_ref[...]
    xn = _rms(h, g_ref[...]).astype(BF16)
    merged = jnp.zeros(h.shape, F32)
    for idx, (o_ref, w_ref) in enumerate(((oa_ref, wa_ref), (ob_ref, wb_ref), (oc_ref, wc_ref))):
        cols = slice(idx * D_MODEL, (idx + 1) * D_MODEL)
        gate = jax.nn.sigmoid(_dot(xn, wg_ref[:, cols]) + bg_ref[:, cols])
        merged = merged + gate * _dot(o_ref[...], w_ref[...])
    out_ref[...] = h + _dot(merged.astype(BF16), wo_ref[...])


def _merge(h, g, oa, ob, oc, w_gate, b_gate, w_a, w_b, w_c, w_out, tm=256):
    t_tokens = h.shape[0]
    full = lambda shape: pl.BlockSpec(shape, lambda i: (0, 0))
    ospec = pl.BlockSpec((tm, WIDTH), lambda i: (i, 0))
    return pl.pallas_call(
        _merge_kernel,
        out_shape=jax.ShapeDtypeStruct((t_tokens, D_MODEL), F32),
        grid=(t_tokens // tm,),
        in_specs=[pl.BlockSpec((tm, D_MODEL), lambda i: (i, 0)), full((1, D_MODEL)),
                  ospec, ospec, ospec,
                  full((D_MODEL, 3 * D_MODEL)), full((1, 3 * D_MODEL)),
                  full((WIDTH, D_MODEL)), full((WIDTH, D_MODEL)), full((WIDTH, D_MODEL)),
                  full((D_MODEL, D_MODEL))],
        out_specs=pl.BlockSpec((tm, D_MODEL), lambda i: (i, 0)),
        compiler_params=_params(("parallel",)),
        name="gated_merge",
    )(h, g, oa, ob, oc, w_gate, b_gate, w_a, w_b, w_c, w_out)


_CAND_COUNTS = tuple(PEER_TOPK // (i + 1) for i in range(PEER_TOPK))
_N_CAND = sum(_CAND_COUNTS)
_CAND_ROWS = 56
PIECE = 256


def _top16_rows(src, dst_ref):
    cur = src
    for r in range(PEER_TOPK):
        mx = jnp.max(cur, axis=0, keepdims=True)
        dst_ref[r:r + 1, :] = mx
        if r + 1 < PEER_TOPK:
            cur = jnp.where(cur == mx, NEG_INF, cur)


def _sort16_pairs():
    n, out, p = PEER_TOPK, [], 1
    while p < n:
        k = p
        while k >= 1:
            for j in range(k % p, n - k, 2 * k):
                for i in range(min(k, n - j - k)):
                    if (i + j) // (2 * p) == (i + j + k) // (2 * p):
                        out.append((i + j, i + j + k))
            k //= 2
        p *= 2
    return tuple(out)


_SORT16 = _sort16_pairs()
SUBLANES = 8


def _top16_of_keys(src, dst_ref):
    rows = [src[SUBLANES * k:SUBLANES * (k + 1), :] for k in range(N_KEYS // SUBLANES)]

    def exchange(i, j):
        rows[i], rows[j] = jnp.maximum(rows[i], rows[j]), jnp.minimum(rows[i], rows[j])

    for i, j in _SORT16:
        exchange(i, j)
    for shift in (4, 2, 1):
        other = [pltpu.roll(r, shift, 0) for r in rows]
        rows = [jnp.maximum(rows[k], other[PEER_TOPK - 1 - k]) for k in range(PEER_TOPK)]
        for d in (8, 4, 2, 1):
            for k in range(PEER_TOPK):
                if k & d == 0:
                    exchange(k, k + d)
    for r in range(PEER_TOPK):
        dst_ref[r:r + 1, :] = rows[r][0:1, :]


def _peer_kernel_plain(h_ref, g_ref, wq_ref, k1_ref, k2_ref, u_ref, vt_ref, out_ref, *scratch, ce, th):
    _peer_body(h_ref, g_ref, wq_ref, k1_ref, k2_ref, u_ref, vt_ref, None, out_ref, *scratch, ce=ce, th=th)


def _peer_kernel_final(h_ref, g_ref, wq_ref, k1_ref, k2_ref, u_ref, vt_ref, fg_ref, out_ref, *scratch, ce, th):
    _peer_body(h_ref, g_ref, wq_ref, k1_ref, k2_ref, u_ref, vt_ref, fg_ref, out_ref, *scratch, ce=ce, th=th)


def _peer_body(h_ref, g_ref, wq_ref, k1_ref, k2_ref, u_ref, vt_ref, fg_ref, out_ref,
               xn_s, s1_s, s2_s, tau_s, v1_s, v2_s, cand_s, top_s, y_s, w_s, *, ce, th):
    j = pl.program_id(1)
    nj = pl.num_programs(1)
    tm = h_ref.shape[0]
    a_per_step = ce // N_KEYS
    halves = [slice(t0, t0 + th) for t0 in range(0, tm, th)]

    def fill_candidates():
        off = 0
        for i, cnt in enumerate(_CAND_COUNTS):
            cand_s[off:off + cnt, :] = v1_s[i:i + 1, :] + v2_s[0:cnt, :]
            off += cnt

    @pl.when(j == 0)
    def _route():
        xn_s[...] = _rms(h_ref[...], g_ref[...]).astype(BF16)
        y_s[...] = jnp.zeros_like(y_s)
        cand_s[...] = jnp.full(cand_s.shape, NEG_INF, F32)
        for tok in halves:
            xn = xn_s[tok, :]
            for hd in range(PEER_HEADS):
                q1 = _dot(xn, wq_ref[:, (2 * hd) * N_KEYS:(2 * hd + 1) * N_KEYS]).astype(BF16)
                q2 = _dot(xn, wq_ref[:, (2 * hd + 1) * N_KEYS:(2 * hd + 2) * N_KEYS]).astype(BF16)
                s1 = _dot_nt(k1_ref[...], q1)
                s2 = _dot_nt(k2_ref[...], q2)
                _top16_of_keys(s1, v1_s)
                _top16_of_keys(s2, v2_s)
                fill_candidates()
                _top16_rows(cand_s[...], top_s)
                top = top_s[...]
                mx = top[0:1, :]
                shift = mx + jnp.log(jnp.sum(jnp.exp(top - mx), axis=0, keepdims=True))
                s1_s[hd, :, tok] = (s1 - shift) * LOG2E - 1.0
                s2_s[hd, :, tok] = s2 * LOG2E
                v1_s[...] = (v1_s[...] - shift) * LOG2E - 1.0
                v2_s[...] = v2_s[...] * LOG2E
                fill_candidates()
                _top16_rows(cand_s[...], top_s)
                tau_s[hd:hd + 1, tok] = top_s[PEER_TOPK - 1:PEER_TOPK, :]

    n_pieces = ce // PIECE

    def act_piece(tok, p):
        return _dot_nt(u_ref[p * PIECE:(p + 1) * PIECE, :], xn_s[tok, :])

    def mix_piece(tok, p, act):
        for aa in range(PIECE // N_KEYS):
            a = j * a_per_step + p * (PIECE // N_KEYS) + aa
            coef = jnp.zeros((N_KEYS, th), F32)
            for hd in range(PEER_HEADS):
                s = s2_s[hd, :, tok] + s1_s[hd, pl.ds(a, 1), tok]
                coef = coef + jnp.where(s >= tau_s[hd:hd + 1, tok], jnp.exp2(s), 0.0)
            x = act[aa * N_KEYS:(aa + 1) * N_KEYS, :]
            gelu2 = x * (1.0 + lax.erf(x * math.sqrt(0.5)))
            rows = slice(p * PIECE + aa * N_KEYS, p * PIECE + (aa + 1) * N_KEYS)
            w_s[rows, tok] = (coef * gelu2).astype(BF16)

    def out_piece(tok, p):
        return _dot(vt_ref[:, p * PIECE:(p + 1) * PIECE], w_s[p * PIECE:(p + 1) * PIECE, tok])

    acts = [act_piece(halves[0], p) for p in range(n_pieces)]
    for hi, tok in enumerate(halves):
        nxt = halves[hi + 1] if hi + 1 < len(halves) else None
        prv = halves[hi - 1] if hi > 0 else None
        nxt_acts, y_prev = [], None
        for p in range(n_pieces):
            mix_piece(tok, p, acts[p])
            if nxt is not None:
                nxt_acts.append(act_piece(nxt, p))
            if prv is not None:
                yp = out_piece(prv, p)
                y_prev = yp if y_prev is None else y_prev + yp
        if prv is not None:
            y_s[:, prv] += y_prev
        acts = nxt_acts
    y_last = out_piece(halves[-1], 0)
    for p in range(1, n_pieces):
        y_last = y_last + out_piece(halves[-1], p)
    y_s[:, halves[-1]] += y_last

    @pl.when(j == nj - 1)
    def _finish():
        y = h_ref[...] + y_s[...].T
        if fg_ref is not None:
            y = _rms(y, fg_ref[...])
        out_ref[...] = y


def _peer(h, g, wq, k1, k2, u, vt, final_g=None, tm=512, ce=1024, th=256):
    t_tokens = h.shape[0]
    tm = min(tm, t_tokens)
    full = lambda shape: pl.BlockSpec(shape, lambda i, j: (0, 0))
    in_specs = [pl.BlockSpec((tm, D_MODEL), lambda i, j: (i, 0)), full((1, D_MODEL)),
                full((D_MODEL, 2 * PEER_HEADS * N_KEYS)), full((N_KEYS, N_KEYS)), full((N_KEYS, N_KEYS)),
                pl.BlockSpec((ce, D_MODEL), lambda i, j: (j, 0)),
                pl.BlockSpec((D_MODEL, ce), lambda i, j: (0, j))]
    args = [h, g, wq, k1, k2, u, vt]
    kern = _peer_kernel_plain
    if final_g is not None:
        in_specs.append(full((1, D_MODEL)))
        args.append(final_g)
        kern = _peer_kernel_final
    head_buf = pltpu.VMEM((PEER_HEADS, N_KEYS, tm), F32)
    return pl.pallas_call(
        functools.partial(kern, ce=ce, th=th),
        out_shape=jax.ShapeDtypeStruct((t_tokens, D_MODEL), F32),
        grid=(t_tokens // tm, N_EXPERTS // ce),
        in_specs=in_specs,
        out_specs=pl.BlockSpec((tm, D_MODEL), lambda i, j: (i, 0)),
        scratch_shapes=[pltpu.VMEM((tm, D_MODEL), BF16), head_buf, head_buf,
                        pltpu.VMEM((PEER_HEADS, tm), F32),
                        pltpu.VMEM((PEER_TOPK, th), F32), pltpu.VMEM((PEER_TOPK, th), F32),
                        pltpu.VMEM((_CAND_ROWS, th), F32), pltpu.VMEM((PEER_TOPK, th), F32),
                        pltpu.VMEM((D_MODEL, tm), F32), pltpu.VMEM((ce, tm), BF16)],
        compiler_params=_params(("parallel", "arbitrary")),
        name="peer_ffn",
    )(*args)


def _rope_tables(positions):
    inv_freq = ROPE_THETA ** (-jnp.arange(0, HEAD_DIM, 2, dtype=F32) / HEAD_DIM)
    ang = positions.astype(F32).reshape(-1, 1) * inv_freq
    cos, sin = jnp.cos(ang), jnp.sin(ang)
    return jnp.concatenate([cos] * 4, axis=-1), jnp.concatenate([-sin, sin, -sin, sin], axis=-1)


def _split_w_in(w_in):
    sizes = (WIDTH,) * 6 + (N_HEADS, WIDTH, C_KV_HEADS * HEAD_DIM, C_KV_HEADS * HEAD_DIM, 3 * D_MODEL)
    parts, start = [], 0
    for size in sizes:
        parts.append(w_in[:, start:start + size])
        start += size
    return parts


_C_HEAD_ORDER = (0, 3, 1, 4, 2, 5)


def _permute_heads(w, axis):
    shape = w.shape
    w = w.reshape(shape[:axis] + (N_HEADS, HEAD_DIM) + shape[axis + 1:])
    w = jnp.take(w, jnp.array(_C_HEAD_ORDER), axis=axis)
    return w.reshape(shape)


def _mixer(h, cos4, sin4, batch, norm_g, w_in, b_forget, sinks, b_gate, w_br_a, w_br_b, w_br_c, w_out):
    t_tokens = h.shape[0]
    seq = t_tokens // batch
    qa, ka, va, qb, kb, vb, fb, qc, kc, vc, wg = _split_w_in(w_in)
    w_f = jnp.pad(fb, ((0, 0), (0, LANE - N_HEADS)))
    w_att = jnp.concatenate([qa, ka, va, qb, kb, vb, _permute_heads(qc, 1), kc, vc, w_f], axis=1).astype(BF16)
    qkv, f = _in_proj(h, norm_g.reshape(1, -1), w_att, cos4, sin4)

    qkv3 = qkv.reshape(batch, seq, -1)
    outs, lses = [], []
    for dil in A_DILATIONS:
        o, lse = _banded(qkv3, dil, COL_QA, COL_KA, COL_VA, A_MAX_DIST, kv_shared=False)
        outs.append(o)
        lses.append(lse)
    oa = _combine(outs, lses)

    sink_tab = jnp.take(sinks.astype(F32), jnp.array(_C_HEAD_ORDER))
    (oc,) = _banded(qkv3, 1, COL_QC, COL_KC, COL_VC, C_MAX_DIST, kv_shared=True, sink=sink_tab, want_lse=False)

    bf_row = jnp.pad(b_forget.astype(F32), (0, LANE - N_HEADS)).reshape(1, LANE)
    augq, augk = _cum_aug(f.reshape(batch, seq, LANE), bf_row)
    ob = _fox(qkv3, augq, augk).reshape(t_tokens, WIDTH)

    return _merge(h, norm_g.reshape(1, -1), oa, ob, oc, wg.astype(BF16), b_gate.reshape(1, -1).astype(F32),
                  w_br_a.astype(BF16), w_br_b.astype(BF16), _permute_heads(w_br_c, 0).astype(BF16),
                  w_out.astype(BF16))


def kernel(x, positions, norm1_g, w_in, b_forget, sinks, b_gate, w_br_a, w_br_b, w_br_c, w_out, norm2_g,
           peer_wq, peer_k1, peer_k2, peer_u, peer_v, final_g):
    batch, seq, _ = x.shape
    depth = w_in.shape[0]
    cos4, sin4 = _rope_tables(positions)
    h = x.reshape(batch * seq, D_MODEL)
    for l in range(depth):
        h = _mixer(h, cos4, sin4, batch, norm1_g[l], w_in[l], b_forget[l], sinks[l], b_gate[l],
                   w_br_a[l], w_br_b[l], w_br_c[l], w_out[l])
        h = _peer(h, norm2_g[l].reshape(1, -1), peer_wq[l].astype(BF16), peer_k1[l].astype(BF16),
                  peer_k2[l].astype(BF16), peer_u[l].astype(BF16), peer_v[l].T.astype(BF16),
                  final_g=final_g.reshape(1, -1) if l == depth - 1 else None)
    return h.reshape(batch, seq, D_MODEL)
```

```python
import functools
import math

import jax
import jax.numpy as jnp
from jax import lax
from jax.experimental import pallas as pl
from jax.experimental.pallas import tpu as pltpu

F32 = jnp.float32
BF16 = jnp.bfloat16

D_MODEL = 1024
HEAD_DIM = 64
N_HEADS = 6
N_PAIRS = N_HEADS // 2
C_KV_HEADS = 2
A_DILATIONS = (1, 4, 16)
A_MAX_DIST = 128
C_MAX_DIST = 127
BLOCK = 128
LANE = 128
ROPE_THETA = 10000.0
RMS_EPS = 1e-6
NEG_INF = -1e30
PEER_HEADS = 8
N_KEYS = 128
N_EXPERTS = N_KEYS * N_KEYS
PEER_TOPK = 16
WIDTH = N_HEADS * HEAD_DIM

COL_QA, COL_KA, COL_VA = 0, 3, 6
COL_QB, COL_KB, COL_VB = 9, 12, 15
COL_QC, COL_KC, COL_VC = 18, 21, 22
N_QKV_BLOCKS = 23
COL_F = 23
A_BLOCKS = 9
ROPE_BLOCKS = (0, 1, 2, 3, 4, 5, 18, 19, 20, 21)
QUERY_BLOCKS = (0, 1, 2, 9, 10, 11, 18, 19, 20)
QK_SCALE = HEAD_DIM ** -0.5
LOG2E = math.log2(math.e)

VMEM_LIMIT = 56 * 1024 * 1024


def _params(sem):
    return pltpu.CompilerParams(dimension_semantics=sem, vmem_limit_bytes=VMEM_LIMIT)


def _dot(a, b):
    return jnp.dot(a, b, preferred_element_type=F32)


def _dot_nt(a, b):
    return lax.dot_general(a, b, (((1,), (1,)), ((), ())), preferred_element_type=F32)


def _rms(x, g):
    var = jnp.mean(x * x, axis=-1, keepdims=True)
    return x * lax.rsqrt(var + RMS_EPS) * g


def _in_proj_kernel(h_ref, g_ref, w_ref, cos_ref, sin_ref, qkv_ref, f_ref, *rest):
    class_refs, proj_s, fin_s = rest[:-2], rest[-2], rest[-1]
    tm = h_ref.shape[0]
    xn = _rms(h_ref[...], g_ref[...]).astype(BF16)
    proj_s[...] = _dot(xn, w_ref[...])
    cos = cos_ref[...]
    sin = sin_ref[...]
    lane = lax.broadcasted_iota(jnp.int32, cos.shape, 1)
    first_half = (lane % HEAD_DIM) < (HEAD_DIM // 2)
    for c in range(N_QKV_BLOCKS):
        t = proj_s[:, c * LANE:(c + 1) * LANE]
        if c in ROPE_BLOCKS:
            rot = jnp.where(first_half, pltpu.roll(t, LANE - HEAD_DIM // 2, 1), pltpu.roll(t, HEAD_DIM // 2, 1))
            t = t * cos + rot * sin
        if c in QUERY_BLOCKS:
            t = t * QK_SCALE
        if c < A_BLOCKS:
            fin_s[c] = t
        qkv_ref[:, c * LANE:(c + 1) * LANE] = t.astype(BF16)
    f_ref[...] = proj_s[:, COL_F * LANE:(COL_F + 1) * LANE]
    for dil, ref in zip(A_DILATIONS[1:], class_refs):
        for c in range(dil):
            for blk in range(A_BLOCKS):
                rows = fin_s[blk, pl.ds(c, tm // dil, stride=dil), :]
                ref[c, :, blk * LANE:(blk + 1) * LANE] = rows.astype(BF16)


def _in_proj(h, g, w_att, cos4, sin4, batch, tm=512):
    t_tokens = h.shape[0]
    seq = t_tokens // batch
    tiles = seq // tm
    n_cols = w_att.shape[1]
    a_cols = A_BLOCKS * LANE
    class_shapes = tuple(jax.ShapeDtypeStruct((batch, dil, seq // dil, a_cols), BF16) for dil in A_DILATIONS[1:])
    class_specs = tuple(pl.BlockSpec((None, dil, tm // dil, a_cols), lambda i: (i // tiles, 0, i % tiles, 0))
                        for dil in A_DILATIONS[1:])
    return pl.pallas_call(
        _in_proj_kernel,
        out_shape=(jax.ShapeDtypeStruct((t_tokens, N_QKV_BLOCKS * LANE), BF16),
                   jax.ShapeDtypeStruct((t_tokens, LANE), F32)) + class_shapes,
        grid=(t_tokens // tm,),
        in_specs=[pl.BlockSpec((tm, D_MODEL), lambda i: (i, 0)),
                  pl.BlockSpec((1, D_MODEL), lambda i: (0, 0)),
                  pl.BlockSpec((D_MODEL, n_cols), lambda i: (0, 0)),
                  pl.BlockSpec((tm, LANE), lambda i: (i, 0)),
                  pl.BlockSpec((tm, LANE), lambda i: (i, 0))],
        out_specs=(pl.BlockSpec((tm, N_QKV_BLOCKS * LANE), lambda i: (i, 0)),
                   pl.BlockSpec((tm, LANE), lambda i: (i, 0))) + class_specs,
        scratch_shapes=[pltpu.VMEM((tm, n_cols), F32), pltpu.VMEM((A_BLOCKS, tm, LANE), F32)],
        compiler_params=_params(("parallel",)),
        name="in_proj",
    )(h, g, w_att, cos4, sin4)


def _banded_kernel(*refs, max_dist, qb, kv_shared, has_sink, want_lse):
    refs = list(refs)
    q_ref, kp_ref, kc_ref, vp_ref, vc_ref = refs[:5]
    pos = 5
    sink_ref = None
    if has_sink:
        sink_ref = refs[pos]
        pos += 1
    o_ref = refs[pos]
    pos += 1
    lse_ref = None
    if want_lse:
        lse_ref = refs[pos]
        pos += 1
    kwin, vwin = refs[pos], refs[pos + 1]

    i = pl.program_id(1)
    kwin[0:BLOCK, :] = kp_ref[...]
    kwin[BLOCK:, :] = kc_ref[...]
    vwin[0:BLOCK, :] = vp_ref[...]
    vwin[BLOCK:, :] = vc_ref[...]

    row = lax.broadcasted_iota(jnp.int32, (BLOCK, 2 * BLOCK), 0)
    col = lax.broadcasted_iota(jnp.int32, (BLOCK, 2 * BLOCK), 1)
    dist = row + BLOCK - col
    band = (dist >= 0) & (dist <= max_dist)
    lane = lax.broadcasted_iota(jnp.int32, (1, LANE), 1)
    lower = lane < HEAD_DIM

    def body(sb, carry):
        off = pl.multiple_of(sb * BLOCK, BLOCK)
        valid = band & ((col >= BLOCK) | (i * qb + sb > 0))
        for p in range(N_PAIRS):
            q = q_ref[pl.ds(off, BLOCK), p * LANE:(p + 1) * LANE]
            kv_cols = slice(0, LANE) if kv_shared else slice(p * LANE, (p + 1) * LANE)
            k = kwin[pl.ds(off, 2 * BLOCK), kv_cols]
            v = vwin[pl.ds(off, 2 * BLOCK), kv_cols]
            o_pair = jnp.zeros((BLOCK, LANE), F32)
            lse_pair = jnp.zeros((BLOCK, LANE), F32)
            for hf in range(2):
                sel = lower if hf == 0 else jnp.logical_not(lower)
                qm = jnp.where(sel, q, jnp.zeros_like(q))
                s = _dot_nt(qm, k)
                s = jnp.where(valid, s, NEG_INF)
                m = jnp.max(s, axis=-1, keepdims=True)
                if has_sink:
                    sk = sink_ref[2 * p + hf]
                    m = jnp.maximum(m, sk)
                pr = jnp.exp(s - m)
                den = jnp.sum(pr, axis=-1, keepdims=True)
                if has_sink:
                    den = den + jnp.exp(sk - m)
                o = _dot(pr.astype(BF16), v) / den
                o_pair = jnp.where(sel, o, o_pair)
                if want_lse:
                    lse_pair = jnp.where(sel, m + jnp.log(den), lse_pair)
            o_ref[pl.ds(off, BLOCK), p * LANE:(p + 1) * LANE] = o_pair.astype(BF16)
            if want_lse:
                lse_ref[pl.ds(off, BLOCK), p * LANE:(p + 1) * LANE] = lse_pair
        return carry

    lax.fori_loop(0, qb, body, 0, unroll=True)


def _banded(x, qcol, kcol, vcol, max_dist, kv_shared, sink=None, want_lse=True, qb=4):
    n, length, _ = x.shape
    qb = min(qb, length // BLOCK)
    tq = qb * BLOCK
    kvw = LANE if kv_shared else WIDTH
    kblk = kcol if kv_shared else kcol // N_PAIRS
    vblk = vcol if kv_shared else vcol // N_PAIRS
    in_specs = [
        pl.BlockSpec((None, tq, WIDTH), lambda b, i: (b, i, qcol // N_PAIRS)),
        pl.BlockSpec((None, BLOCK, kvw), lambda b, i: (b, jnp.maximum(i * qb - 1, 0), kblk)),
        pl.BlockSpec((None, tq, kvw), lambda b, i: (b, i, kblk)),
        pl.BlockSpec((None, BLOCK, kvw), lambda b, i: (b, jnp.maximum(i * qb - 1, 0), vblk)),
        pl.BlockSpec((None, tq, kvw), lambda b, i: (b, i, vblk)),
    ]
    args = [x, x, x, x, x]
    if sink is not None:
        in_specs.append(pl.BlockSpec(memory_space=pltpu.SMEM))
        args.append(sink)
    out_shape = [jax.ShapeDtypeStruct((n, length, WIDTH), BF16)]
    out_specs = [pl.BlockSpec((None, tq, WIDTH), lambda b, i: (b, i, 0))]
    if want_lse:
        out_shape.append(jax.ShapeDtypeStruct((n, length, WIDTH), F32))
        out_specs.append(pl.BlockSpec((None, tq, WIDTH), lambda b, i: (b, i, 0)))
    kern = functools.partial(_banded_kernel, max_dist=max_dist, qb=qb, kv_shared=kv_shared,
                             has_sink=sink is not None, want_lse=want_lse)
    return pl.pallas_call(
        kern,
        out_shape=tuple(out_shape),
        grid=(n, length // tq),
        in_specs=in_specs,
        out_specs=tuple(out_specs),
        scratch_shapes=[pltpu.VMEM((tq + BLOCK, kvw), BF16), pltpu.VMEM((tq + BLOCK, kvw), BF16)],
        compiler_params=_params(("parallel", "parallel")),
        name="banded_attention",
    )(*args)


def _combine_kernel(o1_ref, l1_ref, *rest):
    n_cls = len(A_DILATIONS) - 1
    o_refs, l_refs = rest[:n_cls], rest[n_cls:2 * n_cls]
    out_ref = rest[2 * n_cls]
    o_s, l_s = rest[2 * n_cls + 1:3 * n_cls + 1], rest[3 * n_cls + 1:]
    tm = o1_ref.shape[0]
    for dil, o_ref, l_ref, os_, ls_ in zip(A_DILATIONS[1:], o_refs, l_refs, o_s, l_s):
        for c in range(dil):
            for p in range(N_PAIRS):
                cols = slice(p * LANE, (p + 1) * LANE)
                os_[p, pl.ds(c, tm // dil, stride=dil), :] = o_ref[c, :, cols].astype(F32)
                ls_[p, pl.ds(c, tm // dil, stride=dil), :] = l_ref[c, :, cols]
    for p in range(N_PAIRS):
        cols = slice(p * LANE, (p + 1) * LANE)
        lses = [l1_ref[:, cols]] + [ls_[p] for ls_ in l_s]
        outs = [o1_ref[:, cols].astype(F32)] + [os_[p] for os_ in o_s]
        m = functools.reduce(jnp.maximum, lses)
        ws = [jnp.exp(l - m) for l in lses]
        num = functools.reduce(lambda a, b: a + b, [w * o for w, o in zip(ws, outs)])
        out_ref[:, cols] = (num / functools.reduce(lambda a, b: a + b, ws)).astype(BF16)


def _combine(o1, l1, class_outs, class_lses, batch, tm=1024):
    t_tokens = o1.shape[0]
    tiles = t_tokens // batch // tm
    tok = pl.BlockSpec((tm, WIDTH), lambda i: (i, 0))
    cls = [pl.BlockSpec((None, dil, tm // dil, WIDTH), lambda i: (i // tiles, 0, i % tiles, 0))
           for dil in A_DILATIONS[1:]]
    n_cls = len(cls)
    return pl.pallas_call(
        _combine_kernel,
        out_shape=jax.ShapeDtypeStruct((t_tokens, WIDTH), BF16),
        grid=(t_tokens // tm,),
        in_specs=[tok, tok] + cls + cls,
        out_specs=tok,
        scratch_shapes=[pltpu.VMEM((N_PAIRS, tm, LANE), F32)] * (2 * n_cls),
        compiler_params=_params(("parallel",)),
        name="combine_patterns",
    )(o1, l1, *class_outs, *class_lses)


CUM_BLOCK = 256
BF16_ROWS = 16
UNDERFLOW = 104.0
NORM_SLACK = 1.02


def _cum_kernel(f_ref, bf_ref, augq_ref, augk_ref, carry_s):
    seq = f_ref.shape[0]

    @pl.when(pl.program_id(1) == 0)
    def _():
        carry_s[...] = jnp.zeros_like(carry_s)

    r = lax.broadcasted_iota(jnp.int32, (LANE, LANE), 0)
    rr = lax.broadcasted_iota(jnp.int32, (CUM_BLOCK, CUM_BLOCK), 0)
    cc = lax.broadcasted_iota(jnp.int32, (CUM_BLOCK, CUM_BLOCK), 1)
    tril = jnp.where(rr >= cc, 1.0, 0.0).astype(F32)
    lane = lax.broadcasted_iota(jnp.int32, (CUM_BLOCK, LANE), 1) % HEAD_DIM
    bias = bf_ref[...]

    def body(blk, carry):
        off = pl.multiple_of(blk * CUM_BLOCK, CUM_BLOCK)
        x = f_ref[pl.ds(off, CUM_BLOCK), :] + bias
        log_f = jnp.minimum(x, 0.0) - jnp.log1p(jnp.exp(-jnp.abs(x)))
        cum_all = jnp.dot(tril, log_f, preferred_element_type=F32, precision=lax.Precision.HIGHEST) + carry
        for head in range(N_HEADS):
            sel = jnp.where(r == head, 1.0, 0.0).astype(F32)
            cum = jnp.dot(cum_all, sel, preferred_element_type=F32, precision=lax.Precision.HIGHEST)
            c1 = cum.astype(BF16).astype(F32)
            c2 = (cum - c1).astype(BF16).astype(F32)
            c3 = (cum - c1 - c2).astype(BF16).astype(F32)
            piece = jnp.where(lane % 3 == 0, c1, jnp.where(lane % 3 == 1, c2, c3))
            one = jnp.ones_like(cum)
            zero = jnp.zeros_like(cum)
            aq = jnp.where(lane < 3, piece, jnp.where(lane < 6, one, zero))
            ak = jnp.where(lane < 3, one, jnp.where(lane < 6, -piece, zero))
            augq_ref[head, pl.ds(off, CUM_BLOCK), :] = aq.astype(BF16)
            augk_ref[head, pl.ds(off, CUM_BLOCK), :] = ak.astype(BF16)
        return cum_all[CUM_BLOCK - 1:CUM_BLOCK, :]

    carry_s[...] = lax.fori_loop(0, seq // CUM_BLOCK, body, carry_s[...])


def _cum_aug(f, b_forget_row, ts=2048):
    b, seq, _ = f.shape
    ts = min(ts, seq)
    out = jax.ShapeDtypeStruct((b, N_HEADS, seq, LANE), BF16)
    ospec = pl.BlockSpec((None, N_HEADS, ts, LANE), lambda bi, si: (bi, 0, si, 0))
    return pl.pallas_call(
        _cum_kernel,
        out_shape=(out, out),
        grid=(b, seq // ts),
        in_specs=[pl.BlockSpec((None, ts, LANE), lambda bi, si: (bi, si, 0)),
                  pl.BlockSpec((1, LANE), lambda bi, si: (0, 0))],
        out_specs=(ospec, ospec),
        scratch_shapes=[pltpu.VMEM((1, LANE), F32)],
        compiler_params=_params(("parallel", "arbitrary")),
        name="forget_cumsum",
    )(f, b_forget_row)


def _fox_kernel(q_ref, k_ref, v_ref, aq_ref, ak_ref, o_ref, kaug, qaug, kstat, *, tq, tk):
    qi = pl.program_id(2)
    n_kb = k_ref.shape[0] // tk
    n_sub = tq // tk
    lane = lax.broadcasted_iota(jnp.int32, (1, LANE), 1)
    lower = lane < HEAD_DIM
    hr = lax.broadcasted_iota(jnp.int32, (LANE, LANE), 0) // HEAD_DIM
    hc = lax.broadcasted_iota(jnp.int32, (LANE, LANE), 1) // HEAD_DIM
    same_head = jnp.where(hr == hc, 1.0, 0.0).astype(BF16)

    def max_sq_norm(x):
        xf = x.astype(F32)
        return jnp.max(_dot((xf * xf).astype(BF16), same_head), axis=0, keepdims=True)

    def lane_sum(row_vec, lo, hi):
        return jnp.sum(jnp.where((lane >= lo) & (lane < hi), row_vec.astype(F32), 0.0), axis=1, keepdims=True)

    @pl.when(qi == 0)
    def _():
        k = k_ref[...]
        kaug[0] = jnp.where(lower, k, ak_ref[0])
        kaug[1] = jnp.where(lower, ak_ref[1], k)

        def stats(kb, carry):
            kn0, kn1, nc0, nc1 = carry
            off = pl.multiple_of(kb * tk, tk)
            n2 = max_sq_norm(k_ref[pl.ds(off, tk), :])
            tail = pl.multiple_of(off + tk - BF16_ROWS, BF16_ROWS)
            here = lane == kb
            kn0 = jnp.where(here, n2[:, 0:1], kn0)
            kn1 = jnp.where(here, n2[:, HEAD_DIM:HEAD_DIM + 1], kn1)
            last0 = ak_ref[0, pl.ds(tail, BF16_ROWS), :][BF16_ROWS - 1:BF16_ROWS, :]
            last1 = ak_ref[1, pl.ds(tail, BF16_ROWS), :][BF16_ROWS - 1:BF16_ROWS, :]
            nc0 = jnp.where(here, lane_sum(last0, 3, 6), nc0)
            nc1 = jnp.where(here, lane_sum(last1, 3, 6), nc1)
            return kn0, kn1, nc0, nc1

        zero = jnp.zeros((1, LANE), F32)
        kn0, kn1, nc0, nc1 = lax.fori_loop(0, n_kb, stats, (zero, zero, zero, zero))
        kstat[0:1, :] = kn0
        kstat[1:2, :] = kn1
        kstat[2:3, :] = nc0
        kstat[3:4, :] = nc1

    q = q_ref[...]
    qaug[0] = jnp.where(lower, q, aq_ref[0])
    qaug[1] = jnp.where(lower, aq_ref[1], q)
    row = lax.broadcasted_iota(jnp.int32, (tq, tk), 0)
    col = lax.broadcasted_iota(jnp.int32, (tq, tk), 1)

    qn = max_sq_norm(q)
    first = None
    for hf in range(2):
        qn2 = qn[:, hf * HEAD_DIM:hf * HEAD_DIM + 1]
        kn2 = kstat[hf:hf + 1, :]
        kn2_all = jnp.max(kn2, axis=1, keepdims=True)
        cum_q = lane_sum(aq_ref[hf, 0:1, :], 0, 3)
        upper = jnp.sqrt(qn2 * kn2) * NORM_SLACK + cum_q + kstat[2 + hf:3 + hf, :] + 1.0
        floor = -jnp.sqrt(qn2 * kn2_all) * NORM_SLACK
        needed = (upper >= floor - UNDERFLOW) & (lane < n_kb)
        first_h = jnp.min(jnp.where(needed, lane.astype(F32), float(n_kb)))
        first = first_h if first is None else jnp.minimum(first, first_h)
    kb_start = jnp.minimum(first.astype(jnp.int32), qi * n_sub)

    def step(kb, carry, mask_shift):
        off = pl.multiple_of(kb * tk, tk)
        v = v_ref[pl.ds(off, tk), :]
        new = []
        for hf in range(2):
            m, l, acc = carry[hf]
            s = _dot_nt(qaug[hf], kaug[hf, pl.ds(off, tk), :])
            if mask_shift is not None:
                s = jnp.where(col + mask_shift <= row, s, NEG_INF)
            m_new = jnp.maximum(m, jnp.max(s, axis=-1, keepdims=True))
            alpha = jnp.exp(m - m_new)
            pr = jnp.exp(s - m_new)
            l = alpha * l + jnp.sum(pr, axis=-1, keepdims=True)
            acc = alpha * acc + _dot(pr.astype(BF16), v)
            new.append((m_new, l, acc))
        return tuple(new)

    one = (jnp.full((tq, 1), NEG_INF, F32), jnp.zeros((tq, 1), F32), jnp.zeros((tq, LANE), F32))
    carry = lax.fori_loop(kb_start, qi * n_sub, functools.partial(step, mask_shift=None), (one, one))
    for d in range(n_sub):
        carry = step(qi * n_sub + d, carry, d * tk)
    (_, l0, acc0), (_, l1, acc1) = carry
    o_ref[...] = jnp.where(lower, acc0 / l0, acc1 / l1).astype(BF16)


def _fox(qkv, augq, augk, tq=512, tk=256):
    b, seq, _ = qkv.shape
    return pl.pallas_call(
        functools.partial(_fox_kernel, tq=tq, tk=tk),
        out_shape=jax.ShapeDtypeStruct((b, seq, WIDTH), BF16),
        grid=(b, N_PAIRS, seq // tq),
        in_specs=[pl.BlockSpec((None, tq, LANE), lambda bi, p, i: (bi, i, COL_QB + p)),
                  pl.BlockSpec((None, seq, LANE), lambda bi, p, i: (bi, 0, COL_KB + p)),
                  pl.BlockSpec((None, seq, LANE), lambda bi, p, i: (bi, 0, COL_VB + p)),
                  pl.BlockSpec((None, 2, tq, LANE), lambda bi, p, i: (bi, p, i, 0)),
                  pl.BlockSpec((None, 2, seq, LANE), lambda bi, p, i: (bi, p, 0, 0))],
        out_specs=pl.BlockSpec((None, tq, LANE), lambda bi, p, i: (bi, i, p)),
        scratch_shapes=[pltpu.VMEM((2, seq, LANE), BF16), pltpu.VMEM((2, tq, LANE), BF16),
                        pltpu.VMEM((8, LANE), F32)],
        compiler_params=_params(("parallel", "parallel", "arbitrary")),
        name="forgetting_attention",
    )(qkv, qkv, qkv, augq, augk)


def _merge_kernel(h_ref, g_ref, oa_ref, ob_ref, oc_ref, wg_ref, bg_ref, wa_ref, wb_ref, wc_ref, wo_ref, out_ref):
    h = h_ref[...]
    xn = _rms(h, g_ref[...]).astype(BF16)
    merged = jnp.zeros(h.shape, F32)
    for idx, (o_ref, w_ref) in enumerate(((oa_ref, wa_ref), (ob_ref, wb_ref), (oc_ref, wc_ref))):
        cols = slice(idx * D_MODEL, (idx + 1) * D_MODEL)
        gate = jax.nn.sigmoid(_dot(xn, wg_ref[:, cols]) + bg_ref[:, cols])
        merged = merged + gate * _dot(o_ref[...], w_ref[...])
    out_ref[...] = h + _dot(merged.astype(BF16), wo_ref[...])


def _merge(h, g, oa, ob, oc, w_gate, b_gate, w_a, w_b, w_c, w_out, tm=256):
    t_tokens = h.shape[0]
    full = lambda shape: pl.BlockSpec(shape, lambda i: (0, 0))
    ospec = pl.BlockSpec((tm, WIDTH), lambda i: (i, 0))
    return pl.pallas_call(
        _merge_kernel,
        out_shape=jax.ShapeDtypeStruct((t_tokens, D_MODEL), F32),
        grid=(t_tokens // tm,),
        in_specs=[pl.BlockSpec((tm, D_MODEL), lambda i: (i, 0)), full((1, D_MODEL)),
                  ospec, ospec, ospec,
                  full((D_MODEL, 3 * D_MODEL)), full((1, 3 * D_MODEL)),
                  full((WIDTH, D_MODEL)), full((WIDTH, D_MODEL)), full((WIDTH, D_MODEL)),
                  full((D_MODEL, D_MODEL))],
        out_specs=pl.BlockSpec((tm, D_MODEL), lambda i: (i, 0)),
        compiler_params=_params(("parallel",)),
        name="gated_merge",
    )(h, g, oa, ob, oc, w_gate, b_gate, w_a, w_b, w_c, w_out)


_CAND_COUNTS = tuple(PEER_TOPK // (i + 1) for i in range(PEER_TOPK))
_CAND_ROWS = 56
PIECE = 256


def _top16_rows(src, dst_ref):
    cur = src
    for r in range(PEER_TOPK):
        mx = jnp.max(cur, axis=0, keepdims=True)
        dst_ref[r:r + 1, :] = mx
        if r + 1 < PEER_TOPK:
            cur = jnp.where(cur == mx, NEG_INF, cur)


def _sort16_pairs():
    n, out, p = PEER_TOPK, [], 1
    while p < n:
        k = p
        while k >= 1:
            for j in range(k % p, n - k, 2 * k):
                for i in range(min(k, n - j - k)):
                    if (i + j) // (2 * p) == (i + j + k) // (2 * p):
                        out.append((i + j, i + j + k))
            k //= 2
        p *= 2
    return tuple(out)


_SORT16 = _sort16_pairs()
SUBLANES = 8


def _top16_of_keys(src, dst_ref):
    rows = [src[SUBLANES * k:SUBLANES * (k + 1), :] for k in range(N_KEYS // SUBLANES)]

    def exchange(i, j):
        rows[i], rows[j] = jnp.maximum(rows[i], rows[j]), jnp.minimum(rows[i], rows[j])

    for i, j in _SORT16:
        exchange(i, j)
    for shift in (4, 2, 1):
        other = [pltpu.roll(r, shift, 0) for r in rows]
        rows = [jnp.maximum(rows[k], other[PEER_TOPK - 1 - k]) for k in range(PEER_TOPK)]
        for d in (8, 4, 2, 1):
            for k in range(PEER_TOPK):
                if k & d == 0:
                    exchange(k, k + d)
    for r in range(PEER_TOPK):
        dst_ref[r:r + 1, :] = rows[r][0:1, :]


def _peer_kernel_plain(h_ref, g_ref, wq_ref, k1_ref, k2_ref, u_ref, vt_ref, out_ref, *scratch, ce, th):
    _peer_body(h_ref, g_ref, wq_ref, k1_ref, k2_ref, u_ref, vt_ref, None, out_ref, *scratch, ce=ce, th=th)


def _peer_kernel_final(h_ref, g_ref, wq_ref, k1_ref, k2_ref, u_ref, vt_ref, fg_ref, out_ref, *scratch, ce, th):
    _peer_body(h_ref, g_ref, wq_ref, k1_ref, k2_ref, u_ref, vt_ref, fg_ref, out_ref, *scratch, ce=ce, th=th)


def _peer_body(h_ref, g_ref, wq_ref, k1_ref, k2_ref, u_ref, vt_ref, fg_ref, out_ref,
               xn_s, s1_s, s2_s, tau_s, v1_s, v2_s, cand_s, top_s, y_s, w_s, *, ce, th):
    j = pl.program_id(1)
    nj = pl.num_programs(1)
    tm = h_ref.shape[0]
    a_per_step = ce // N_KEYS
    halves = [slice(t0, t0 + th) for t0 in range(0, tm, th)]

    def fill_candidates():
        off = 0
        for i, cnt in enumerate(_CAND_COUNTS):
            cand_s[off:off + cnt, :] = v1_s[i:i + 1, :] + v2_s[0:cnt, :]
            off += cnt

    @pl.when(j == 0)
    def _route():
        xn_s[...] = _rms(h_ref[...], g_ref[...]).astype(BF16)
        y_s[...] = jnp.zeros_like(y_s)
        cand_s[...] = jnp.full(cand_s.shape, NEG_INF, F32)
        for tok in halves:
            xn = xn_s[tok, :]
            for hd in range(PEER_HEADS):
                q1 = _dot(xn, wq_ref[:, (2 * hd) * N_KEYS:(2 * hd + 1) * N_KEYS]).astype(BF16)
                q2 = _dot(xn, wq_ref[:, (2 * hd + 1) * N_KEYS:(2 * hd + 2) * N_KEYS]).astype(BF16)
                s1 = _dot_nt(k1_ref[...], q1)
                s2 = _dot_nt(k2_ref[...], q2)
                _top16_of_keys(s1, v1_s)
                _top16_of_keys(s2, v2_s)
                fill_candidates()
                _top16_rows(cand_s[...], top_s)
                top = top_s[...]
                mx = top[0:1, :]
                shift = mx + jnp.log(jnp.sum(jnp.exp(top - mx), axis=0, keepdims=True))
                s1_s[hd, :, tok] = (s1 - shift) * LOG2E - 1.0
                s2_s[hd, :, tok] = s2 * LOG2E
                v1_s[...] = (v1_s[...] - shift) * LOG2E - 1.0
                v2_s[...] = v2_s[...] * LOG2E
                fill_candidates()
                _top16_rows(cand_s[...], top_s)
                tau_s[hd:hd + 1, tok] = top_s[PEER_TOPK - 1:PEER_TOPK, :]

    n_pieces = ce // PIECE

    def act_piece(tok, p):
        return _dot_nt(u_ref[p * PIECE:(p + 1) * PIECE, :], xn_s[tok, :])

    def mix_piece(tok, p, act):
        for aa in range(PIECE // N_KEYS):
            a = j * a_per_step + p * (PIECE // N_KEYS) + aa
            coef = jnp.zeros((N_KEYS, th), F32)
            for hd in range(PEER_HEADS):
                s = s2_s[hd, :, tok] + s1_s[hd, pl.ds(a, 1), tok]
                coef = coef + jnp.where(s >= tau_s[hd:hd + 1, tok], jnp.exp2(s), 0.0)
            x = act[aa * N_KEYS:(aa + 1) * N_KEYS, :]
            gelu2 = x * (1.0 + lax.erf(x * math.sqrt(0.5)))
            rows = slice(p * PIECE + aa * N_KEYS, p * PIECE + (aa + 1) * N_KEYS)
            w_s[rows, tok] = (coef * gelu2).astype(BF16)

    def out_piece(tok, p):
        return _dot(vt_ref[:, p * PIECE:(p + 1) * PIECE], w_s[p * PIECE:(p + 1) * PIECE, tok])

    acts = [act_piece(halves[0], p) for p in range(n_pieces)]
    for hi, tok in enumerate(halves):
        nxt = halves[hi + 1] if hi + 1 < len(halves) else None
        prv = halves[hi - 1] if hi > 0 else None
        nxt_acts, y_prev = [], None
        for p in range(n_pieces):
            mix_piece(tok, p, acts[p])
            if nxt is not None:
                nxt_acts.append(act_piece(nxt, p))
            if prv is not None:
                yp = out_piece(prv, p)
                y_prev = yp if y_prev is None else y_prev + yp
        if prv is not None:
            y_s[:, prv] += y_prev
        acts = nxt_acts
    y_last = out_piece(halves[-1], 0)
    for p in range(1, n_pieces):
        y_last = y_last + out_piece(halves[-1], p)
    y_s[:, halves[-1]] += y_last

    @pl.when(j == nj - 1)
    def _finish():
        y = h_ref[...] + y_s[...].T
        if fg_ref is not None:
            y = _rms(y, fg_ref[...])
        out_ref[...] = y


def _peer(h, g, wq, k1, k2, u, vt, final_g=None, tm=512, ce=1024, th=256):
    t_tokens = h.shape[0]
    tm = min(tm, t_tokens)
    full = lambda shape: pl.BlockSpec(shape, lambda i, j: (0, 0))
    in_specs = [pl.BlockSpec((tm, D_MODEL), lambda i, j: (i, 0)), full((1, D_MODEL)),
                full((D_MODEL, 2 * PEER_HEADS * N_KEYS)), full((N_KEYS, N_KEYS)), full((N_KEYS, N_KEYS)),
                pl.BlockSpec((ce, D_MODEL), lambda i, j: (j, 0)),
                pl.BlockSpec((D_MODEL, ce), lambda i, j: (0, j))]
    args = [h, g, wq, k1, k2, u, vt]
    kern = _peer_kernel_plain
    if final_g is not None:
        in_specs.append(full((1, D_MODEL)))
        args.append(final_g)
        kern = _peer_kernel_final
    head_buf = pltpu.VMEM((PEER_HEADS, N_KEYS, tm), F32)
    return pl.pallas_call(
        functools.partial(kern, ce=ce, th=th),
        out_shape=jax.ShapeDtypeStruct((t_tokens, D_MODEL), F32),
        grid=(t_tokens // tm, N_EXPERTS // ce),
        in_specs=in_specs,
        out_specs=pl.BlockSpec((tm, D_MODEL), lambda i, j: (i, 0)),
        scratch_shapes=[pltpu.VMEM((tm, D_MODEL), BF16), head_buf, head_buf,
                        pltpu.VMEM((PEER_HEADS, tm), F32),
                        pltpu.VMEM((PEER_TOPK, th), F32), pltpu.VMEM((PEER_TOPK, th), F32),
                        pltpu.VMEM((_CAND_ROWS, th), F32), pltpu.VMEM((PEER_TOPK, th), F32),
                        pltpu.VMEM((D_MODEL, tm), F32), pltpu.VMEM((ce, tm), BF16)],
        compiler_params=_params(("parallel", "arbitrary")),
        name="peer_ffn",
    )(*args)


def _rope_tables(positions):
    inv_freq = ROPE_THETA ** (-jnp.arange(0, HEAD_DIM, 2, dtype=F32) / HEAD_DIM)
    ang = positions.astype(F32).reshape(-1, 1) * inv_freq
    cos, sin = jnp.cos(ang), jnp.sin(ang)
    return jnp.concatenate([cos] * 4, axis=-1), jnp.concatenate([-sin, sin, -sin, sin], axis=-1)


def _split_w_in(w_in):
    sizes = (WIDTH,) * 6 + (N_HEADS, WIDTH, C_KV_HEADS * HEAD_DIM, C_KV_HEADS * HEAD_DIM, 3 * D_MODEL)
    parts, start = [], 0
    for size in sizes:
        parts.append(w_in[:, start:start + size])
        start += size
    return parts


_C_HEAD_ORDER = (0, 3, 1, 4, 2, 5)


def _permute_heads(w, axis):
    shape = w.shape
    w = w.reshape(shape[:axis] + (N_HEADS, HEAD_DIM) + shape[axis + 1:])
    w = jnp.take(w, jnp.array(_C_HEAD_ORDER), axis=axis)
    return w.reshape(shape)


def _mixer(h, cos4, sin4, batch, norm_g, w_in, b_forget, sinks, b_gate, w_br_a, w_br_b, w_br_c, w_out):
    t_tokens = h.shape[0]
    seq = t_tokens // batch
    qa, ka, va, qb, kb, vb, fb, qc, kc, vc, wg = _split_w_in(w_in)
    w_f = jnp.pad(fb, ((0, 0), (0, LANE - N_HEADS)))
    w_att = jnp.concatenate([qa, ka, va, qb, kb, vb, _permute_heads(qc, 1), kc, vc, w_f], axis=1).astype(BF16)
    qkv, f, *qkv_classes = _in_proj(h, norm_g.reshape(1, -1), w_att, cos4, sin4, batch)
    qkv3 = qkv.reshape(batch, seq, -1)

    o1, l1 = _banded(qkv3, COL_QA, COL_KA, COL_VA, A_MAX_DIST, kv_shared=False)
    class_outs, class_lses = [], []
    for dil, x in zip(A_DILATIONS[1:], qkv_classes):
        o, lse = _banded(x.reshape(batch * dil, seq // dil, -1), COL_QA, COL_KA, COL_VA, A_MAX_DIST, kv_shared=False)
        class_outs.append(o.reshape(batch, dil, seq // dil, WIDTH))
        class_lses.append(lse.reshape(batch, dil, seq // dil, WIDTH))
    oa = _combine(o1.reshape(t_tokens, WIDTH), l1.reshape(t_tokens, WIDTH), class_outs, class_lses, batch)

    sink_tab = jnp.take(sinks.astype(F32), jnp.array(_C_HEAD_ORDER))
    (oc,) = _banded(qkv3, COL_QC, COL_KC, COL_VC, C_MAX_DIST, kv_shared=True, sink=sink_tab, want_lse=False)
    oc = oc.reshape(t_tokens, WIDTH)

    bf_row = jnp.pad(b_forget.astype(F32), (0, LANE - N_HEADS)).reshape(1, LANE)
    augq, augk = _cum_aug(f.reshape(batch, seq, LANE), bf_row)
    ob = _fox(qkv3, augq, augk).reshape(t_tokens, WIDTH)

    return _merge(h, norm_g.reshape(1, -1), oa, ob, oc, wg.astype(BF16), b_gate.reshape(1, -1).astype(F32),
                  w_br_a.astype(BF16), w_br_b.astype(BF16), _permute_heads(w_br_c, 0).astype(BF16),
                  w_out.astype(BF16))


def kernel(x, positions, norm1_g, w_in, b_forget, sinks, b_gate, w_br_a, w_br_b, w_br_c, w_out, norm2_g,
           peer_wq, peer_k1, peer_k2, peer_u, peer_v, final_g):
    batch, seq, _ = x.shape
    depth = w_in.shape[0]
    cos4, sin4 = _rope_tables(positions)
    h = x.reshape(batch * seq, D_MODEL)
    for l in range(depth):
        h = _mixer(h, cos4, sin4, batch, norm1_g[l], w_in[l], b_forget[l], sinks[l], b_gate[l],
                   w_br_a[l], w_br_b[l], w_br_c[l], w_out[l])
        h = _peer(h, norm2_g[l].reshape(1, -1), peer_wq[l].astype(BF16), peer_k1[l].astype(BF16),
                  peer_k2[l].astype(BF16), peer_u[l].astype(BF16), peer_v[l].T.astype(BF16),
                  final_g=final_g.reshape(1, -1) if l == depth - 1 else None)
    return h.reshape(batch, seq, D_MODEL)
```

```python
import functools
import math

import jax
import jax.numpy as jnp
from jax import lax
from jax.experimental import pallas as pl
from jax.experimental.pallas import tpu as pltpu

F32 = jnp.float32
BF16 = jnp.bfloat16

D_MODEL = 1024
HEAD_DIM = 64
N_HEADS = 6
N_PAIRS = N_HEADS // 2
C_KV_HEADS = 2
A_DILATIONS = (1, 4, 16)
A_MAX_DIST = 128
C_MAX_DIST = 127
BLOCK = 128
LANE = 128
ROPE_THETA = 10000.0
RMS_EPS = 1e-6
NEG_INF = -1e30
PEER_HEADS = 8
N_KEYS = 128
N_EXPERTS = N_KEYS * N_KEYS
PEER_TOPK = 16
WIDTH = N_HEADS * HEAD_DIM

COL_QA, COL_KA, COL_VA = 0, 3, 6
COL_QB, COL_KB, COL_VB = 9, 12, 15
COL_QC, COL_KC, COL_VC = 18, 21, 22
N_QKV_BLOCKS = 23
COL_F = 23
A_BLOCKS = 9
ROPE_BLOCKS = (0, 1, 2, 3, 4, 5, 18, 19, 20, 21)
QUERY_BLOCKS = (0, 1, 2, 9, 10, 11, 18, 19, 20)
QK_SCALE = HEAD_DIM ** -0.5
LOG2E = math.log2(math.e)

VMEM_LIMIT = 56 * 1024 * 1024


def _params(sem):
    return pltpu.CompilerParams(dimension_semantics=sem, vmem_limit_bytes=VMEM_LIMIT)


def _dot(a, b):
    return jnp.dot(a, b, preferred_element_type=F32)


def _dot_nt(a, b):
    return lax.dot_general(a, b, (((1,), (1,)), ((), ())), preferred_element_type=F32)


def _rms(x, g):
    var = jnp.mean(x * x, axis=-1, keepdims=True)
    return x * lax.rsqrt(var + RMS_EPS) * g


def _in_proj_kernel(h_ref, g_ref, w_ref, cos_ref, sin_ref, qkv_ref, f_ref, *rest):
    class_refs, proj_s, fin_s = rest[:-2], rest[-2], rest[-1]
    tm = h_ref.shape[0]
    xn = _rms(h_ref[...], g_ref[...]).astype(BF16)
    proj_s[...] = _dot(xn, w_ref[...])
    cos = cos_ref[...]
    sin = sin_ref[...]
    lane = lax.broadcasted_iota(jnp.int32, cos.shape, 1)
    first_half = (lane % HEAD_DIM) < (HEAD_DIM // 2)
    for c in range(N_QKV_BLOCKS):
        t = proj_s[:, c * LANE:(c + 1) * LANE]
        if c in ROPE_BLOCKS:
            rot = jnp.where(first_half, pltpu.roll(t, LANE - HEAD_DIM // 2, 1), pltpu.roll(t, HEAD_DIM // 2, 1))
            t = t * cos + rot * sin
        if c in QUERY_BLOCKS:
            t = t * QK_SCALE
        if c < A_BLOCKS:
            fin_s[c] = t
        qkv_ref[:, c * LANE:(c + 1) * LANE] = t.astype(BF16)
    f_ref[...] = proj_s[:, COL_F * LANE:(COL_F + 1) * LANE]
    for dil, ref in zip(A_DILATIONS[1:], class_refs):
        for c in range(dil):
            for blk in range(A_BLOCKS):
                rows = fin_s[blk, pl.ds(c, tm // dil, stride=dil), :]
                ref[c, :, blk * LANE:(blk + 1) * LANE] = rows.astype(BF16)


def _in_proj(h, g, w_att, cos4, sin4, batch, tm=512):
    t_tokens = h.shape[0]
    seq = t_tokens // batch
    tiles = seq // tm
    n_cols = w_att.shape[1]
    a_cols = A_BLOCKS * LANE
    class_shapes = tuple(jax.ShapeDtypeStruct((batch, dil, seq // dil, a_cols), BF16) for dil in A_DILATIONS[1:])
    class_specs = tuple(pl.BlockSpec((None, dil, tm // dil, a_cols), lambda i: (i // tiles, 0, i % tiles, 0))
                        for dil in A_DILATIONS[1:])
    return pl.pallas_call(
        _in_proj_kernel,
        out_shape=(jax.ShapeDtypeStruct((t_tokens, N_QKV_BLOCKS * LANE), BF16),
                   jax.ShapeDtypeStruct((t_tokens, LANE), F32)) + class_shapes,
        grid=(t_tokens // tm,),
        in_specs=[pl.BlockSpec((tm, D_MODEL), lambda i: (i, 0)),
                  pl.BlockSpec((1, D_MODEL), lambda i: (0, 0)),
                  pl.BlockSpec((D_MODEL, n_cols), lambda i: (0, 0)),
                  pl.BlockSpec((tm, LANE), lambda i: (i, 0)),
                  pl.BlockSpec((tm, LANE), lambda i: (i, 0))],
        out_specs=(pl.BlockSpec((tm, N_QKV_BLOCKS * LANE), lambda i: (i, 0)),
                   pl.BlockSpec((tm, LANE), lambda i: (i, 0))) + class_specs,
        scratch_shapes=[pltpu.VMEM((tm, n_cols), F32), pltpu.VMEM((A_BLOCKS, tm, LANE), F32)],
        compiler_params=_params(("parallel",)),
        name="in_proj",
    )(h, g, w_att, cos4, sin4)


def _banded_kernel(*refs, max_dist, qb, kv_shared, has_sink, want_lse):
    refs = list(refs)
    q_ref, kp_ref, kc_ref, vp_ref, vc_ref = refs[:5]
    pos = 5
    sink_ref = None
    if has_sink:
        sink_ref = refs[pos]
        pos += 1
    o_ref = refs[pos]
    pos += 1
    lse_ref = None
    if want_lse:
        lse_ref = refs[pos]
        pos += 1
    kwin, vwin = refs[pos], refs[pos + 1]

    i = pl.program_id(1)
    kwin[0:BLOCK, :] = kp_ref[...]
    kwin[BLOCK:, :] = kc_ref[...]
    vwin[0:BLOCK, :] = vp_ref[...]
    vwin[BLOCK:, :] = vc_ref[...]

    row = lax.broadcasted_iota(jnp.int32, (BLOCK, 2 * BLOCK), 0)
    col = lax.broadcasted_iota(jnp.int32, (BLOCK, 2 * BLOCK), 1)
    dist = row + BLOCK - col
    band = (dist >= 0) & (dist <= max_dist)
    lane = lax.broadcasted_iota(jnp.int32, (1, LANE), 1)
    lower = lane < HEAD_DIM

    def body(sb, carry):
        off = pl.multiple_of(sb * BLOCK, BLOCK)
        valid = band & ((col >= BLOCK) | (i * qb + sb > 0))
        for p in range(N_PAIRS):
            q = q_ref[pl.ds(off, BLOCK), p * LANE:(p + 1) * LANE]
            kv_cols = slice(0, LANE) if kv_shared else slice(p * LANE, (p + 1) * LANE)
            k = kwin[pl.ds(off, 2 * BLOCK), kv_cols]
            v = vwin[pl.ds(off, 2 * BLOCK), kv_cols]
            o_pair = jnp.zeros((BLOCK, LANE), F32)
            lse_pair = jnp.zeros((BLOCK, LANE), F32)
            for hf in range(2):
                sel = lower if hf == 0 else jnp.logical_not(lower)
                qm = jnp.where(sel, q, jnp.zeros_like(q))
                s = _dot_nt(qm, k)
                s = jnp.where(valid, s, NEG_INF)
                m = jnp.max(s, axis=-1, keepdims=True)
                if has_sink:
                    sk = sink_ref[2 * p + hf]
                    m = jnp.maximum(m, sk)
                pr = jnp.exp(s - m)
                den = jnp.sum(pr, axis=-1, keepdims=True)
                if has_sink:
                    den = den + jnp.exp(sk - m)
                o = _dot(pr.astype(BF16), v) / den
                o_pair = jnp.where(sel, o, o_pair)
                if want_lse:
                    lse_pair = jnp.where(sel, m + jnp.log(den), lse_pair)
            o_ref[pl.ds(off, BLOCK), p * LANE:(p + 1) * LANE] = o_pair.astype(BF16)
            if want_lse:
                lse_ref[pl.ds(off, BLOCK), p * LANE:(p + 1) * LANE] = lse_pair
        return carry

    lax.fori_loop(0, qb, body, 0, unroll=True)


def _banded(x, qcol, kcol, vcol, max_dist, kv_shared, sink=None, want_lse=True, qb=4):
    n, length, _ = x.shape
    qb = min(qb, length // BLOCK)
    tq = qb * BLOCK
    kvw = LANE if kv_shared else WIDTH
    kblk = kcol if kv_shared else kcol // N_PAIRS
    vblk = vcol if kv_shared else vcol // N_PAIRS
    in_specs = [
        pl.BlockSpec((None, tq, WIDTH), lambda b, i: (b, i, qcol // N_PAIRS)),
        pl.BlockSpec((None, BLOCK, kvw), lambda b, i: (b, jnp.maximum(i * qb - 1, 0), kblk)),
        pl.BlockSpec((None, tq, kvw), lambda b, i: (b, i, kblk)),
        pl.BlockSpec((None, BLOCK, kvw), lambda b, i: (b, jnp.maximum(i * qb - 1, 0), vblk)),
        pl.BlockSpec((None, tq, kvw), lambda b, i: (b, i, vblk)),
    ]
    args = [x, x, x, x, x]
    if sink is not None:
        in_specs.append(pl.BlockSpec(memory_space=pltpu.SMEM))
        args.append(sink)
    out_shape = [jax.ShapeDtypeStruct((n, length, WIDTH), BF16)]
    out_specs = [pl.BlockSpec((None, tq, WIDTH), lambda b, i: (b, i, 0))]
    if want_lse:
        out_shape.append(jax.ShapeDtypeStruct((n, length, WIDTH), F32))
        out_specs.append(pl.BlockSpec((None, tq, WIDTH), lambda b, i: (b, i, 0)))
    kern = functools.partial(_banded_kernel, max_dist=max_dist, qb=qb, kv_shared=kv_shared,
                             has_sink=sink is not None, want_lse=want_lse)
    return pl.pallas_call(
        kern,
        out_shape=tuple(out_shape),
        grid=(n, length // tq),
        in_specs=in_specs,
        out_specs=tuple(out_specs),
        scratch_shapes=[pltpu.VMEM((tq + BLOCK, kvw), BF16), pltpu.VMEM((tq + BLOCK, kvw), BF16)],
        compiler_params=_params(("parallel", "parallel")),
        name="banded_attention",
    )(*args)


def _combine_kernel(o1_ref, l1_ref, *rest):
    n_cls = len(A_DILATIONS) - 1
    o_refs, l_refs = rest[:n_cls], rest[n_cls:2 * n_cls]
    out_ref = rest[2 * n_cls]
    o_s, l_s = rest[2 * n_cls + 1:3 * n_cls + 1], rest[3 * n_cls + 1:]
    tm = o1_ref.shape[0]
    for dil, o_ref, l_ref, os_, ls_ in zip(A_DILATIONS[1:], o_refs, l_refs, o_s, l_s):
        for c in range(dil):
            for p in range(N_PAIRS):
                cols = slice(p * LANE, (p + 1) * LANE)
                os_[p, pl.ds(c, tm // dil, stride=dil), :] = o_ref[c, :, cols].astype(F32)
                ls_[p, pl.ds(c, tm // dil, stride=dil), :] = l_ref[c, :, cols]
    for p in range(N_PAIRS):
        cols = slice(p * LANE, (p + 1) * LANE)
        lses = [l1_ref[:, cols]] + [ls_[p] for ls_ in l_s]
        outs = [o1_ref[:, cols].astype(F32)] + [os_[p] for os_ in o_s]
        m = functools.reduce(jnp.maximum, lses)
        ws = [jnp.exp(l - m) for l in lses]
        num = functools.reduce(lambda a, b: a + b, [w * o for w, o in zip(ws, outs)])
        out_ref[:, cols] = (num / functools.reduce(lambda a, b: a + b, ws)).astype(BF16)


def _combine(o1, l1, class_outs, class_lses, batch, tm=1024):
    t_tokens = o1.shape[0]
    tiles = t_tokens // batch // tm
    tok = pl.BlockSpec((tm, WIDTH), lambda i: (i, 0))
    cls = [pl.BlockSpec((None, dil, tm // dil, WIDTH), lambda i: (i // tiles, 0, i % tiles, 0))
           for dil in A_DILATIONS[1:]]
    n_cls = len(cls)
    return pl.pallas_call(
        _combine_kernel,
        out_shape=jax.ShapeDtypeStruct((t_tokens, WIDTH), BF16),
        grid=(t_tokens // tm,),
        in_specs=[tok, tok] + cls + cls,
        out_specs=tok,
        scratch_shapes=[pltpu.VMEM((N_PAIRS, tm, LANE), F32)] * (2 * n_cls),
        compiler_params=_params(("parallel",)),
        name="combine_patterns",
    )(o1, l1, *class_outs, *class_lses)


CUM_BLOCK = 256
BF16_ROWS = 16
UNDERFLOW = 104.0
NORM_SLACK = 1.02


def _cum_kernel(f_ref, bf_ref, augq_ref, augk_ref, carry_s):
    seq = f_ref.shape[0]

    @pl.when(pl.program_id(1) == 0)
    def _():
        carry_s[...] = jnp.zeros_like(carry_s)

    r = lax.broadcasted_iota(jnp.int32, (LANE, LANE), 0)
    rr = lax.broadcasted_iota(jnp.int32, (CUM_BLOCK, CUM_BLOCK), 0)
    cc = lax.broadcasted_iota(jnp.int32, (CUM_BLOCK, CUM_BLOCK), 1)
    tril = jnp.where(rr >= cc, 1.0, 0.0).astype(F32)
    lane = lax.broadcasted_iota(jnp.int32, (CUM_BLOCK, LANE), 1) % HEAD_DIM
    bias = bf_ref[...]

    def body(blk, carry):
        off = pl.multiple_of(blk * CUM_BLOCK, CUM_BLOCK)
        x = f_ref[pl.ds(off, CUM_BLOCK), :] + bias
        log_f = jnp.minimum(x, 0.0) - jnp.log1p(jnp.exp(-jnp.abs(x)))
        cum_all = jnp.dot(tril, log_f, preferred_element_type=F32, precision=lax.Precision.HIGHEST) + carry
        for head in range(N_HEADS):
            sel = jnp.where(r == head, 1.0, 0.0).astype(F32)
            cum = jnp.dot(cum_all, sel, preferred_element_type=F32, precision=lax.Precision.HIGHEST)
            c1 = cum.astype(BF16).astype(F32)
            c2 = (cum - c1).astype(BF16).astype(F32)
            c3 = (cum - c1 - c2).astype(BF16).astype(F32)
            piece = jnp.where(lane % 3 == 0, c1, jnp.where(lane % 3 == 1, c2, c3))
            one = jnp.ones_like(cum)
            zero = jnp.zeros_like(cum)
            aq = jnp.where(lane < 3, piece, jnp.where(lane < 6, one, zero))
            ak = jnp.where(lane < 3, one, jnp.where(lane < 6, -piece, zero))
            augq_ref[head, pl.ds(off, CUM_BLOCK), :] = aq.astype(BF16)
            augk_ref[head, pl.ds(off, CUM_BLOCK), :] = ak.astype(BF16)
        return cum_all[CUM_BLOCK - 1:CUM_BLOCK, :]

    carry_s[...] = lax.fori_loop(0, seq // CUM_BLOCK, body, carry_s[...])


def _cum_aug(f, b_forget_row, ts=2048):
    b, seq, _ = f.shape
    ts = min(ts, seq)
    out = jax.ShapeDtypeStruct((b, N_HEADS, seq, LANE), BF16)
    ospec = pl.BlockSpec((None, N_HEADS, ts, LANE), lambda bi, si: (bi, 0, si, 0))
    return pl.pallas_call(
        _cum_kernel,
        out_shape=(out, out),
        grid=(b, seq // ts),
        in_specs=[pl.BlockSpec((None, ts, LANE), lambda bi, si: (bi, si, 0)),
                  pl.BlockSpec((1, LANE), lambda bi, si: (0, 0))],
        out_specs=(ospec, ospec),
        scratch_shapes=[pltpu.VMEM((1, LANE), F32)],
        compiler_params=_params(("parallel", "arbitrary")),
        name="forget_cumsum",
    )(f, b_forget_row)


def _fox_kernel(q_ref, k_ref, v_ref, aq_ref, ak_ref, o_ref, kaug, qaug, kstat, *, tq, tk):
    qi = pl.program_id(2)
    n_kb = k_ref.shape[0] // tk
    n_sub = tq // tk
    lane = lax.broadcasted_iota(jnp.int32, (1, LANE), 1)
    lower = lane < HEAD_DIM
    hr = lax.broadcasted_iota(jnp.int32, (LANE, LANE), 0) // HEAD_DIM
    hc = lax.broadcasted_iota(jnp.int32, (LANE, LANE), 1) // HEAD_DIM
    same_head = jnp.where(hr == hc, 1.0, 0.0).astype(BF16)

    def max_sq_norm(x):
        xf = x.astype(F32)
        return jnp.max(_dot((xf * xf).astype(BF16), same_head), axis=0, keepdims=True)

    def lane_sum(row_vec, lo, hi):
        return jnp.sum(jnp.where((lane >= lo) & (lane < hi), row_vec.astype(F32), 0.0), axis=1, keepdims=True)

    @pl.when(qi == 0)
    def _():
        k = k_ref[...]
        kaug[0] = jnp.where(lower, k, ak_ref[0])
        kaug[1] = jnp.where(lower, ak_ref[1], k)

        def stats(kb, carry):
            kn0, kn1, nc0, nc1 = carry
            off = pl.multiple_of(kb * tk, tk)
            n2 = max_sq_norm(k_ref[pl.ds(off, tk), :])
            tail = pl.multiple_of(off + tk - BF16_ROWS, BF16_ROWS)
            here = lane == kb
            kn0 = jnp.where(here, n2[:, 0:1], kn0)
            kn1 = jnp.where(here, n2[:, HEAD_DIM:HEAD_DIM + 1], kn1)
            last0 = ak_ref[0, pl.ds(tail, BF16_ROWS), :][BF16_ROWS - 1:BF16_ROWS, :]
            last1 = ak_ref[1, pl.ds(tail, BF16_ROWS), :][BF16_ROWS - 1:BF16_ROWS, :]
            nc0 = jnp.where(here, lane_sum(last0, 3, 6), nc0)
            nc1 = jnp.where(here, lane_sum(last1, 3, 6), nc1)
            return kn0, kn1, nc0, nc1

        zero = jnp.zeros((1, LANE), F32)
        kn0, kn1, nc0, nc1 = lax.fori_loop(0, n_kb, stats, (zero, zero, zero, zero))
        kstat[0:1, :] = kn0
        kstat[1:2, :] = kn1
        kstat[2:3, :] = nc0
        kstat[3:4, :] = nc1

    q = q_ref[...]
    qaug[0] = jnp.where(lower, q, aq_ref[0])
    qaug[1] = jnp.where(lower, aq_ref[1], q)
    row = lax.broadcasted_iota(jnp.int32, (tq, tk), 0)
    col = lax.broadcasted_iota(jnp.int32, (tq, tk), 1)

    qn = max_sq_norm(q)
    first = None
    for hf in range(2):
        qn2 = qn[:, hf * HEAD_DIM:hf * HEAD_DIM + 1]
        kn2 = kstat[hf:hf + 1, :]
        kn2_all = jnp.max(kn2, axis=1, keepdims=True)
        cum_q = lane_sum(aq_ref[hf, 0:1, :], 0, 3)
        upper = jnp.sqrt(qn2 * kn2) * NORM_SLACK + cum_q + kstat[2 + hf:3 + hf, :] + 1.0
        floor = -jnp.sqrt(qn2 * kn2_all) * NORM_SLACK
        needed = (upper >= floor - UNDERFLOW) & (lane < n_kb)
        first_h = jnp.min(jnp.where(needed, lane.astype(F32), float(n_kb)))
        first = first_h if first is None else jnp.minimum(first, first_h)
    kb_start = jnp.minimum(first.astype(jnp.int32), qi * n_sub)

    def step(kb, carry, mask_shift):
        off = pl.multiple_of(kb * tk, tk)
        v = v_ref[pl.ds(off, tk), :]
        new = []
        for hf in range(2):
            m, l, acc = carry[hf]
            s = _dot_nt(qaug[hf], kaug[hf, pl.ds(off, tk), :])
            if mask_shift is not None:
                s = jnp.where(col + mask_shift <= row, s, NEG_INF)
            m_new = jnp.maximum(m, jnp.max(s, axis=-1, keepdims=True))
            alpha = jnp.exp(m - m_new)
            pr = jnp.exp(s - m_new)
            l = alpha * l + jnp.sum(pr, axis=-1, keepdims=True)
            acc = alpha * acc + _dot(pr.astype(BF16), v)
            new.append((m_new, l, acc))
        return tuple(new)

    one = (jnp.full((tq, 1), NEG_INF, F32), jnp.zeros((tq, 1), F32), jnp.zeros((tq, LANE), F32))
    carry = lax.fori_loop(kb_start, qi * n_sub, functools.partial(step, mask_shift=None), (one, one))
    for d in range(n_sub):
        carry = step(qi * n_sub + d, carry, d * tk)
    (_, l0, acc0), (_, l1, acc1) = carry
    o_ref[...] = jnp.where(lower, acc0 / l0, acc1 / l1).astype(BF16)


def _fox(qkv, augq, augk, tq=512, tk=512):
    b, seq, _ = qkv.shape
    return pl.pallas_call(
        functools.partial(_fox_kernel, tq=tq, tk=tk),
        out_shape=jax.ShapeDtypeStruct((b, seq, WIDTH), BF16),
        grid=(b, N_PAIRS, seq // tq),
        in_specs=[pl.BlockSpec((None, tq, LANE), lambda bi, p, i: (bi, i, COL_QB + p)),
                  pl.BlockSpec((None, seq, LANE), lambda bi, p, i: (bi, 0, COL_KB + p)),
                  pl.BlockSpec((None, seq, LANE), lambda bi, p, i: (bi, 0, COL_VB + p)),
                  pl.BlockSpec((None, 2, tq, LANE), lambda bi, p, i: (bi, p, i, 0)),
                  pl.BlockSpec((None, 2, seq, LANE), lambda bi, p, i: (bi, p, 0, 0))],
        out_specs=pl.BlockSpec((None, tq, LANE), lambda bi, p, i: (bi, i, p)),
        scratch_shapes=[pltpu.VMEM((2, seq, LANE), BF16), pltpu.VMEM((2, tq, LANE), BF16),
                        pltpu.VMEM((8, LANE), F32)],
        compiler_params=_params(("parallel", "parallel", "arbitrary")),
        name="forgetting_attention",
    )(qkv, qkv, qkv, augq, augk)


def _merge_kernel(h_ref, g_ref, oa_ref, ob_ref, oc_ref, wg_ref, bg_ref, wa_ref, wb_ref, wc_ref, wo_ref, out_ref):
    h = h_ref[...]
    xn = _rms(h, g_ref[...]).astype(BF16)
    merged = jnp.zeros(h.shape, F32)
    for idx, (o_ref, w_ref) in enumerate(((oa_ref, wa_ref), (ob_ref, wb_ref), (oc_ref, wc_ref))):
        cols = slice(idx * D_MODEL, (idx + 1) * D_MODEL)
        gate = jax.nn.sigmoid(_dot(xn, wg_ref[:, cols]) + bg_ref[:, cols])
        merged = merged + gate * _dot(o_ref[...], w_ref[...])
    out_ref[...] = h + _dot(merged.astype(BF16), wo_ref[...])


def _merge(h, g, oa, ob, oc, w_gate, b_gate, w_a, w_b, w_c, w_out, tm=256):
    t_tokens = h.shape[0]
    full = lambda shape: pl.BlockSpec(shape, lambda i: (0, 0))
    ospec = pl.BlockSpec((tm, WIDTH), lambda i: (i, 0))
    return pl.pallas_call(
        _merge_kernel,
        out_shape=jax.ShapeDtypeStruct((t_tokens, D_MODEL), F32),
        grid=(t_tokens // tm,),
        in_specs=[pl.BlockSpec((tm, D_MODEL), lambda i: (i, 0)), full((1, D_MODEL)),
                  ospec, ospec, ospec,
                  full((D_MODEL, 3 * D_MODEL)), full((1, 3 * D_MODEL)),
                  full((WIDTH, D_MODEL)), full((WIDTH, D_MODEL)), full((WIDTH, D_MODEL)),
                  full((D_MODEL, D_MODEL))],
        out_specs=pl.BlockSpec((tm, D_MODEL), lambda i: (i, 0)),
        compiler_params=_params(("parallel",)),
        name="gated_merge",
    )(h, g, oa, ob, oc, w_gate, b_gate, w_a, w_b, w_c, w_out)


_CAND_COUNTS = tuple(PEER_TOPK // (i + 1) for i in range(PEER_TOPK))
_CAND_ROWS = 56
PIECE = 256


def _top16_rows(src, dst_ref):
    cur = src
    for r in range(PEER_TOPK):
        mx = jnp.max(cur, axis=0, keepdims=True)
        dst_ref[r:r + 1, :] = mx
        if r + 1 < PEER_TOPK:
            cur = jnp.where(cur == mx, NEG_INF, cur)


def _sort16_pairs():
    n, out, p = PEER_TOPK, [], 1
    while p < n:
        k = p
        while k >= 1:
            for j in range(k % p, n - k, 2 * k):
                for i in range(min(k, n - j - k)):
                    if (i + j) // (2 * p) == (i + j + k) // (2 * p):
                        out.append((i + j, i + j + k))
            k //= 2
        p *= 2
    return tuple(out)


_SORT16 = _sort16_pairs()
SUBLANES = 8


def _top16_of_keys(src, dst_ref):
    rows = [src[SUBLANES * k:SUBLANES * (k + 1), :] for k in range(N_KEYS // SUBLANES)]

    def exchange(i, j):
        rows[i], rows[j] = jnp.maximum(rows[i], rows[j]), jnp.minimum(rows[i], rows[j])

    for i, j in _SORT16:
        exchange(i, j)
    for shift in (4, 2, 1):
        other = [pltpu.roll(r, shift, 0) for r in rows]
        rows = [jnp.maximum(rows[k], other[PEER_TOPK - 1 - k]) for k in range(PEER_TOPK)]
        for d in (8, 4, 2, 1):
            for k in range(PEER_TOPK):
                if k & d == 0:
                    exchange(k, k + d)
    for r in range(PEER_TOPK):
        dst_ref[r:r + 1, :] = rows[r][0:1, :]


def _peer_kernel_plain(h_ref, g_ref, wq_ref, k1_ref, k2_ref, u_ref, vt_ref, out_ref, *scratch, ce, th):
    _peer_body(h_ref, g_ref, wq_ref, k1_ref, k2_ref, u_ref, vt_ref, None, out_ref, *scratch, ce=ce, th=th)


def _peer_kernel_final(h_ref, g_ref, wq_ref, k1_ref, k2_ref, u_ref, vt_ref, fg_ref, out_ref, *scratch, ce, th):
    _peer_body(h_ref, g_ref, wq_ref, k1_ref, k2_ref, u_ref, vt_ref, fg_ref, out_ref, *scratch, ce=ce, th=th)


def _peer_body(h_ref, g_ref, wq_ref, k1_ref, k2_ref, u_ref, vt_ref, fg_ref, out_ref,
               xn_s, s1_s, s2_s, tau_s, v1_s, v2_s, cand_s, top_s, y_s, w_s, *, ce, th):
    j = pl.program_id(1)
    nj = pl.num_programs(1)
    tm = h_ref.shape[0]
    a_per_step = ce // N_KEYS
    halves = [slice(t0, t0 + th) for t0 in range(0, tm, th)]

    def fill_candidates():
        off = 0
        for i, cnt in enumerate(_CAND_COUNTS):
            cand_s[off:off + cnt, :] = v1_s[i:i + 1, :] + v2_s[0:cnt, :]
            off += cnt

    @pl.when(j == 0)
    def _route():
        xn_s[...] = _rms(h_ref[...], g_ref[...]).astype(BF16)
        y_s[...] = jnp.zeros_like(y_s)
        cand_s[...] = jnp.full(cand_s.shape, NEG_INF, F32)
        for tok in halves:
            xn = xn_s[tok, :]
            for hd in range(PEER_HEADS):
                q1 = _dot(xn, wq_ref[:, (2 * hd) * N_KEYS:(2 * hd + 1) * N_KEYS]).astype(BF16)
                q2 = _dot(xn, wq_ref[:, (2 * hd + 1) * N_KEYS:(2 * hd + 2) * N_KEYS]).astype(BF16)
                s1 = _dot_nt(k1_ref[...], q1)
                s2 = _dot_nt(k2_ref[...], q2)
                _top16_of_keys(s1, v1_s)
                _top16_of_keys(s2, v2_s)
                fill_candidates()
                _top16_rows(cand_s[...], top_s)
                top = top_s[...]
                mx = top[0:1, :]
                shift = mx + jnp.log(jnp.sum(jnp.exp(top - mx), axis=0, keepdims=True))
                s1_s[hd, :, tok] = (s1 - shift) * LOG2E - 1.0
                s2_s[hd, :, tok] = s2 * LOG2E
                v1_s[...] = (v1_s[...] - shift) * LOG2E - 1.0
                v2_s[...] = v2_s[...] * LOG2E
                fill_candidates()
                _top16_rows(cand_s[...], top_s)
                tau_s[hd:hd + 1, tok] = top_s[PEER_TOPK - 1:PEER_TOPK, :]

    n_pieces = ce // PIECE

    def act_piece(tok, p):
        return _dot_nt(u_ref[p * PIECE:(p + 1) * PIECE, :], xn_s[tok, :])

    def mix_piece(tok, p, act):
        for aa in range(PIECE // N_KEYS):
            a = j * a_per_step + p * (PIECE // N_KEYS) + aa
            coef = jnp.zeros((N_KEYS, th), F32)
            for hd in range(PEER_HEADS):
                s = s2_s[hd, :, tok] + s1_s[hd, pl.ds(a, 1), tok]
                coef = coef + jnp.where(s >= tau_s[hd:hd + 1, tok], jnp.exp2(s), 0.0)
            x = act[aa * N_KEYS:(aa + 1) * N_KEYS, :]
            gelu2 = x * (1.0 + lax.erf(x * math.sqrt(0.5)))
            rows = slice(p * PIECE + aa * N_KEYS, p * PIECE + (aa + 1) * N_KEYS)
            w_s[rows, tok] = (coef * gelu2).astype(BF16)

    def out_piece(tok, p):
        return _dot(vt_ref[:, p * PIECE:(p + 1) * PIECE], w_s[p * PIECE:(p + 1) * PIECE, tok])

    acts = [act_piece(halves[0], p) for p in range(n_pieces)]
    for hi, tok in enumerate(halves):
        nxt = halves[hi + 1] if hi + 1 < len(halves) else None
        prv = halves[hi - 1] if hi > 0 else None
        nxt_acts, y_prev = [], None
        for p in range(n_pieces):
            mix_piece(tok, p, acts[p])
            if nxt is not None:
                nxt_acts.append(act_piece(nxt, p))
            if prv is not None:
                yp = out_piece(prv, p)
                y_prev = yp if y_prev is None else y_prev + yp
        if prv is not None:
            y_s[:, prv] += y_prev
        acts = nxt_acts
    y_last = out_piece(halves[-1], 0)
    for p in range(1, n_pieces):
        y_last = y_last + out_piece(halves[-1], p)
    y_s[:, halves[-1]] += y_last

    @pl.when(j == nj - 1)
    def _finish():
        y = h_ref[...] + y_s[...].T
        if fg_ref is not None:
            y = _rms(y, fg_ref[...])
        out_ref[...] = y


def _peer(h, g, wq, k1, k2, u, vt, final_g=None, tm=512, ce=2048, th=256):
    t_tokens = h.shape[0]
    tm = min(tm, t_tokens)
    full = lambda shape: pl.BlockSpec(shape, lambda i, j: (0, 0))
    in_specs = [pl.BlockSpec((tm, D_MODEL), lambda i, j: (i, 0)), full((1, D_MODEL)),
                full((D_MODEL, 2 * PEER_HEADS * N_KEYS)), full((N_KEYS, N_KEYS)), full((N_KEYS, N_KEYS)),
                pl.BlockSpec((ce, D_MODEL), lambda i, j: (j, 0)),
                pl.BlockSpec((D_MODEL, ce), lambda i, j: (0, j))]
    args = [h, g, wq, k1, k2, u, vt]
    kern = _peer_kernel_plain
    if final_g is not None:
        in_specs.append(full((1, D_MODEL)))
        args.append(final_g)
        kern = _peer_kernel_final
    head_buf = pltpu.VMEM((PEER_HEADS, N_KEYS, tm), F32)
    return pl.pallas_call(
        functools.partial(kern, ce=ce, th=th),
        out_shape=jax.ShapeDtypeStruct((t_tokens, D_MODEL), F32),
        grid=(t_tokens // tm, N_EXPERTS // ce),
        in_specs=in_specs,
        out_specs=pl.BlockSpec((tm, D_MODEL), lambda i, j: (i, 0)),
        scratch_shapes=[pltpu.VMEM((tm, D_MODEL), BF16), head_buf, head_buf,
                        pltpu.VMEM((PEER_HEADS, tm), F32),
                        pltpu.VMEM((PEER_TOPK, th), F32), pltpu.VMEM((PEER_TOPK, th), F32),
                        pltpu.VMEM((_CAND_ROWS, th), F32), pltpu.VMEM((PEER_TOPK, th), F32),
                        pltpu.VMEM((D_MODEL, tm), F32), pltpu.VMEM((ce, tm), BF16)],
        compiler_params=_params(("parallel", "arbitrary")),
        name="peer_ffn",
    )(*args)


def _transpose_cast_kernel(x_ref, o_ref):
    o_ref[...] = x_ref[...].T.astype(BF16)


def _transpose_cast(x, tr=512):
    rows, cols = x.shape
    return pl.pallas_call(
        _transpose_cast_kernel,
        out_shape=jax.ShapeDtypeStruct((cols, rows), BF16),
        grid=(rows // tr,),
        in_specs=[pl.BlockSpec((tr, cols), lambda i: (i, 0))],
        out_specs=pl.BlockSpec((cols, tr), lambda i: (0, i)),
        compiler_params=_params(("parallel",)),
        name="transpose_cast",
    )(x)


def _rope_tables(positions):
    inv_freq = ROPE_THETA ** (-jnp.arange(0, HEAD_DIM, 2, dtype=F32) / HEAD_DIM)
    ang = positions.astype(F32).reshape(-1, 1) * inv_freq
    cos, sin = jnp.cos(ang), jnp.sin(ang)
    return jnp.concatenate([cos] * 4, axis=-1), jnp.concatenate([-sin, sin, -sin, sin], axis=-1)


def _split_w_in(w_in):
    sizes = (WIDTH,) * 6 + (N_HEADS, WIDTH, C_KV_HEADS * HEAD_DIM, C_KV_HEADS * HEAD_DIM, 3 * D_MODEL)
    parts, start = [], 0
    for size in sizes:
        parts.append(w_in[:, start:start + size])
        start += size
    return parts


_C_HEAD_ORDER = (0, 3, 1, 4, 2, 5)


def _permute_heads(w, axis):
    shape = w.shape
    w = w.reshape(shape[:axis] + (N_HEADS, HEAD_DIM) + shape[axis + 1:])
    w = jnp.take(w, jnp.array(_C_HEAD_ORDER), axis=axis)
    return w.reshape(shape)


def _mixer(h, cos4, sin4, batch, norm_g, w_in, b_forget, sinks, b_gate, w_br_a, w_br_b, w_br_c, w_out):
    t_tokens = h.shape[0]
    seq = t_tokens // batch
    qa, ka, va, qb, kb, vb, fb, qc, kc, vc, wg = _split_w_in(w_in)
    w_f = jnp.pad(fb, ((0, 0), (0, LANE - N_HEADS)))
    w_att = jnp.concatenate([qa, ka, va, qb, kb, vb, _permute_heads(qc, 1), kc, vc, w_f], axis=1).astype(BF16)
    qkv, f, *qkv_classes = _in_proj(h, norm_g.reshape(1, -1), w_att, cos4, sin4, batch)
    qkv3 = qkv.reshape(batch, seq, -1)

    o1, l1 = _banded(qkv3, COL_QA, COL_KA, COL_VA, A_MAX_DIST, kv_shared=False)
    class_outs, class_lses = [], []
    for dil, x in zip(A_DILATIONS[1:], qkv_classes):
        o, lse = _banded(x.reshape(batch * dil, seq // dil, -1), COL_QA, COL_KA, COL_VA, A_MAX_DIST, kv_shared=False)
        class_outs.append(o.reshape(batch, dil, seq // dil, WIDTH))
        class_lses.append(lse.reshape(batch, dil, seq // dil, WIDTH))
    oa = _combine(o1.reshape(t_tokens, WIDTH), l1.reshape(t_tokens, WIDTH), class_outs, class_lses, batch)

    sink_tab = jnp.take(sinks.astype(F32), jnp.array(_C_HEAD_ORDER))
    (oc,) = _banded(qkv3, COL_QC, COL_KC, COL_VC, C_MAX_DIST, kv_shared=True, sink=sink_tab, want_lse=False)
    oc = oc.reshape(t_tokens, WIDTH)

    bf_row = jnp.pad(b_forget.astype(F32), (0, LANE - N_HEADS)).reshape(1, LANE)
    augq, augk = _cum_aug(f.reshape(batch, seq, LANE), bf_row)
    ob = _fox(qkv3, augq, augk).reshape(t_tokens, WIDTH)

    return _merge(h, norm_g.reshape(1, -1), oa, ob, oc, wg.astype(BF16), b_gate.reshape(1, -1).astype(F32),
                  w_br_a.astype(BF16), w_br_b.astype(BF16), _permute_heads(w_br_c, 0).astype(BF16),
                  w_out.astype(BF16))


def kernel(x, positions, norm1_g, w_in, b_forget, sinks, b_gate, w_br_a, w_br_b, w_br_c, w_out, norm2_g,
           peer_wq, peer_k1, peer_k2, peer_u, peer_v, final_g):
    batch, seq, _ = x.shape
    depth = w_in.shape[0]
    cos4, sin4 = _rope_tables(positions)
    h = x.reshape(batch * seq, D_MODEL)
    for l in range(depth):
        h = _mixer(h, cos4, sin4, batch, norm1_g[l], w_in[l], b_forget[l], sinks[l], b_gate[l],
                   w_br_a[l], w_br_b[l], w_br_c[l], w_out[l])
        h = _peer(h, norm2_g[l].reshape(1, -1), peer_wq[l].astype(BF16), peer_k1[l].astype(BF16),
                  peer_k2[l].astype(BF16), peer_u[l].astype(BF16), _transpose_cast(peer_v[l]),
                  final_g=final_g.reshape(1, -1) if l == depth - 1 else None)
    return h.reshape(batch, seq, D_MODEL)
```

```python
import functools
import math

import jax
import jax.numpy as jnp
from jax import lax
from jax.experimental import pallas as pl
from jax.experimental.pallas import tpu as pltpu

F32 = jnp.float32
BF16 = jnp.bfloat16

D_MODEL = 1024
HEAD_DIM = 64
N_HEADS = 6
N_PAIRS = N_HEADS // 2
C_KV_HEADS = 2
A_DILATIONS = (1, 4, 16)
A_MAX_DIST = 128
C_MAX_DIST = 127
BLOCK = 128
LANE = 128
ROPE_THETA = 10000.0
RMS_EPS = 1e-6
NEG_INF = -1e30
PEER_HEADS = 8
N_KEYS = 128
N_EXPERTS = N_KEYS * N_KEYS
PEER_TOPK = 16
WIDTH = N_HEADS * HEAD_DIM

COL_QA, COL_KA, COL_VA = 0, 3, 6
COL_QB, COL_KB, COL_VB = 9, 12, 15
COL_QC, COL_KC, COL_VC = 18, 21, 22
N_QKV_BLOCKS = 23
COL_F = 23
A_BLOCKS = 9
ROPE_BLOCKS = (0, 1, 2, 3, 4, 5, 18, 19, 20, 21)
QUERY_BLOCKS = (0, 1, 2, 9, 10, 11, 18, 19, 20)
QK_SCALE = HEAD_DIM ** -0.5
LOG2E = math.log2(math.e)

VMEM_LIMIT = 56 * 1024 * 1024


def _params(sem):
    return pltpu.CompilerParams(dimension_semantics=sem, vmem_limit_bytes=VMEM_LIMIT)


def _dot(a, b):
    return jnp.dot(a, b, preferred_element_type=F32)


def _dot_nt(a, b):
    return lax.dot_general(a, b, (((1,), (1,)), ((), ())), preferred_element_type=F32)


def _rms(x, g):
    var = jnp.mean(x * x, axis=-1, keepdims=True)
    return x * lax.rsqrt(var + RMS_EPS) * g


def _in_proj_kernel(h_ref, g_ref, w_ref, cos_ref, sin_ref, qkv_ref, f_ref, *rest):
    class_refs, proj_s, fin_s = rest[:-2], rest[-2], rest[-1]
    tm = h_ref.shape[0]
    xn = _rms(h_ref[...], g_ref[...]).astype(BF16)
    proj_s[...] = _dot(xn, w_ref[...])
    cos = cos_ref[...]
    sin = sin_ref[...]
    lane = lax.broadcasted_iota(jnp.int32, cos.shape, 1)
    first_half = (lane % HEAD_DIM) < (HEAD_DIM // 2)
    for c in range(N_QKV_BLOCKS):
        t = proj_s[:, c * LANE:(c + 1) * LANE]
        if c in ROPE_BLOCKS:
            rot = jnp.where(first_half, pltpu.roll(t, LANE - HEAD_DIM // 2, 1), pltpu.roll(t, HEAD_DIM // 2, 1))
            t = t * cos + rot * sin
        if c in QUERY_BLOCKS:
            t = t * QK_SCALE
        if c < A_BLOCKS:
            fin_s[c] = t
        qkv_ref[:, c * LANE:(c + 1) * LANE] = t.astype(BF16)
    f_ref[...] = proj_s[:, COL_F * LANE:(COL_F + 1) * LANE]
    for dil, ref in zip(A_DILATIONS[1:], class_refs):
        for c in range(dil):
            for blk in range(A_BLOCKS):
                rows = fin_s[blk, pl.ds(c, tm // dil, stride=dil), :]
                ref[c, :, blk * LANE:(blk + 1) * LANE] = rows.astype(BF16)


def _in_proj(h, g, w_att, cos4, sin4, batch, tm=512):
    t_tokens = h.shape[0]
    seq = t_tokens // batch
    tiles = seq // tm
    n_cols = w_att.shape[1]
    a_cols = A_BLOCKS * LANE
    class_shapes = tuple(jax.ShapeDtypeStruct((batch, dil, seq // dil, a_cols), BF16) for dil in A_DILATIONS[1:])
    class_specs = tuple(pl.BlockSpec((None, dil, tm // dil, a_cols), lambda i: (i // tiles, 0, i % tiles, 0))
                        for dil in A_DILATIONS[1:])
    return pl.pallas_call(
        _in_proj_kernel,
        out_shape=(jax.ShapeDtypeStruct((t_tokens, N_QKV_BLOCKS * LANE), BF16),
                   jax.ShapeDtypeStruct((t_tokens, LANE), F32)) + class_shapes,
        grid=(t_tokens // tm,),
        in_specs=[pl.BlockSpec((tm, D_MODEL), lambda i: (i, 0)),
                  pl.BlockSpec((1, D_MODEL), lambda i: (0, 0)),
                  pl.BlockSpec((D_MODEL, n_cols), lambda i: (0, 0)),
                  pl.BlockSpec((tm, LANE), lambda i: (i, 0)),
                  pl.BlockSpec((tm, LANE), lambda i: (i, 0))],
        out_specs=(pl.BlockSpec((tm, N_QKV_BLOCKS * LANE), lambda i: (i, 0)),
                   pl.BlockSpec((tm, LANE), lambda i: (i, 0))) + class_specs,
        scratch_shapes=[pltpu.VMEM((tm, n_cols), F32), pltpu.VMEM((A_BLOCKS, tm, LANE), F32)],
        compiler_params=_params(("parallel",)),
        name="in_proj",
    )(h, g, w_att, cos4, sin4)


def _banded_kernel(*refs, max_dist, qb, kv_shared, has_sink, want_lse):
    refs = list(refs)
    q_ref, kp_ref, kc_ref, vp_ref, vc_ref = refs[:5]
    pos = 5
    sink_ref = None
    if has_sink:
        sink_ref = refs[pos]
        pos += 1
    o_ref = refs[pos]
    pos += 1
    lse_ref = None
    if want_lse:
        lse_ref = refs[pos]
        pos += 1
    kwin, vwin = refs[pos], refs[pos + 1]

    i = pl.program_id(1)
    kwin[0:BLOCK, :] = kp_ref[...]
    kwin[BLOCK:, :] = kc_ref[...]
    vwin[0:BLOCK, :] = vp_ref[...]
    vwin[BLOCK:, :] = vc_ref[...]

    row = lax.broadcasted_iota(jnp.int32, (BLOCK, 2 * BLOCK), 0)
    col = lax.broadcasted_iota(jnp.int32, (BLOCK, 2 * BLOCK), 1)
    dist = row + BLOCK - col
    band = (dist >= 0) & (dist <= max_dist)
    lane = lax.broadcasted_iota(jnp.int32, (1, LANE), 1)
    lower = lane < HEAD_DIM

    def body(sb, carry):
        off = pl.multiple_of(sb * BLOCK, BLOCK)
        valid = band & ((col >= BLOCK) | (i * qb + sb > 0))
        for p in range(N_PAIRS):
            q = q_ref[pl.ds(off, BLOCK), p * LANE:(p + 1) * LANE]
            kv_cols = slice(0, LANE) if kv_shared else slice(p * LANE, (p + 1) * LANE)
            k = kwin[pl.ds(off, 2 * BLOCK), kv_cols]
            v = vwin[pl.ds(off, 2 * BLOCK), kv_cols]
            o_pair = jnp.zeros((BLOCK, LANE), F32)
            lse_pair = jnp.zeros((BLOCK, LANE), F32)
            for hf in range(2):
                sel = lower if hf == 0 else jnp.logical_not(lower)
                qm = jnp.where(sel, q, jnp.zeros_like(q))
                s = _dot_nt(qm, k)
                s = jnp.where(valid, s, NEG_INF)
                m = jnp.max(s, axis=-1, keepdims=True)
                if has_sink:
                    sk = sink_ref[2 * p + hf]
                    m = jnp.maximum(m, sk)
                pr = jnp.exp(s - m)
                den = jnp.sum(pr, axis=-1, keepdims=True)
                if has_sink:
                    den = den + jnp.exp(sk - m)
                o = _dot(pr.astype(BF16), v) / den
                o_pair = jnp.where(sel, o, o_pair)
                if want_lse:
                    lse_pair = jnp.where(sel, m + jnp.log(den), lse_pair)
            o_ref[pl.ds(off, BLOCK), p * LANE:(p + 1) * LANE] = o_pair.astype(BF16)
            if want_lse:
                lse_ref[pl.ds(off, BLOCK), p * LANE:(p + 1) * LANE] = lse_pair
        return carry

    lax.fori_loop(0, qb, body, 0, unroll=True)


def _banded(x, qcol, kcol, vcol, max_dist, kv_shared, sink=None, want_lse=True, qb=4):
    n, length, _ = x.shape
    qb = min(qb, length // BLOCK)
    tq = qb * BLOCK
    kvw = LANE if kv_shared else WIDTH
    kblk = kcol if kv_shared else kcol // N_PAIRS
    vblk = vcol if kv_shared else vcol // N_PAIRS
    in_specs = [
        pl.BlockSpec((None, tq, WIDTH), lambda b, i: (b, i, qcol // N_PAIRS)),
        pl.BlockSpec((None, BLOCK, kvw), lambda b, i: (b, jnp.maximum(i * qb - 1, 0), kblk)),
        pl.BlockSpec((None, tq, kvw), lambda b, i: (b, i, kblk)),
        pl.BlockSpec((None, BLOCK, kvw), lambda b, i: (b, jnp.maximum(i * qb - 1, 0), vblk)),
        pl.BlockSpec((None, tq, kvw), lambda b, i: (b, i, vblk)),
    ]
    args = [x, x, x, x, x]
    if sink is not None:
        in_specs.append(pl.BlockSpec(memory_space=pltpu.SMEM))
        args.append(sink)
    out_shape = [jax.ShapeDtypeStruct((n, length, WIDTH), BF16)]
    out_specs = [pl.BlockSpec((None, tq, WIDTH), lambda b, i: (b, i, 0))]
    if want_lse:
        out_shape.append(jax.ShapeDtypeStruct((n, length, WIDTH), F32))
        out_specs.append(pl.BlockSpec((None, tq, WIDTH), lambda b, i: (b, i, 0)))
    kern = functools.partial(_banded_kernel, max_dist=max_dist, qb=qb, kv_shared=kv_shared,
                             has_sink=sink is not None, want_lse=want_lse)
    return pl.pallas_call(
        kern,
        out_shape=tuple(out_shape),
        grid=(n, length // tq),
        in_specs=in_specs,
        out_specs=tuple(out_specs),
        scratch_shapes=[pltpu.VMEM((tq + BLOCK, kvw), BF16), pltpu.VMEM((tq + BLOCK, kvw), BF16)],
        compiler_params=_params(("parallel", "parallel")),
        name="banded_attention",
    )(*args)


def _combine_kernel(o1_ref, l1_ref, *rest):
    n_cls = len(A_DILATIONS) - 1
    o_refs, l_refs = rest[:n_cls], rest[n_cls:2 * n_cls]
    out_ref = rest[2 * n_cls]
    o_s, l_s = rest[2 * n_cls + 1:3 * n_cls + 1], rest[3 * n_cls + 1:]
    tm = o1_ref.shape[0]
    for dil, o_ref, l_ref, os_, ls_ in zip(A_DILATIONS[1:], o_refs, l_refs, o_s, l_s):
        for c in range(dil):
            for p in range(N_PAIRS):
                cols = slice(p * LANE, (p + 1) * LANE)
                os_[p, pl.ds(c, tm // dil, stride=dil), :] = o_ref[c, :, cols].astype(F32)
                ls_[p, pl.ds(c, tm // dil, stride=dil), :] = l_ref[c, :, cols]
    for p in range(N_PAIRS):
        cols = slice(p * LANE, (p + 1) * LANE)
        lses = [l1_ref[:, cols]] + [ls_[p] for ls_ in l_s]
        outs = [o1_ref[:, cols].astype(F32)] + [os_[p] for os_ in o_s]
        m = functools.reduce(jnp.maximum, lses)
        ws = [jnp.exp(l - m) for l in lses]
        num = functools.reduce(lambda a, b: a + b, [w * o for w, o in zip(ws, outs)])
        out_ref[:, cols] = (num / functools.reduce(lambda a, b: a + b, ws)).astype(BF16)


def _combine(o1, l1, class_outs, class_lses, batch, tm=1024):
    t_tokens = o1.shape[0]
    tiles = t_tokens // batch // tm
    tok = pl.BlockSpec((tm, WIDTH), lambda i: (i, 0))
    cls = [pl.BlockSpec((None, dil, tm // dil, WIDTH), lambda i: (i // tiles, 0, i % tiles, 0))
           for dil in A_DILATIONS[1:]]
    n_cls = len(cls)
    return pl.pallas_call(
        _combine_kernel,
        out_shape=jax.ShapeDtypeStruct((t_tokens, WIDTH), BF16),
        grid=(t_tokens // tm,),
        in_specs=[tok, tok] + cls + cls,
        out_specs=tok,
        scratch_shapes=[pltpu.VMEM((N_PAIRS, tm, LANE), F32)] * (2 * n_cls),
        compiler_params=_params(("parallel",)),
        name="combine_patterns",
    )(o1, l1, *class_outs, *class_lses)


CUM_BLOCK = 256
BF16_ROWS = 16
UNDERFLOW = 104.0
NORM_SLACK = 1.02


def _cum_kernel(f_ref, bf_ref, augq_ref, augk_ref, carry_s):
    seq = f_ref.shape[0]

    @pl.when(pl.program_id(1) == 0)
    def _():
        carry_s[...] = jnp.zeros_like(carry_s)

    r = lax.broadcasted_iota(jnp.int32, (LANE, LANE), 0)
    rr = lax.broadcasted_iota(jnp.int32, (CUM_BLOCK, CUM_BLOCK), 0)
    cc = lax.broadcasted_iota(jnp.int32, (CUM_BLOCK, CUM_BLOCK), 1)
    tril = jnp.where(rr >= cc, 1.0, 0.0).astype(F32)
    lane = lax.broadcasted_iota(jnp.int32, (CUM_BLOCK, LANE), 1) % HEAD_DIM
    bias = bf_ref[...]

    def body(blk, carry):
        off = pl.multiple_of(blk * CUM_BLOCK, CUM_BLOCK)
        x = f_ref[pl.ds(off, CUM_BLOCK), :] + bias
        log_f = jnp.minimum(x, 0.0) - jnp.log1p(jnp.exp(-jnp.abs(x)))
        cum_all = jnp.dot(tril, log_f, preferred_element_type=F32, precision=lax.Precision.HIGHEST) + carry
        for head in range(N_HEADS):
            sel = jnp.where(r == head, 1.0, 0.0).astype(F32)
            cum = jnp.dot(cum_all, sel, preferred_element_type=F32, precision=lax.Precision.HIGHEST)
            c1 = cum.astype(BF16).astype(F32)
            c2 = (cum - c1).astype(BF16).astype(F32)
            c3 = (cum - c1 - c2).astype(BF16).astype(F32)
            piece = jnp.where(lane % 3 == 0, c1, jnp.where(lane % 3 == 1, c2, c3))
            one = jnp.ones_like(cum)
            zero = jnp.zeros_like(cum)
            aq = jnp.where(lane < 3, piece, jnp.where(lane < 6, one, zero))
            ak = jnp.where(lane < 3, one, jnp.where(lane < 6, -piece, zero))
            augq_ref[head, pl.ds(off, CUM_BLOCK), :] = aq.astype(BF16)
            augk_ref[head, pl.ds(off, CUM_BLOCK), :] = ak.astype(BF16)
        return cum_all[CUM_BLOCK - 1:CUM_BLOCK, :]

    carry_s[...] = lax.fori_loop(0, seq // CUM_BLOCK, body, carry_s[...])


def _cum_aug(f, b_forget_row, ts=2048):
    b, seq, _ = f.shape
    ts = min(ts, seq)
    out = jax.ShapeDtypeStruct((b, N_HEADS, seq, LANE), BF16)
    ospec = pl.BlockSpec((None, N_HEADS, ts, LANE), lambda bi, si: (bi, 0, si, 0))
    return pl.pallas_call(
        _cum_kernel,
        out_shape=(out, out),
        grid=(b, seq // ts),
        in_specs=[pl.BlockSpec((None, ts, LANE), lambda bi, si: (bi, si, 0)),
                  pl.BlockSpec((1, LANE), lambda bi, si: (0, 0))],
        out_specs=(ospec, ospec),
        scratch_shapes=[pltpu.VMEM((1, LANE), F32)],
        compiler_params=_params(("parallel", "arbitrary")),
        name="forget_cumsum",
    )(f, b_forget_row)


def _fox_kernel(q_ref, k_ref, v_ref, aq_ref, ak_ref, o_ref, kaug, qaug, kstat, *, tq, tk):
    qi = pl.program_id(2)
    n_kb = k_ref.shape[0] // tk
    n_sub = tq // tk
    lane = lax.broadcasted_iota(jnp.int32, (1, LANE), 1)
    lower = lane < HEAD_DIM
    hr = lax.broadcasted_iota(jnp.int32, (LANE, LANE), 0) // HEAD_DIM
    hc = lax.broadcasted_iota(jnp.int32, (LANE, LANE), 1) // HEAD_DIM
    same_head = jnp.where(hr == hc, 1.0, 0.0).astype(BF16)

    def max_sq_norm(x):
        xf = x.astype(F32)
        return jnp.max(_dot((xf * xf).astype(BF16), same_head), axis=0, keepdims=True)

    def lane_sum(row_vec, lo, hi):
        return jnp.sum(jnp.where((lane >= lo) & (lane < hi), row_vec.astype(F32), 0.0), axis=1, keepdims=True)

    @pl.when(qi == 0)
    def _():
        k = k_ref[...]
        kaug[0] = jnp.where(lower, k, ak_ref[0])
        kaug[1] = jnp.where(lower, ak_ref[1], k)

        def stats(kb, carry):
            kn0, kn1, nc0, nc1 = carry
            off = pl.multiple_of(kb * tk, tk)
            n2 = max_sq_norm(k_ref[pl.ds(off, tk), :])
            tail = pl.multiple_of(off + tk - BF16_ROWS, BF16_ROWS)
            here = lane == kb
            kn0 = jnp.where(here, n2[:, 0:1], kn0)
            kn1 = jnp.where(here, n2[:, HEAD_DIM:HEAD_DIM + 1], kn1)
            last0 = ak_ref[0, pl.ds(tail, BF16_ROWS), :][BF16_ROWS - 1:BF16_ROWS, :]
            last1 = ak_ref[1, pl.ds(tail, BF16_ROWS), :][BF16_ROWS - 1:BF16_ROWS, :]
            nc0 = jnp.where(here, lane_sum(last0, 3, 6), nc0)
            nc1 = jnp.where(here, lane_sum(last1, 3, 6), nc1)
            return kn0, kn1, nc0, nc1

        zero = jnp.zeros((1, LANE), F32)
        kn0, kn1, nc0, nc1 = lax.fori_loop(0, n_kb, stats, (zero, zero, zero, zero))
        kstat[0:1, :] = kn0
        kstat[1:2, :] = kn1
        kstat[2:3, :] = nc0
        kstat[3:4, :] = nc1

    q = q_ref[...]
    qaug[0] = jnp.where(lower, q, aq_ref[0])
    qaug[1] = jnp.where(lower, aq_ref[1], q)
    row = lax.broadcasted_iota(jnp.int32, (tq, tk), 0)
    col = lax.broadcasted_iota(jnp.int32, (tq, tk), 1)

    qn = max_sq_norm(q)
    first = None
    for hf in range(2):
        qn2 = qn[:, hf * HEAD_DIM:hf * HEAD_DIM + 1]
        kn2 = kstat[hf:hf + 1, :]
        kn2_all = jnp.max(kn2, axis=1, keepdims=True)
        cum_q = lane_sum(aq_ref[hf, 0:1, :], 0, 3)
        upper = jnp.sqrt(qn2 * kn2) * NORM_SLACK + cum_q + kstat[2 + hf:3 + hf, :] + 1.0
        floor = -jnp.sqrt(qn2 * kn2_all) * NORM_SLACK
        needed = (upper >= floor - UNDERFLOW) & (lane < n_kb)
        first_h = jnp.min(jnp.where(needed, lane.astype(F32), float(n_kb)))
        first = first_h if first is None else jnp.minimum(first, first_h)
    kb_start = jnp.minimum(first.astype(jnp.int32), qi * n_sub)

    def step(kb, carry, mask_shift):
        off = pl.multiple_of(kb * tk, tk)
        v = v_ref[pl.ds(off, tk), :]
        new = []
        for hf in range(2):
            m, l, acc = carry[hf]
            s = _dot_nt(qaug[hf], kaug[hf, pl.ds(off, tk), :])
            if mask_shift is not None:
                s = jnp.where(col + mask_shift <= row, s, NEG_INF)
            m_new = jnp.maximum(m, jnp.max(s, axis=-1, keepdims=True))
            alpha = jnp.exp(m - m_new)
            pr = jnp.exp(s - m_new)
            l = alpha * l + jnp.sum(pr, axis=-1, keepdims=True)
            acc = alpha * acc + _dot(pr.astype(BF16), v)
            new.append((m_new, l, acc))
        return tuple(new)

    one = (jnp.full((tq, 1), NEG_INF, F32), jnp.zeros((tq, 1), F32), jnp.zeros((tq, LANE), F32))
    carry = lax.fori_loop(kb_start, qi * n_sub, functools.partial(step, mask_shift=None), (one, one))
    for d in range(n_sub):
        carry = step(qi * n_sub + d, carry, d * tk)
    (_, l0, acc0), (_, l1, acc1) = carry
    o_ref[...] = jnp.where(lower, acc0 / l0, acc1 / l1).astype(BF16)


def _fox(qkv, augq, augk, tq=512, tk=512):
    b, seq, _ = qkv.shape
    return pl.pallas_call(
        functools.partial(_fox_kernel, tq=tq, tk=tk),
        out_shape=jax.ShapeDtypeStruct((b, seq, WIDTH), BF16),
        grid=(b, N_PAIRS, seq // tq),
        in_specs=[pl.BlockSpec((None, tq, LANE), lambda bi, p, i: (bi, i, COL_QB + p)),
                  pl.BlockSpec((None, seq, LANE), lambda bi, p, i: (bi, 0, COL_KB + p)),
                  pl.BlockSpec((None, seq, LANE), lambda bi, p, i: (bi, 0, COL_VB + p)),
                  pl.BlockSpec((None, 2, tq, LANE), lambda bi, p, i: (bi, p, i, 0)),
                  pl.BlockSpec((None, 2, seq, LANE), lambda bi, p, i: (bi, p, 0, 0))],
        out_specs=pl.BlockSpec((None, tq, LANE), lambda bi, p, i: (bi, i, p)),
        scratch_shapes=[pltpu.VMEM((2, seq, LANE), BF16), pltpu.VMEM((2, tq, LANE), BF16),
                        pltpu.VMEM((8, LANE), F32)],
        compiler_params=_params(("parallel", "parallel", "arbitrary")),
        name="forgetting_attention",
    )(qkv, qkv, qkv, augq, augk)


def _merge_kernel(h_ref, g_ref, oa_ref, ob_ref, oc_ref, wg_ref, bg_ref, wa_ref, wb_ref, wc_ref, wo_ref, out_ref):
    h = h_ref[...]
    xn = _rms(h, g_ref[...]).astype(BF16)
    merged = jnp.zeros(h.shape, F32)
    for idx, (o_ref, w_ref) in enumerate(((oa_ref, wa_ref), (ob_ref, wb_ref), (oc_ref, wc_ref))):
        cols = slice(idx * D_MODEL, (idx + 1) * D_MODEL)
        gate = jax.nn.sigmoid(_dot(xn, wg_ref[:, cols]) + bg_ref[:, cols])
        merged = merged + gate * _dot(o_ref[...], w_ref[...])
    out_ref[...] = h + _dot(merged.astype(BF16), wo_ref[...])


def _merge(h, g, oa, ob, oc, w_gate, b_gate, w_a, w_b, w_c, w_out, tm=256):
    t_tokens = h.shape[0]
    full = lambda shape: pl.BlockSpec(shape, lambda i: (0, 0))
    ospec = pl.BlockSpec((tm, WIDTH), lambda i: (i, 0))
    return pl.pallas_call(
        _merge_kernel,
        out_shape=jax.ShapeDtypeStruct((t_tokens, D_MODEL), F32),
        grid=(t_tokens // tm,),
        in_specs=[pl.BlockSpec((tm, D_MODEL), lambda i: (i, 0)), full((1, D_MODEL)),
                  ospec, ospec, ospec,
                  full((D_MODEL, 3 * D_MODEL)), full((1, 3 * D_MODEL)),
                  full((WIDTH, D_MODEL)), full((WIDTH, D_MODEL)), full((WIDTH, D_MODEL)),
                  full((D_MODEL, D_MODEL))],
        out_specs=pl.BlockSpec((tm, D_MODEL), lambda i: (i, 0)),
        compiler_params=_params(("parallel",)),
        name="gated_merge",
    )(h, g, oa, ob, oc, w_gate, b_gate, w_a, w_b, w_c, w_out)


_CAND_COUNTS = tuple(PEER_TOPK // (i + 1) for i in range(PEER_TOPK))
_CAND_ROWS = 56
PIECE = 256


def _top16_rows(src, dst_ref):
    cur = src
    for r in range(PEER_TOPK):
        mx = jnp.max(cur, axis=0, keepdims=True)
        dst_ref[r:r + 1, :] = mx
        if r + 1 < PEER_TOPK:
            cur = jnp.where(cur == mx, NEG_INF, cur)


def _sort16_pairs():
    n, out, p = PEER_TOPK, [], 1
    while p < n:
        k = p
        while k >= 1:
            for j in range(k % p, n - k, 2 * k):
                for i in range(min(k, n - j - k)):
                    if (i + j) // (2 * p) == (i + j + k) // (2 * p):
                        out.append((i + j, i + j + k))
            k //= 2
        p *= 2
    return tuple(out)


_SORT16 = _sort16_pairs()
SUBLANES = 8


def _top16_of_keys(src, dst_ref):
    rows = [src[SUBLANES * k:SUBLANES * (k + 1), :] for k in range(N_KEYS // SUBLANES)]

    def exchange(i, j):
        rows[i], rows[j] = jnp.maximum(rows[i], rows[j]), jnp.minimum(rows[i], rows[j])

    for i, j in _SORT16:
        exchange(i, j)
    for shift in (4, 2, 1):
        other = [pltpu.roll(r, shift, 0) for r in rows]
        rows = [jnp.maximum(rows[k], other[PEER_TOPK - 1 - k]) for k in range(PEER_TOPK)]
        for d in (8, 4, 2, 1):
            for k in range(PEER_TOPK):
                if k & d == 0:
                    exchange(k, k + d)
    for r in range(PEER_TOPK):
        dst_ref[r:r + 1, :] = rows[r][0:1, :]


def _peer_kernel_plain(h_ref, g_ref, wq_ref, k1_ref, k2_ref, u_ref, vt_ref, out_ref, *scratch, ce, th):
    _peer_body(h_ref, g_ref, wq_ref, k1_ref, k2_ref, u_ref, vt_ref, None, out_ref, *scratch, ce=ce, th=th)


def _peer_kernel_final(h_ref, g_ref, wq_ref, k1_ref, k2_ref, u_ref, vt_ref, fg_ref, out_ref, *scratch, ce, th):
    _peer_body(h_ref, g_ref, wq_ref, k1_ref, k2_ref, u_ref, vt_ref, fg_ref, out_ref, *scratch, ce=ce, th=th)


def _peer_body(h_ref, g_ref, wq_ref, k1_ref, k2_ref, u_ref, vt_ref, fg_ref, out_ref,
               xn_s, s1_s, s2_s, tau_s, v1_s, v2_s, cand_s, top_s, y_s, w_s, *, ce, th):
    j = pl.program_id(1)
    nj = pl.num_programs(1)
    tm = h_ref.shape[0]
    a_per_step = ce // N_KEYS
    halves = [slice(t0, t0 + th) for t0 in range(0, tm, th)]

    def fill_candidates():
        off = 0
        for i, cnt in enumerate(_CAND_COUNTS):
            cand_s[off:off + cnt, :] = v1_s[i:i + 1, :] + v2_s[0:cnt, :]
            off += cnt

    @pl.when(j == 0)
    def _route():
        xn_s[...] = _rms(h_ref[...], g_ref[...]).astype(BF16)
        y_s[...] = jnp.zeros_like(y_s)
        cand_s[...] = jnp.full(cand_s.shape, NEG_INF, F32)
        for tok in halves:
            xn = xn_s[tok, :]
            for hd in range(PEER_HEADS):
                q1 = _dot(xn, wq_ref[:, (2 * hd) * N_KEYS:(2 * hd + 1) * N_KEYS]).astype(BF16)
                q2 = _dot(xn, wq_ref[:, (2 * hd + 1) * N_KEYS:(2 * hd + 2) * N_KEYS]).astype(BF16)
                s1 = _dot_nt(k1_ref[...], q1)
                s2 = _dot_nt(k2_ref[...], q2)
                _top16_of_keys(s1, v1_s)
                _top16_of_keys(s2, v2_s)
                fill_candidates()
                _top16_rows(cand_s[...], top_s)
                top = top_s[...]
                mx = top[0:1, :]
                shift = mx + jnp.log(jnp.sum(jnp.exp(top - mx), axis=0, keepdims=True))
                s1_s[hd, :, tok] = (s1 - shift) * LOG2E - 1.0
                s2_s[hd, :, tok] = s2 * LOG2E
                v1_s[...] = (v1_s[...] - shift) * LOG2E - 1.0
                v2_s[...] = v2_s[...] * LOG2E
                fill_candidates()
                _top16_rows(cand_s[...], top_s)
                tau_s[hd:hd + 1, tok] = top_s[PEER_TOPK - 1:PEER_TOPK, :]

    n_pieces = ce // PIECE

    def act_piece(tok, p):
        return _dot_nt(u_ref[p * PIECE:(p + 1) * PIECE, :], xn_s[tok, :])

    def mix_piece(tok, p, act):
        for aa in range(PIECE // N_KEYS):
            a = j * a_per_step + p * (PIECE // N_KEYS) + aa
            coef = jnp.zeros((N_KEYS, th), F32)
            for hd in range(PEER_HEADS):
                s = s2_s[hd, :, tok] + s1_s[hd, pl.ds(a, 1), tok]
                coef = coef + jnp.where(s >= tau_s[hd:hd + 1, tok], jnp.exp2(s), 0.0)
            x = act[aa * N_KEYS:(aa + 1) * N_KEYS, :]
            gelu2 = x * (1.0 + lax.erf(x * math.sqrt(0.5)))
            rows = slice(p * PIECE + aa * N_KEYS, p * PIECE + (aa + 1) * N_KEYS)
            w_s[rows, tok] = (coef * gelu2).astype(BF16)

    def out_piece(tok, p):
        return _dot(vt_ref[:, p * PIECE:(p + 1) * PIECE], w_s[p * PIECE:(p + 1) * PIECE, tok])

    acts = [act_piece(halves[0], p) for p in range(n_pieces)]
    for hi, tok in enumerate(halves):
        nxt = halves[hi + 1] if hi + 1 < len(halves) else None
        prv = halves[hi - 1] if hi > 0 else None
        nxt_acts, y_prev = [], None
        for p in range(n_pieces):
            mix_piece(tok, p, acts[p])
            if nxt is not None:
                nxt_acts.append(act_piece(nxt, p))
            if prv is not None:
                yp = out_piece(prv, p)
                y_prev = yp if y_prev is None else y_prev + yp
        if prv is not None:
            y_s[:, prv] += y_prev
        acts = nxt_acts
    y_last = out_piece(halves[-1], 0)
    for p in range(1, n_pieces):
        y_last = y_last + out_piece(halves[-1], p)
    y_s[:, halves[-1]] += y_last

    @pl.when(j == nj - 1)
    def _finish():
        y = h_ref[...] + y_s[...].T
        if fg_ref is not None:
            y = _rms(y, fg_ref[...])
        out_ref[...] = y


def _peer(h, g, wq, k1, k2, u, vt, final_g=None, tm=512, ce=2048, th=256):
    t_tokens = h.shape[0]
    tm = min(tm, t_tokens)
    full = lambda shape: pl.BlockSpec(shape, lambda i, j: (0, 0))
    in_specs = [pl.BlockSpec((tm, D_MODEL), lambda i, j: (i, 0)), full((1, D_MODEL)),
                full((D_MODEL, 2 * PEER_HEADS * N_KEYS)), full((N_KEYS, N_KEYS)), full((N_KEYS, N_KEYS)),
                pl.BlockSpec((ce, D_MODEL), lambda i, j: (j, 0)),
                pl.BlockSpec((D_MODEL, ce), lambda i, j: (0, j))]
    args = [h, g, wq, k1, k2, u, vt]
    kern = _peer_kernel_plain
    if final_g is not None:
        in_specs.append(full((1, D_MODEL)))
        args.append(final_g)
        kern = _peer_kernel_final
    head_buf = pltpu.VMEM((PEER_HEADS, N_KEYS, tm), F32)
    return pl.pallas_call(
        functools.partial(kern, ce=ce, th=th),
        out_shape=jax.ShapeDtypeStruct((t_tokens, D_MODEL), F32),
        grid=(t_tokens // tm, N_EXPERTS // ce),
        in_specs=in_specs,
        out_specs=pl.BlockSpec((tm, D_MODEL), lambda i, j: (i, 0)),
        scratch_shapes=[pltpu.VMEM((tm, D_MODEL), BF16), head_buf, head_buf,
                        pltpu.VMEM((PEER_HEADS, tm), F32),
                        pltpu.VMEM((PEER_TOPK, th), F32), pltpu.VMEM((PEER_TOPK, th), F32),
                        pltpu.VMEM((_CAND_ROWS, th), F32), pltpu.VMEM((PEER_TOPK, th), F32),
                        pltpu.VMEM((D_MODEL, tm), F32), pltpu.VMEM((ce, tm), BF16)],
        compiler_params=_params(("parallel", "arbitrary")),
        name="peer_ffn",
    )(*args)


def _transpose_cast_kernel(x_ref, o_ref):
    o_ref[...] = x_ref[...].T.astype(BF16)


def _transpose_cast(x, layer, tr=512):
    _, rows, cols = x.shape
    return pl.pallas_call(
        _transpose_cast_kernel,
        out_shape=jax.ShapeDtypeStruct((cols, rows), BF16),
        grid=(rows // tr,),
        in_specs=[pl.BlockSpec((None, tr, cols), lambda i: (layer, i, 0))],
        out_specs=pl.BlockSpec((cols, tr), lambda i: (0, i)),
        compiler_params=_params(("parallel",)),
        name="transpose_cast",
    )(x)


def _cast_kernel(x_ref, o_ref):
    o_ref[...] = x_ref[...].astype(BF16)


def _cast_layer(x, layer, tr=1024):
    _, rows, cols = x.shape
    tr = min(tr, rows)
    return pl.pallas_call(
        _cast_kernel,
        out_shape=jax.ShapeDtypeStruct((rows, cols), BF16),
        grid=(rows // tr,),
        in_specs=[pl.BlockSpec((None, tr, cols), lambda i: (layer, i, 0))],
        out_specs=pl.BlockSpec((tr, cols), lambda i: (i, 0)),
        compiler_params=_params(("parallel",)),
        name="cast_layer",
    )(x)


_IN_SIZES = (WIDTH,) * 6 + (N_HEADS, WIDTH, C_KV_HEADS * HEAD_DIM, C_KV_HEADS * HEAD_DIM, 3 * D_MODEL)
_IN_STARTS = tuple(sum(_IN_SIZES[:k]) for k in range(len(_IN_SIZES)))
_C_HEAD_ORDER = (0, 3, 1, 4, 2, 5)


def _w_in_kernel(w_ref, watt_ref, wgate_ref):
    def piece(idx, offset=0, size=None):
        start = _IN_STARTS[idx] + offset
        return w_ref[:, start:start + (_IN_SIZES[idx] if size is None else size)].astype(BF16)

    col = 0
    for idx in (0, 1, 2, 3, 4, 5):
        watt_ref[:, col:col + WIDTH] = piece(idx)
        col += WIDTH
    for hd in _C_HEAD_ORDER:
        watt_ref[:, col:col + HEAD_DIM] = piece(7, hd * HEAD_DIM, HEAD_DIM)
        col += HEAD_DIM
    for idx in (8, 9):
        watt_ref[:, col:col + LANE] = piece(idx)
        col += LANE
    forget = piece(6)
    watt_ref[:, col:col + LANE] = jnp.concatenate(
        [forget, jnp.zeros((forget.shape[0], LANE - N_HEADS), BF16)], axis=1)
    wgate_ref[...] = piece(10)


def _w_in_layout(w_in, layer, tr=256):
    _, rows, cols = w_in.shape
    att_cols = (N_QKV_BLOCKS + 1) * LANE
    return pl.pallas_call(
        _w_in_kernel,
        out_shape=(jax.ShapeDtypeStruct((rows, att_cols), BF16), jax.ShapeDtypeStruct((rows, 3 * D_MODEL), BF16)),
        grid=(rows // tr,),
        in_specs=[pl.BlockSpec((None, tr, cols), lambda i: (layer, i, 0))],
        out_specs=(pl.BlockSpec((tr, att_cols), lambda i: (i, 0)), pl.BlockSpec((tr, 3 * D_MODEL), lambda i: (i, 0))),
        compiler_params=_params(("parallel",)),
        name="w_in_layout",
    )(w_in)


def _rope_tables(positions):
    inv_freq = ROPE_THETA ** (-jnp.arange(0, HEAD_DIM, 2, dtype=F32) / HEAD_DIM)
    ang = positions.astype(F32).reshape(-1, 1) * inv_freq
    cos, sin = jnp.cos(ang), jnp.sin(ang)
    return jnp.concatenate([cos] * 4, axis=-1), jnp.concatenate([-sin, sin, -sin, sin], axis=-1)


def _permute_heads(w, axis):
    heads = [lax.slice_in_dim(w, hd * HEAD_DIM, (hd + 1) * HEAD_DIM, axis=axis) for hd in _C_HEAD_ORDER]
    return jnp.concatenate(heads, axis=axis)


def _mixer(h, cos4, sin4, batch, norm_g, w_att, wg, b_forget, sinks, b_gate, w_br_a, w_br_b, w_br_c, w_out):
    t_tokens = h.shape[0]
    seq = t_tokens // batch
    qkv, f, *qkv_classes = _in_proj(h, norm_g.reshape(1, -1), w_att, cos4, sin4, batch)
    qkv3 = qkv.reshape(batch, seq, -1)

    o1, l1 = _banded(qkv3, COL_QA, COL_KA, COL_VA, A_MAX_DIST, kv_shared=False)
    class_outs, class_lses = [], []
    for dil, x in zip(A_DILATIONS[1:], qkv_classes):
        o, lse = _banded(x.reshape(batch * dil, seq // dil, -1), COL_QA, COL_KA, COL_VA, A_MAX_DIST, kv_shared=False)
        class_outs.append(o.reshape(batch, dil, seq // dil, WIDTH))
        class_lses.append(lse.reshape(batch, dil, seq // dil, WIDTH))
    oa = _combine(o1.reshape(t_tokens, WIDTH), l1.reshape(t_tokens, WIDTH), class_outs, class_lses, batch)

    sink_tab = jnp.stack([sinks[hd] for hd in _C_HEAD_ORDER]).astype(F32)
    (oc,) = _banded(qkv3, COL_QC, COL_KC, COL_VC, C_MAX_DIST, kv_shared=True, sink=sink_tab, want_lse=False)
    oc = oc.reshape(t_tokens, WIDTH)

    bf_row = jnp.pad(b_forget.astype(F32), (0, LANE - N_HEADS)).reshape(1, LANE)
    augq, augk = _cum_aug(f.reshape(batch, seq, LANE), bf_row)
    ob = _fox(qkv3, augq, augk).reshape(t_tokens, WIDTH)

    return _merge(h, norm_g.reshape(1, -1), oa, ob, oc, wg, b_gate.reshape(1, -1).astype(F32),
                  w_br_a.astype(BF16), w_br_b.astype(BF16), _permute_heads(w_br_c, 0).astype(BF16),
                  w_out.astype(BF16))


def kernel(x, positions, norm1_g, w_in, b_forget, sinks, b_gate, w_br_a, w_br_b, w_br_c, w_out, norm2_g,
           peer_wq, peer_k1, peer_k2, peer_u, peer_v, final_g):
    batch, seq, _ = x.shape
    depth = w_in.shape[0]
    cos4, sin4 = _rope_tables(positions)
    h = x.reshape(batch * seq, D_MODEL)
    for l in range(depth):
        w_att, w_gate = _w_in_layout(w_in, l)
        h = _mixer(h, cos4, sin4, batch, norm1_g[l], w_att, w_gate, b_forget[l], sinks[l], b_gate[l],
                   w_br_a[l], w_br_b[l], w_br_c[l], w_out[l])
        h = _peer(h, norm2_g[l].reshape(1, -1), _cast_layer(peer_wq, l), peer_k1[l].astype(BF16),
                  peer_k2[l].astype(BF16), _cast_layer(peer_u, l), _transpose_cast(peer_v, l),
                  final_g=final_g.reshape(1, -1) if l == depth - 1 else None)
    return h.reshape(batch, seq, D_MODEL)
```

```python
import functools
import math

import jax
import jax.numpy as jnp
from jax import lax
from jax.experimental import pallas as pl
from jax.experimental.pallas import tpu as pltpu

F32 = jnp.float32
BF16 = jnp.bfloat16

D_MODEL = 1024
HEAD_DIM = 64
N_HEADS = 6
N_PAIRS = N_HEADS // 2
C_KV_HEADS = 2
A_DILATIONS = (1, 4, 16)
A_MAX_DIST = 128
C_MAX_DIST = 127
BLOCK = 128
LANE = 128
ROPE_THETA = 10000.0
RMS_EPS = 1e-6
NEG_INF = -1e30
PEER_HEADS = 8
N_KEYS = 128
N_EXPERTS = N_KEYS * N_KEYS
PEER_TOPK = 16
WIDTH = N_HEADS * HEAD_DIM

COL_QA, COL_KA, COL_VA = 0, 3, 6
COL_QB, COL_KB, COL_VB = 9, 12, 15
COL_QC, COL_KC, COL_VC = 18, 21, 22
N_QKV_BLOCKS = 23
COL_F = 23
A_BLOCKS = 9
ROPE_BLOCKS = (0, 1, 2, 3, 4, 5, 18, 19, 20, 21)
QUERY_BLOCKS = (0, 1, 2, 9, 10, 11, 18, 19, 20)
QK_SCALE = HEAD_DIM ** -0.5
LOG2E = math.log2(math.e)

VMEM_LIMIT = 56 * 1024 * 1024


def _params(sem):
    return pltpu.CompilerParams(dimension_semantics=sem, vmem_limit_bytes=VMEM_LIMIT)


def _dot(a, b):
    return jnp.dot(a, b, preferred_element_type=F32)


def _dot_nt(a, b):
    return lax.dot_general(a, b, (((1,), (1,)), ((), ())), preferred_element_type=F32)


def _rms(x, g):
    var = jnp.mean(x * x, axis=-1, keepdims=True)
    return x * lax.rsqrt(var + RMS_EPS) * g


def _in_proj_kernel(h_ref, g_ref, w_ref, cos_ref, sin_ref, qkv_ref, f_ref, *rest):
    class_refs, proj_s, fin_s = rest[:-2], rest[-2], rest[-1]
    tm = h_ref.shape[0]
    xn = _rms(h_ref[...], g_ref[...]).astype(BF16)
    proj_s[...] = _dot(xn, w_ref[...])
    cos = cos_ref[...]
    sin = sin_ref[...]
    lane = lax.broadcasted_iota(jnp.int32, cos.shape, 1)
    first_half = (lane % HEAD_DIM) < (HEAD_DIM // 2)
    for c in range(N_QKV_BLOCKS):
        t = proj_s[:, c * LANE:(c + 1) * LANE]
        if c in ROPE_BLOCKS:
            rot = jnp.where(first_half, pltpu.roll(t, LANE - HEAD_DIM // 2, 1), pltpu.roll(t, HEAD_DIM // 2, 1))
            t = t * cos + rot * sin
        if c in QUERY_BLOCKS:
            t = t * QK_SCALE
        if c < A_BLOCKS:
            fin_s[c] = t
        qkv_ref[:, c * LANE:(c + 1) * LANE] = t.astype(BF16)
    f_ref[...] = proj_s[:, COL_F * LANE:(COL_F + 1) * LANE]
    for dil, ref in zip(A_DILATIONS[1:], class_refs):
        for c in range(dil):
            for blk in range(A_BLOCKS):
                rows = fin_s[blk, pl.ds(c, tm // dil, stride=dil), :]
                ref[c, :, blk * LANE:(blk + 1) * LANE] = rows.astype(BF16)


def _in_proj(h, g, w_att, cos4, sin4, batch, tm=512):
    t_tokens = h.shape[0]
    seq = t_tokens // batch
    tiles = seq // tm
    n_cols = w_att.shape[1]
    a_cols = A_BLOCKS * LANE
    class_shapes = tuple(jax.ShapeDtypeStruct((batch, dil, seq // dil, a_cols), BF16) for dil in A_DILATIONS[1:])
    class_specs = tuple(pl.BlockSpec((None, dil, tm // dil, a_cols), lambda i: (i // tiles, 0, i % tiles, 0))
                        for dil in A_DILATIONS[1:])
    return pl.pallas_call(
        _in_proj_kernel,
        out_shape=(jax.ShapeDtypeStruct((t_tokens, N_QKV_BLOCKS * LANE), BF16),
                   jax.ShapeDtypeStruct((t_tokens, LANE), F32)) + class_shapes,
        grid=(t_tokens // tm,),
        in_specs=[pl.BlockSpec((tm, D_MODEL), lambda i: (i, 0)),
                  pl.BlockSpec((1, D_MODEL), lambda i: (0, 0)),
                  pl.BlockSpec((D_MODEL, n_cols), lambda i: (0, 0)),
                  pl.BlockSpec((tm, LANE), lambda i: (i, 0)),
                  pl.BlockSpec((tm, LANE), lambda i: (i, 0))],
        out_specs=(pl.BlockSpec((tm, N_QKV_BLOCKS * LANE), lambda i: (i, 0)),
                   pl.BlockSpec((tm, LANE), lambda i: (i, 0))) + class_specs,
        scratch_shapes=[pltpu.VMEM((tm, n_cols), F32), pltpu.VMEM((A_BLOCKS, tm, LANE), F32)],
        compiler_params=_params(("parallel",)),
        name="in_proj",
    )(h, g, w_att, cos4, sin4)


def _banded_kernel(*refs, max_dist, qb, kv_shared, has_sink, want_lse):
    refs = list(refs)
    q_ref, kp_ref, kc_ref, vp_ref, vc_ref = refs[:5]
    pos = 5
    sink_ref = None
    if has_sink:
        sink_ref = refs[pos]
        pos += 1
    o_ref = refs[pos]
    pos += 1
    lse_ref = None
    if want_lse:
        lse_ref = refs[pos]
        pos += 1
    kwin, vwin = refs[pos], refs[pos + 1]

    i = pl.program_id(1)
    kwin[0:BLOCK, :] = kp_ref[...]
    kwin[BLOCK:, :] = kc_ref[...]
    vwin[0:BLOCK, :] = vp_ref[...]
    vwin[BLOCK:, :] = vc_ref[...]

    row = lax.broadcasted_iota(jnp.int32, (BLOCK, 2 * BLOCK), 0)
    col = lax.broadcasted_iota(jnp.int32, (BLOCK, 2 * BLOCK), 1)
    dist = row + BLOCK - col
    band = (dist >= 0) & (dist <= max_dist)
    lane = lax.broadcasted_iota(jnp.int32, (1, LANE), 1)
    lower = lane < HEAD_DIM

    def body(sb, carry):
        off = pl.multiple_of(sb * BLOCK, BLOCK)
        valid = band & ((col >= BLOCK) | (i * qb + sb > 0))
        for p in range(N_PAIRS):
            q = q_ref[pl.ds(off, BLOCK), p * LANE:(p + 1) * LANE]
            kv_cols = slice(0, LANE) if kv_shared else slice(p * LANE, (p + 1) * LANE)
            k = kwin[pl.ds(off, 2 * BLOCK), kv_cols]
            v = vwin[pl.ds(off, 2 * BLOCK), kv_cols]
            o_pair = jnp.zeros((BLOCK, LANE), F32)
            lse_pair = jnp.zeros((BLOCK, LANE), F32)
            for hf in range(2):
                sel = lower if hf == 0 else jnp.logical_not(lower)
                qm = jnp.where(sel, q, jnp.zeros_like(q))
                s = _dot_nt(qm, k)
                s = jnp.where(valid, s, NEG_INF)
                m = jnp.max(s, axis=-1, keepdims=True)
                if has_sink:
                    sk = sink_ref[2 * p + hf]
                    m = jnp.maximum(m, sk)
                pr = jnp.exp(s - m)
                den = jnp.sum(pr, axis=-1, keepdims=True)
                if has_sink:
                    den = den + jnp.exp(sk - m)
                o = _dot(pr.astype(BF16), v) / den
                o_pair = jnp.where(sel, o, o_pair)
                if want_lse:
                    lse_pair = jnp.where(sel, m + jnp.log(den), lse_pair)
            o_ref[pl.ds(off, BLOCK), p * LANE:(p + 1) * LANE] = o_pair.astype(BF16)
            if want_lse:
                lse_ref[pl.ds(off, BLOCK), p * LANE:(p + 1) * LANE] = lse_pair
        return carry

    lax.fori_loop(0, qb, body, 0, unroll=True)


def _banded(x, qcol, kcol, vcol, max_dist, kv_shared, sink=None, want_lse=True, qb=4):
    n, length, _ = x.shape
    qb = min(qb, length // BLOCK)
    tq = qb * BLOCK
    kvw = LANE if kv_shared else WIDTH
    kblk = kcol if kv_shared else kcol // N_PAIRS
    vblk = vcol if kv_shared else vcol // N_PAIRS
    in_specs = [
        pl.BlockSpec((None, tq, WIDTH), lambda b, i: (b, i, qcol // N_PAIRS)),
        pl.BlockSpec((None, BLOCK, kvw), lambda b, i: (b, jnp.maximum(i * qb - 1, 0), kblk)),
        pl.BlockSpec((None, tq, kvw), lambda b, i: (b, i, kblk)),
        pl.BlockSpec((None, BLOCK, kvw), lambda b, i: (b, jnp.maximum(i * qb - 1, 0), vblk)),
        pl.BlockSpec((None, tq, kvw), lambda b, i: (b, i, vblk)),
    ]
    args = [x, x, x, x, x]
    if sink is not None:
        in_specs.append(pl.BlockSpec(memory_space=pltpu.SMEM))
        args.append(sink)
    out_shape = [jax.ShapeDtypeStruct((n, length, WIDTH), BF16)]
    out_specs = [pl.BlockSpec((None, tq, WIDTH), lambda b, i: (b, i, 0))]
    if want_lse:
        out_shape.append(jax.ShapeDtypeStruct((n, length, WIDTH), F32))
        out_specs.append(pl.BlockSpec((None, tq, WIDTH), lambda b, i: (b, i, 0)))
    kern = functools.partial(_banded_kernel, max_dist=max_dist, qb=qb, kv_shared=kv_shared,
                             has_sink=sink is not None, want_lse=want_lse)
    return pl.pallas_call(
        kern,
        out_shape=tuple(out_shape),
        grid=(n, length // tq),
        in_specs=in_specs,
        out_specs=tuple(out_specs),
        scratch_shapes=[pltpu.VMEM((tq + BLOCK, kvw), BF16), pltpu.VMEM((tq + BLOCK, kvw), BF16)],
        compiler_params=_params(("parallel", "parallel")),
        name="banded_attention",
    )(*args)


def _combine_kernel(o1_ref, l1_ref, *rest):
    n_cls = len(A_DILATIONS) - 1
    o_refs, l_refs = rest[:n_cls], rest[n_cls:2 * n_cls]
    out_ref = rest[2 * n_cls]
    o_s, l_s = rest[2 * n_cls + 1:3 * n_cls + 1], rest[3 * n_cls + 1:]
    tm = o1_ref.shape[0]
    for dil, o_ref, l_ref, os_, ls_ in zip(A_DILATIONS[1:], o_refs, l_refs, o_s, l_s):
        for c in range(dil):
            for p in range(N_PAIRS):
                cols = slice(p * LANE, (p + 1) * LANE)
                os_[p, pl.ds(c, tm // dil, stride=dil), :] = o_ref[c, :, cols].astype(F32)
                ls_[p, pl.ds(c, tm // dil, stride=dil), :] = l_ref[c, :, cols]
    for p in range(N_PAIRS):
        cols = slice(p * LANE, (p + 1) * LANE)
        lses = [l1_ref[:, cols]] + [ls_[p] for ls_ in l_s]
        outs = [o1_ref[:, cols].astype(F32)] + [os_[p] for os_ in o_s]
        m = functools.reduce(jnp.maximum, lses)
        ws = [jnp.exp(l - m) for l in lses]
        num = functools.reduce(lambda a, b: a + b, [w * o for w, o in zip(ws, outs)])
        out_ref[:, cols] = (num / functools.reduce(lambda a, b: a + b, ws)).astype(BF16)


def _combine(o1, l1, class_outs, class_lses, batch, tm=1024):
    t_tokens = o1.shape[0]
    tiles = t_tokens // batch // tm
    tok = pl.BlockSpec((tm, WIDTH), lambda i: (i, 0))
    cls = [pl.BlockSpec((None, dil, tm // dil, WIDTH), lambda i: (i // tiles, 0, i % tiles, 0))
           for dil in A_DILATIONS[1:]]
    n_cls = len(cls)
    return pl.pallas_call(
        _combine_kernel,
        out_shape=jax.ShapeDtypeStruct((t_tokens, WIDTH), BF16),
        grid=(t_tokens // tm,),
        in_specs=[tok, tok] + cls + cls,
        out_specs=tok,
        scratch_shapes=[pltpu.VMEM((N_PAIRS, tm, LANE), F32)] * (2 * n_cls),
        compiler_params=_params(("parallel",)),
        name="combine_patterns",
    )(o1, l1, *class_outs, *class_lses)


CUM_BLOCK = 256
BF16_ROWS = 16
UNDERFLOW = 104.0
NORM_SLACK = 1.02


def _split3(x):
    p1 = x.astype(BF16)
    rest = x - p1.astype(F32)
    p2 = rest.astype(BF16)
    p3 = (rest - p2.astype(F32)).astype(BF16)
    return p1, p2, p3


def _cum_kernel(f_ref, bf_ref, augq_ref, augk_ref, carry_s):
    seq = f_ref.shape[0]

    @pl.when(pl.program_id(1) == 0)
    def _():
        carry_s[...] = jnp.zeros_like(carry_s)

    rr = lax.broadcasted_iota(jnp.int32, (CUM_BLOCK, CUM_BLOCK), 0)
    cc = lax.broadcasted_iota(jnp.int32, (CUM_BLOCK, CUM_BLOCK), 1)
    tril = jnp.where(rr >= cc, 1.0, 0.0).astype(BF16)
    n_out = 2 * N_HEADS * LANE
    pr = lax.broadcasted_iota(jnp.int32, (LANE, n_out), 0)
    pc = lax.broadcasted_iota(jnp.int32, (LANE, n_out), 1)
    same_head = pr == pc // (2 * LANE)
    k_side = (pc // LANE) % 2 == 1
    lane64 = pc % HEAD_DIM
    places = [jnp.where(same_head & jnp.logical_not(k_side) & (lane64 == i), 1.0,
                        jnp.where(same_head & k_side & (lane64 == 3 + i), -1.0, 0.0)).astype(BF16)
              for i in range(3)]
    c1d = lax.broadcasted_iota(jnp.int32, (1, n_out), 1)
    c_k = (c1d // LANE) % 2 == 1
    c_l = c1d % HEAD_DIM
    ones_at = (c_k & (c_l < 3)) | (jnp.logical_not(c_k) & (c_l >= 3) & (c_l < 6))
    const = jnp.where(ones_at, 1.0, 0.0).astype(F32)
    bias = bf_ref[...]

    def body(blk, carry):
        off = pl.multiple_of(blk * CUM_BLOCK, CUM_BLOCK)
        x = f_ref[pl.ds(off, CUM_BLOCK), :] + bias
        log_f = jnp.minimum(x, 0.0) - jnp.log1p(jnp.exp(-jnp.abs(x)))
        cum = sum(_dot(tril, piece) for piece in _split3(log_f)) + carry
        aug = sum(_dot(piece, place) for piece, place in zip(_split3(cum), places)) + const
        for head in range(N_HEADS):
            base = head * 2 * LANE
            augq_ref[head, pl.ds(off, CUM_BLOCK), :] = aug[:, base:base + LANE].astype(BF16)
            augk_ref[head, pl.ds(off, CUM_BLOCK), :] = aug[:, base + LANE:base + 2 * LANE].astype(BF16)
        return cum[CUM_BLOCK - 1:CUM_BLOCK, :]

    carry_s[...] = lax.fori_loop(0, seq // CUM_BLOCK, body, carry_s[...])


def _cum_aug(f, b_forget_row, ts=2048):
    b, seq, _ = f.shape
    ts = min(ts, seq)
    out = jax.ShapeDtypeStruct((b, N_HEADS, seq, LANE), BF16)
    ospec = pl.BlockSpec((None, N_HEADS, ts, LANE), lambda bi, si: (bi, 0, si, 0))
    return pl.pallas_call(
        _cum_kernel,
        out_shape=(out, out),
        grid=(b, seq // ts),
        in_specs=[pl.BlockSpec((None, ts, LANE), lambda bi, si: (bi, si, 0)),
                  pl.BlockSpec((1, LANE), lambda bi, si: (0, 0))],
        out_specs=(ospec, ospec),
        scratch_shapes=[pltpu.VMEM((1, LANE), F32)],
        compiler_params=_params(("parallel", "arbitrary")),
        name="forget_cumsum",
    )(f, b_forget_row)


def _fox_kernel(q_ref, k_ref, v_ref, aq_ref, ak_ref, o_ref, kaug, qaug, kstat, *, tq, tk):
    qi = pl.program_id(2)
    n_kb = k_ref.shape[0] // tk
    n_sub = tq // tk
    lane = lax.broadcasted_iota(jnp.int32, (1, LANE), 1)
    lower = lane < HEAD_DIM
    hr = lax.broadcasted_iota(jnp.int32, (LANE, LANE), 0) // HEAD_DIM
    hc = lax.broadcasted_iota(jnp.int32, (LANE, LANE), 1) // HEAD_DIM
    same_head = jnp.where(hr == hc, 1.0, 0.0).astype(BF16)

    def max_sq_norm(x):
        xf = x.astype(F32)
        return jnp.max(_dot((xf * xf).astype(BF16), same_head), axis=0, keepdims=True)

    def lane_sum(row_vec, lo, hi):
        return jnp.sum(jnp.where((lane >= lo) & (lane < hi), row_vec.astype(F32), 0.0), axis=1, keepdims=True)

    @pl.when(qi == 0)
    def _():
        k = k_ref[...]
        kaug[0] = jnp.where(lower, k, ak_ref[0])
        kaug[1] = jnp.where(lower, ak_ref[1], k)

        def stats(kb, carry):
            kn0, kn1, nc0, nc1 = carry
            off = pl.multiple_of(kb * tk, tk)
            n2 = max_sq_norm(k_ref[pl.ds(off, tk), :])
            tail = pl.multiple_of(off + tk - BF16_ROWS, BF16_ROWS)
            here = lane == kb
            kn0 = jnp.where(here, n2[:, 0:1], kn0)
            kn1 = jnp.where(here, n2[:, HEAD_DIM:HEAD_DIM + 1], kn1)
            last0 = ak_ref[0, pl.ds(tail, BF16_ROWS), :][BF16_ROWS - 1:BF16_ROWS, :]
            last1 = ak_ref[1, pl.ds(tail, BF16_ROWS), :][BF16_ROWS - 1:BF16_ROWS, :]
            nc0 = jnp.where(here, lane_sum(last0, 3, 6), nc0)
            nc1 = jnp.where(here, lane_sum(last1, 3, 6), nc1)
            return kn0, kn1, nc0, nc1

        zero = jnp.zeros((1, LANE), F32)
        kn0, kn1, nc0, nc1 = lax.fori_loop(0, n_kb, stats, (zero, zero, zero, zero))
        kstat[0:1, :] = kn0
        kstat[1:2, :] = kn1
        kstat[2:3, :] = nc0
        kstat[3:4, :] = nc1

    q = q_ref[...]
    qaug[0] = jnp.where(lower, q, aq_ref[0])
    qaug[1] = jnp.where(lower, aq_ref[1], q)
    row = lax.broadcasted_iota(jnp.int32, (tq, tk), 0)
    col = lax.broadcasted_iota(jnp.int32, (tq, tk), 1)

    qn = max_sq_norm(q)
    first = None
    for hf in range(2):
        qn2 = qn[:, hf * HEAD_DIM:hf * HEAD_DIM + 1]
        kn2 = kstat[hf:hf + 1, :]
        kn2_all = jnp.max(kn2, axis=1, keepdims=True)
        cum_q = lane_sum(aq_ref[hf, 0:1, :], 0, 3)
        upper = jnp.sqrt(qn2 * kn2) * NORM_SLACK + cum_q + kstat[2 + hf:3 + hf, :] + 1.0
        floor = -jnp.sqrt(qn2 * kn2_all) * NORM_SLACK
        needed = (upper >= floor - UNDERFLOW) & (lane < n_kb)
        first_h = jnp.min(jnp.where(needed, lane.astype(F32), float(n_kb)))
        first = first_h if first is None else jnp.minimum(first, first_h)
    kb_start = jnp.minimum(first.astype(jnp.int32), qi * n_sub)

    def step(kb, carry, mask_shift):
        off = pl.multiple_of(kb * tk, tk)
        v = v_ref[pl.ds(off, tk), :]
        new = []
        for hf in range(2):
            m, l, acc = carry[hf]
            s = _dot_nt(qaug[hf], kaug[hf, pl.ds(off, tk), :])
            if mask_shift is not None:
                s = jnp.where(col + mask_shift <= row, s, NEG_INF)
            m_new = jnp.maximum(m, jnp.max(s, axis=-1, keepdims=True))
            alpha = jnp.exp(m - m_new)
            pr = jnp.exp(s - m_new)
            l = alpha * l + jnp.sum(pr, axis=-1, keepdims=True)
            acc = alpha * acc + _dot(pr.astype(BF16), v)
            new.append((m_new, l, acc))
        return tuple(new)

    one = (jnp.full((tq, 1), NEG_INF, F32), jnp.zeros((tq, 1), F32), jnp.zeros((tq, LANE), F32))
    carry = lax.fori_loop(kb_start, qi * n_sub, functools.partial(step, mask_shift=None), (one, one))
    for d in range(n_sub):
        carry = step(qi * n_sub + d, carry, d * tk)
    (_, l0, acc0), (_, l1, acc1) = carry
    o_ref[...] = jnp.where(lower, acc0 / l0, acc1 / l1).astype(BF16)


def _fox(qkv, augq, augk, tq=512, tk=512):
    b, seq, _ = qkv.shape
    return pl.pallas_call(
        functools.partial(_fox_kernel, tq=tq, tk=tk),
        out_shape=jax.ShapeDtypeStruct((b, seq, WIDTH), BF16),
        grid=(b, N_PAIRS, seq // tq),
        in_specs=[pl.BlockSpec((None, tq, LANE), lambda bi, p, i: (bi, i, COL_QB + p)),
                  pl.BlockSpec((None, seq, LANE), lambda bi, p, i: (bi, 0, COL_KB + p)),
                  pl.BlockSpec((None, seq, LANE), lambda bi, p, i: (bi, 0, COL_VB + p)),
                  pl.BlockSpec((None, 2, tq, LANE), lambda bi, p, i: (bi, p, i, 0)),
                  pl.BlockSpec((None, 2, seq, LANE), lambda bi, p, i: (bi, p, 0, 0))],
        out_specs=pl.BlockSpec((None, tq, LANE), lambda bi, p, i: (bi, i, p)),
        scratch_shapes=[pltpu.VMEM((2, seq, LANE), BF16), pltpu.VMEM((2, tq, LANE), BF16),
                        pltpu.VMEM((8, LANE), F32)],
        compiler_params=_params(("parallel", "parallel", "arbitrary")),
        name="forgetting_attention",
    )(qkv, qkv, qkv, augq, augk)


def _merge_kernel(h_ref, g_ref, oa_ref, ob_ref, oc_ref, wg_ref, bg_ref, wa_ref, wb_ref, wc_ref, wo_ref, out_ref):
    h = h_ref[...]
    xn = _rms(h, g_ref[...]).astype(BF16)
    merged = jnp.zeros(h.shape, F32)
    for idx, (o_ref, w_ref) in enumerate(((oa_ref, wa_ref), (ob_ref, wb_ref), (oc_ref, wc_ref))):
        cols = slice(idx * D_MODEL, (idx + 1) * D_MODEL)
        gate = jax.nn.sigmoid(_dot(xn, wg_ref[:, cols]) + bg_ref[:, cols])
        merged = merged + gate * _dot(o_ref[...], w_ref[...])
    out_ref[...] = h + _dot(merged.astype(BF16), wo_ref[...])


def _merge(h, g, oa, ob, oc, w_gate, b_gate, w_a, w_b, w_c, w_out, tm=512):
    t_tokens = h.shape[0]
    full = lambda shape: pl.BlockSpec(shape, lambda i: (0, 0))
    ospec = pl.BlockSpec((tm, WIDTH), lambda i: (i, 0))
    return pl.pallas_call(
        _merge_kernel,
        out_shape=jax.ShapeDtypeStruct((t_tokens, D_MODEL), F32),
        grid=(t_tokens // tm,),
        in_specs=[pl.BlockSpec((tm, D_MODEL), lambda i: (i, 0)), full((1, D_MODEL)),
                  ospec, ospec, ospec,
                  full((D_MODEL, 3 * D_MODEL)), full((1, 3 * D_MODEL)),
                  full((WIDTH, D_MODEL)), full((WIDTH, D_MODEL)), full((WIDTH, D_MODEL)),
                  full((D_MODEL, D_MODEL))],
        out_specs=pl.BlockSpec((tm, D_MODEL), lambda i: (i, 0)),
        compiler_params=_params(("parallel",)),
        name="gated_merge",
    )(h, g, oa, ob, oc, w_gate, b_gate, w_a, w_b, w_c, w_out)


_CAND_COUNTS = tuple(PEER_TOPK // (i + 1) for i in range(PEER_TOPK))
_CAND_ROWS = 56
PIECE = 256


def _top16_rows(src, dst_ref):
    cur = src
    for r in range(PEER_TOPK):
        mx = jnp.max(cur, axis=0, keepdims=True)
        dst_ref[r:r + 1, :] = mx
        if r + 1 < PEER_TOPK:
            cur = jnp.where(cur == mx, NEG_INF, cur)


def _sort16_pairs():
    n, out, p = PEER_TOPK, [], 1
    while p < n:
        k = p
        while k >= 1:
            for j in range(k % p, n - k, 2 * k):
                for i in range(min(k, n - j - k)):
                    if (i + j) // (2 * p) == (i + j + k) // (2 * p):
                        out.append((i + j, i + j + k))
            k //= 2
        p *= 2
    return tuple(out)


_SORT16 = _sort16_pairs()
SUBLANES = 8


def _top16_of_keys(src, dst_ref):
    rows = [src[SUBLANES * k:SUBLANES * (k + 1), :] for k in range(N_KEYS // SUBLANES)]

    def exchange(i, j):
        rows[i], rows[j] = jnp.maximum(rows[i], rows[j]), jnp.minimum(rows[i], rows[j])

    for i, j in _SORT16:
        exchange(i, j)
    for shift in (4, 2, 1):
        other = [pltpu.roll(r, shift, 0) for r in rows]
        rows = [jnp.maximum(rows[k], other[PEER_TOPK - 1 - k]) for k in range(PEER_TOPK)]
        for d in (8, 4, 2, 1):
            for k in range(PEER_TOPK):
                if k & d == 0:
                    exchange(k, k + d)
    for r in range(PEER_TOPK):
        dst_ref[r:r + 1, :] = rows[r][0:1, :]


def _peer_kernel_plain(h_ref, g_ref, wq_ref, k1_ref, k2_ref, u_ref, vt_ref, out_ref, *scratch, ce, th):
    _peer_body(h_ref, g_ref, wq_ref, k1_ref, k2_ref, u_ref, vt_ref, None, out_ref, *scratch, ce=ce, th=th)


def _peer_kernel_final(h_ref, g_ref, wq_ref, k1_ref, k2_ref, u_ref, vt_ref, fg_ref, out_ref, *scratch, ce, th):
    _peer_body(h_ref, g_ref, wq_ref, k1_ref, k2_ref, u_ref, vt_ref, fg_ref, out_ref, *scratch, ce=ce, th=th)


def _peer_body(h_ref, g_ref, wq_ref, k1_ref, k2_ref, u_ref, vt_ref, fg_ref, out_ref,
               xn_s, s1_s, s2_s, tau_s, v1_s, v2_s, cand_s, top_s, y_s, w_s, *, ce, th):
    j = pl.program_id(1)
    nj = pl.num_programs(1)
    tm = h_ref.shape[0]
    a_per_step = ce // N_KEYS
    halves = [slice(t0, t0 + th) for t0 in range(0, tm, th)]

    def fill_candidates():
        off = 0
        for i, cnt in enumerate(_CAND_COUNTS):
            cand_s[off:off + cnt, :] = v1_s[i:i + 1, :] + v2_s[0:cnt, :]
            off += cnt

    @pl.when(j == 0)
    def _route():
        xn_s[...] = _rms(h_ref[...], g_ref[...]).astype(BF16)
        y_s[...] = jnp.zeros_like(y_s)
        cand_s[...] = jnp.full(cand_s.shape, NEG_INF, F32)
        for tok in halves:
            xn = xn_s[tok, :]
            for hd in range(PEER_HEADS):
                q1 = _dot(xn, wq_ref[:, (2 * hd) * N_KEYS:(2 * hd + 1) * N_KEYS]).astype(BF16)
                q2 = _dot(xn, wq_ref[:, (2 * hd + 1) * N_KEYS:(2 * hd + 2) * N_KEYS]).astype(BF16)
                s1 = _dot_nt(k1_ref[...], q1)
                s2 = _dot_nt(k2_ref[...], q2)
                _top16_of_keys(s1, v1_s)
                _top16_of_keys(s2, v2_s)
                fill_candidates()
                _top16_rows(cand_s[...], top_s)
                top = top_s[...]
                mx = top[0:1, :]
                shift = mx + jnp.log(jnp.sum(jnp.exp(top - mx), axis=0, keepdims=True))
                s1_s[hd, :, tok] = (s1 - shift) * LOG2E - 1.0
                s2_s[hd, :, tok] = s2 * LOG2E
                v1_s[...] = (v1_s[...] - shift) * LOG2E - 1.0
                v2_s[...] = v2_s[...] * LOG2E
                fill_candidates()
                _top16_rows(cand_s[...], top_s)
                tau_s[hd:hd + 1, tok] = top_s[PEER_TOPK - 1:PEER_TOPK, :]

    n_pieces = ce // PIECE

    def act_piece(tok, p):
        return _dot_nt(u_ref[p * PIECE:(p + 1) * PIECE, :], xn_s[tok, :])

    def mix_piece(tok, p, act):
        for aa in range(PIECE // N_KEYS):
            a = j * a_per_step + p * (PIECE // N_KEYS) + aa
            coef = jnp.zeros((N_KEYS, th), F32)
            for hd in range(PEER_HEADS):
                s = s2_s[hd, :, tok] + s1_s[hd, pl.ds(a, 1), tok]
                coef = coef + jnp.where(s >= tau_s[hd:hd + 1, tok], jnp.exp2(s), 0.0)
            x = act[aa * N_KEYS:(aa + 1) * N_KEYS, :]
            gelu2 = x * (1.0 + lax.erf(x * math.sqrt(0.5)))
            rows = slice(p * PIECE + aa * N_KEYS, p * PIECE + (aa + 1) * N_KEYS)
            w_s[rows, tok] = (coef * gelu2).astype(BF16)

    def out_piece(tok, p):
        return _dot(vt_ref[:, p * PIECE:(p + 1) * PIECE], w_s[p * PIECE:(p + 1) * PIECE, tok])

    acts = [act_piece(halves[0], p) for p in range(n_pieces)]
    for hi, tok in enumerate(halves):
        nxt = halves[hi + 1] if hi + 1 < len(halves) else None
        prv = halves[hi - 1] if hi > 0 else None
        nxt_acts, y_prev = [], None
        for p in range(n_pieces):
            mix_piece(tok, p, acts[p])
            if nxt is not None:
                nxt_acts.append(act_piece(nxt, p))
            if prv is not None:
                yp = out_piece(prv, p)
                y_prev = yp if y_prev is None else y_prev + yp
        if prv is not None:
            y_s[:, prv] += y_prev
        acts = nxt_acts
    y_last = out_piece(halves[-1], 0)
    for p in range(1, n_pieces):
        y_last = y_last + out_piece(halves[-1], p)
    y_s[:, halves[-1]] += y_last

    @pl.when(j == nj - 1)
    def _finish():
        y = h_ref[...] + y_s[...].T
        if fg_ref is not None:
            y = _rms(y, fg_ref[...])
        out_ref[...] = y


def _peer(h, g, wq, k1, k2, u, vt, final_g=None, tm=512, ce=2048, th=256):
    t_tokens = h.shape[0]
    tm = min(tm, t_tokens)
    full = lambda shape: pl.BlockSpec(shape, lambda i, j: (0, 0))
    in_specs = [pl.BlockSpec((tm, D_MODEL), lambda i, j: (i, 0)), full((1, D_MODEL)),
                full((D_MODEL, 2 * PEER_HEADS * N_KEYS)), full((N_KEYS, N_KEYS)), full((N_KEYS, N_KEYS)),
                pl.BlockSpec((ce, D_MODEL), lambda i, j: (j, 0)),
                pl.BlockSpec((D_MODEL, ce), lambda i, j: (0, j))]
    args = [h, g, wq, k1, k2, u, vt]
    kern = _peer_kernel_plain
    if final_g is not None:
        in_specs.append(full((1, D_MODEL)))
        args.append(final_g)
        kern = _peer_kernel_final
    head_buf = pltpu.VMEM((PEER_HEADS, N_KEYS, tm), F32)
    return pl.pallas_call(
        functools.partial(kern, ce=ce, th=th),
        out_shape=jax.ShapeDtypeStruct((t_tokens, D_MODEL), F32),
        grid=(t_tokens // tm, N_EXPERTS // ce),
        in_specs=in_specs,
        out_specs=pl.BlockSpec((tm, D_MODEL), lambda i, j: (i, 0)),
        scratch_shapes=[pltpu.VMEM((tm, D_MODEL), BF16), head_buf, head_buf,
                        pltpu.VMEM((PEER_HEADS, tm), F32),
                        pltpu.VMEM((PEER_TOPK, th), F32), pltpu.VMEM((PEER_TOPK, th), F32),
                        pltpu.VMEM((_CAND_ROWS, th), F32), pltpu.VMEM((PEER_TOPK, th), F32),
                        pltpu.VMEM((D_MODEL, tm), F32), pltpu.VMEM((ce, tm), BF16)],
        compiler_params=_params(("parallel", "arbitrary")),
        name="peer_ffn",
    )(*args)


def _transpose_cast_kernel(x_ref, o_ref):
    o_ref[...] = x_ref[...].T.astype(BF16)


def _transpose_cast(x, layer, tr=512):
    _, rows, cols = x.shape
    return pl.pallas_call(
        _transpose_cast_kernel,
        out_shape=jax.ShapeDtypeStruct((cols, rows), BF16),
        grid=(rows // tr,),
        in_specs=[pl.BlockSpec((None, tr, cols), lambda i: (layer, i, 0))],
        out_specs=pl.BlockSpec((cols, tr), lambda i: (0, i)),
        compiler_params=_params(("parallel",)),
        name="transpose_cast",
    )(x)


def _cast_kernel(x_ref, o_ref):
    o_ref[...] = x_ref[...].astype(BF16)


def _cast_layer(x, layer, tr=1024):
    _, rows, cols = x.shape
    tr = min(tr, rows)
    return pl.pallas_call(
        _cast_kernel,
        out_shape=jax.ShapeDtypeStruct((rows, cols), BF16),
        grid=(rows // tr,),
        in_specs=[pl.BlockSpec((None, tr, cols), lambda i: (layer, i, 0))],
        out_specs=pl.BlockSpec((tr, cols), lambda i: (i, 0)),
        compiler_params=_params(("parallel",)),
        name="cast_layer",
    )(x)


_IN_SIZES = (WIDTH,) * 6 + (N_HEADS, WIDTH, C_KV_HEADS * HEAD_DIM, C_KV_HEADS * HEAD_DIM, 3 * D_MODEL)
_IN_STARTS = tuple(sum(_IN_SIZES[:k]) for k in range(len(_IN_SIZES)))
_C_HEAD_ORDER = (0, 3, 1, 4, 2, 5)


def _w_in_kernel(w_ref, watt_ref, wgate_ref):
    def piece(idx, offset=0, size=None):
        start = _IN_STARTS[idx] + offset
        return w_ref[:, start:start + (_IN_SIZES[idx] if size is None else size)].astype(BF16)

    col = 0
    for idx in (0, 1, 2, 3, 4, 5):
        watt_ref[:, col:col + WIDTH] = piece(idx)
        col += WIDTH
    for hd in _C_HEAD_ORDER:
        watt_ref[:, col:col + HEAD_DIM] = piece(7, hd * HEAD_DIM, HEAD_DIM)
        col += HEAD_DIM
    for idx in (8, 9):
        watt_ref[:, col:col + LANE] = piece(idx)
        col += LANE
    forget = piece(6)
    watt_ref[:, col:col + LANE] = jnp.concatenate(
        [forget, jnp.zeros((forget.shape[0], LANE - N_HEADS), BF16)], axis=1)
    wgate_ref[...] = piece(10)


def _w_in_layout(w_in, layer, tr=256):
    _, rows, cols = w_in.shape
    att_cols = (N_QKV_BLOCKS + 1) * LANE
    return pl.pallas_call(
        _w_in_kernel,
        out_shape=(jax.ShapeDtypeStruct((rows, att_cols), BF16), jax.ShapeDtypeStruct((rows, 3 * D_MODEL), BF16)),
        grid=(rows // tr,),
        in_specs=[pl.BlockSpec((None, tr, cols), lambda i: (layer, i, 0))],
        out_specs=(pl.BlockSpec((tr, att_cols), lambda i: (i, 0)), pl.BlockSpec((tr, 3 * D_MODEL), lambda i: (i, 0))),
        compiler_params=_params(("parallel",)),
        name="w_in_layout",
    )(w_in)


def _rope_tables(positions):
    inv_freq = ROPE_THETA ** (-jnp.arange(0, HEAD_DIM, 2, dtype=F32) / HEAD_DIM)
    ang = positions.astype(F32).reshape(-1, 1) * inv_freq
    cos, sin = jnp.cos(ang), jnp.sin(ang)
    return jnp.concatenate([cos] * 4, axis=-1), jnp.concatenate([-sin, sin, -sin, sin], axis=-1)


def _permute_heads(w, axis):
    heads = [lax.slice_in_dim(w, hd * HEAD_DIM, (hd + 1) * HEAD_DIM, axis=axis) for hd in _C_HEAD_ORDER]
    return jnp.concatenate(heads, axis=axis)


def _mixer(h, cos4, sin4, batch, norm_g, w_att, wg, b_forget, sinks, b_gate, w_br_a, w_br_b, w_br_c, w_out):
    t_tokens = h.shape[0]
    seq = t_tokens // batch
    qkv, f, *qkv_classes = _in_proj(h, norm_g.reshape(1, -1), w_att, cos4, sin4, batch)
    qkv3 = qkv.reshape(batch, seq, -1)

    o1, l1 = _banded(qkv3, COL_QA, COL_KA, COL_VA, A_MAX_DIST, kv_shared=False)
    class_outs, class_lses = [], []
    for dil, x in zip(A_DILATIONS[1:], qkv_classes):
        o, lse = _banded(x.reshape(batch * dil, seq // dil, -1), COL_QA, COL_KA, COL_VA, A_MAX_DIST, kv_shared=False)
        class_outs.append(o.reshape(batch, dil, seq // dil, WIDTH))
        class_lses.append(lse.reshape(batch, dil, seq // dil, WIDTH))
    oa = _combine(o1.reshape(t_tokens, WIDTH), l1.reshape(t_tokens, WIDTH), class_outs, class_lses, batch)

    sink_tab = jnp.stack([sinks[hd] for hd in _C_HEAD_ORDER]).astype(F32)
    (oc,) = _banded(qkv3, COL_QC, COL_KC, COL_VC, C_MAX_DIST, kv_shared=True, sink=sink_tab, want_lse=False)
    oc = oc.reshape(t_tokens, WIDTH)

    bf_row = jnp.pad(b_forget.astype(F32), (0, LANE - N_HEADS)).reshape(1, LANE)
    augq, augk = _cum_aug(f.reshape(batch, seq, LANE), bf_row)
    ob = _fox(qkv3, augq, augk).reshape(t_tokens, WIDTH)

    return _merge(h, norm_g.reshape(1, -1), oa, ob, oc, wg, b_gate.reshape(1, -1).astype(F32),
                  w_br_a.astype(BF16), w_br_b.astype(BF16), _permute_heads(w_br_c, 0).astype(BF16),
                  w_out.astype(BF16))


def kernel(x, positions, norm1_g, w_in, b_forget, sinks, b_gate, w_br_a, w_br_b, w_br_c, w_out, norm2_g,
           peer_wq, peer_k1, peer_k2, peer_u, peer_v, final_g):
    batch, seq, _ = x.shape
    depth = w_in.shape[0]
    cos4, sin4 = _rope_tables(positions)
    h = x.reshape(batch * seq, D_MODEL)
    for l in range(depth):
        w_att, w_gate = _w_in_layout(w_in, l)
        h = _mixer(h, cos4, sin4, batch, norm1_g[l], w_att, w_gate, b_forget[l], sinks[l], b_gate[l],
                   w_br_a[l], w_br_b[l], w_br_c[l], w_out[l])
        h = _peer(h, norm2_g[l].reshape(1, -1), _cast_layer(peer_wq, l), peer_k1[l].astype(BF16),
                  peer_k2[l].astype(BF16), _cast_layer(peer_u, l), _transpose_cast(peer_v, l),
                  final_g=final_g.reshape(1, -1) if l == depth - 1 else None)
    return h.reshape(batch, seq, D_MODEL)
```

```python
import functools
import math

import jax
import jax.numpy as jnp
from jax import lax
from jax.experimental import pallas as pl
from jax.experimental.pallas import tpu as pltpu

F32 = jnp.float32
BF16 = jnp.bfloat16

D_MODEL = 1024
HEAD_DIM = 64
N_HEADS = 6
N_PAIRS = N_HEADS // 2
C_KV_HEADS = 2
A_DILATIONS = (1, 4, 16)
A_MAX_DIST = 128
C_MAX_DIST = 127
BLOCK = 128
LANE = 128
ROPE_THETA = 10000.0
RMS_EPS = 1e-6
NEG_INF = -1e30
PEER_HEADS = 8
N_KEYS = 128
N_EXPERTS = N_KEYS * N_KEYS
PEER_TOPK = 16
WIDTH = N_HEADS * HEAD_DIM

COL_QA, COL_KA, COL_VA = 0, 3, 6
COL_QB, COL_KB, COL_VB = 9, 12, 15
COL_QC, COL_KC, COL_VC = 18, 21, 22
N_QKV_BLOCKS = 23
COL_F = 23
A_BLOCKS = 9
ROPE_BLOCKS = (0, 1, 2, 3, 4, 5, 18, 19, 20, 21)
QUERY_BLOCKS = (0, 1, 2, 9, 10, 11, 18, 19, 20)
QK_SCALE = HEAD_DIM ** -0.5
LOG2E = math.log2(math.e)

VMEM_LIMIT = 56 * 1024 * 1024

IN_PROJ_ROWS = 512
BANDED_QUERY_BLOCKS = 8
COMBINE_ROWS = 1024
CUM_ROWS = 2048
FOX_QUERIES = 512
FOX_KEYS = 512
MERGE_ROWS = 512
PEER_ROWS = 512
PEER_EXPERTS = 2048
PEER_HALF = 256
TABLE_ROWS = 512


def _params(sem):
    return pltpu.CompilerParams(dimension_semantics=sem, vmem_limit_bytes=VMEM_LIMIT)


def _dot(a, b):
    return jnp.dot(a, b, preferred_element_type=F32)


def _dot_nt(a, b):
    return lax.dot_general(a, b, (((1,), (1,)), ((), ())), preferred_element_type=F32)


def _rms(x, g):
    var = jnp.mean(x * x, axis=-1, keepdims=True)
    return x * lax.rsqrt(var + RMS_EPS) * g


def _in_proj_kernel(h_ref, g_ref, w_ref, cos_ref, sin_ref, qkv_ref, f_ref, *rest):
    class_refs, proj_s, fin_s = rest[:-2], rest[-2], rest[-1]
    tm = h_ref.shape[0]
    xn = _rms(h_ref[...], g_ref[...]).astype(BF16)
    proj_s[...] = _dot(xn, w_ref[...])
    cos = cos_ref[...]
    sin = sin_ref[...]
    lane = lax.broadcasted_iota(jnp.int32, cos.shape, 1)
    first_half = (lane % HEAD_DIM) < (HEAD_DIM // 2)
    for c in range(N_QKV_BLOCKS):
        t = proj_s[:, c * LANE:(c + 1) * LANE]
        if c in ROPE_BLOCKS:
            rot = jnp.where(first_half, pltpu.roll(t, LANE - HEAD_DIM // 2, 1), pltpu.roll(t, HEAD_DIM // 2, 1))
            t = t * cos + rot * sin
        if c in QUERY_BLOCKS:
            t = t * QK_SCALE
        if c < A_BLOCKS:
            fin_s[c] = t
        qkv_ref[:, c * LANE:(c + 1) * LANE] = t.astype(BF16)
    f_ref[...] = proj_s[:, COL_F * LANE:(COL_F + 1) * LANE]
    for dil, ref in zip(A_DILATIONS[1:], class_refs):
        for c in range(dil):
            for blk in range(A_BLOCKS):
                rows = fin_s[blk, pl.ds(c, tm // dil, stride=dil), :]
                ref[c, :, blk * LANE:(blk + 1) * LANE] = rows.astype(BF16)


def _in_proj(h, g, w_att, cos4, sin4, batch, tm=IN_PROJ_ROWS):
    t_tokens = h.shape[0]
    seq = t_tokens // batch
    tiles = seq // tm
    n_cols = w_att.shape[1]
    a_cols = A_BLOCKS * LANE
    class_shapes = tuple(jax.ShapeDtypeStruct((batch, dil, seq // dil, a_cols), BF16) for dil in A_DILATIONS[1:])
    class_specs = tuple(pl.BlockSpec((None, dil, tm // dil, a_cols), lambda i: (i // tiles, 0, i % tiles, 0))
                        for dil in A_DILATIONS[1:])
    return pl.pallas_call(
        _in_proj_kernel,
        out_shape=(jax.ShapeDtypeStruct((t_tokens, N_QKV_BLOCKS * LANE), BF16),
                   jax.ShapeDtypeStruct((t_tokens, LANE), F32)) + class_shapes,
        grid=(t_tokens // tm,),
        in_specs=[pl.BlockSpec((tm, D_MODEL), lambda i: (i, 0)),
                  pl.BlockSpec((1, D_MODEL), lambda i: (0, 0)),
                  pl.BlockSpec((D_MODEL, n_cols), lambda i: (0, 0)),
                  pl.BlockSpec((tm, LANE), lambda i: (i, 0)),
                  pl.BlockSpec((tm, LANE), lambda i: (i, 0))],
        out_specs=(pl.BlockSpec((tm, N_QKV_BLOCKS * LANE), lambda i: (i, 0)),
                   pl.BlockSpec((tm, LANE), lambda i: (i, 0))) + class_specs,
        scratch_shapes=[pltpu.VMEM((tm, n_cols), F32), pltpu.VMEM((A_BLOCKS, tm, LANE), F32)],
        compiler_params=_params(("parallel",)),
        name="in_proj",
    )(h, g, w_att, cos4, sin4)


def _banded_kernel(*refs, max_dist, qb, kv_shared, has_sink, want_lse):
    refs = list(refs)
    q_ref, kp_ref, kc_ref, vp_ref, vc_ref = refs[:5]
    pos = 5
    sink_ref = None
    if has_sink:
        sink_ref = refs[pos]
        pos += 1
    o_ref = refs[pos]
    pos += 1
    lse_ref = None
    if want_lse:
        lse_ref = refs[pos]
        pos += 1
    kwin, vwin = refs[pos], refs[pos + 1]

    i = pl.program_id(1)
    kwin[0:BLOCK, :] = kp_ref[...]
    kwin[BLOCK:, :] = kc_ref[...]
    vwin[0:BLOCK, :] = vp_ref[...]
    vwin[BLOCK:, :] = vc_ref[...]

    row = lax.broadcasted_iota(jnp.int32, (BLOCK, 2 * BLOCK), 0)
    col = lax.broadcasted_iota(jnp.int32, (BLOCK, 2 * BLOCK), 1)
    dist = row + BLOCK - col
    band = (dist >= 0) & (dist <= max_dist)
    lane = lax.broadcasted_iota(jnp.int32, (1, LANE), 1)
    lower = lane < HEAD_DIM

    def body(sb, carry):
        off = pl.multiple_of(sb * BLOCK, BLOCK)
        valid = band & ((col >= BLOCK) | (i * qb + sb > 0))
        for p in range(N_PAIRS):
            q = q_ref[pl.ds(off, BLOCK), p * LANE:(p + 1) * LANE]
            kv_cols = slice(0, LANE) if kv_shared else slice(p * LANE, (p + 1) * LANE)
            k = kwin[pl.ds(off, 2 * BLOCK), kv_cols]
            v = vwin[pl.ds(off, 2 * BLOCK), kv_cols]
            o_pair = jnp.zeros((BLOCK, LANE), F32)
            lse_pair = jnp.zeros((BLOCK, LANE), F32)
            for hf in range(2):
                sel = lower if hf == 0 else jnp.logical_not(lower)
                qm = jnp.where(sel, q, jnp.zeros_like(q))
                s = _dot_nt(qm, k)
                s = jnp.where(valid, s, NEG_INF)
                m = jnp.max(s, axis=-1, keepdims=True)
                if has_sink:
                    sk = sink_ref[2 * p + hf]
                    m = jnp.maximum(m, sk)
                pr = jnp.exp(s - m)
                den = jnp.sum(pr, axis=-1, keepdims=True)
                if has_sink:
                    den = den + jnp.exp(sk - m)
                o = _dot(pr.astype(BF16), v) / den
                o_pair = jnp.where(sel, o, o_pair)
                if want_lse:
                    lse_pair = jnp.where(sel, m + jnp.log(den), lse_pair)
            o_ref[pl.ds(off, BLOCK), p * LANE:(p + 1) * LANE] = o_pair.astype(BF16)
            if want_lse:
                lse_ref[pl.ds(off, BLOCK), p * LANE:(p + 1) * LANE] = lse_pair
        return carry

    lax.fori_loop(0, qb, body, 0, unroll=True)


def _banded(x, qcol, kcol, vcol, max_dist, kv_shared, sink=None, want_lse=True, qb=BANDED_QUERY_BLOCKS):
    n, length, _ = x.shape
    qb = min(qb, length // BLOCK)
    tq = qb * BLOCK
    kvw = LANE if kv_shared else WIDTH
    kblk = kcol if kv_shared else kcol // N_PAIRS
    vblk = vcol if kv_shared else vcol // N_PAIRS
    in_specs = [
        pl.BlockSpec((None, tq, WIDTH), lambda b, i: (b, i, qcol // N_PAIRS)),
        pl.BlockSpec((None, BLOCK, kvw), lambda b, i: (b, jnp.maximum(i * qb - 1, 0), kblk)),
        pl.BlockSpec((None, tq, kvw), lambda b, i: (b, i, kblk)),
        pl.BlockSpec((None, BLOCK, kvw), lambda b, i: (b, jnp.maximum(i * qb - 1, 0), vblk)),
        pl.BlockSpec((None, tq, kvw), lambda b, i: (b, i, vblk)),
    ]
    args = [x, x, x, x, x]
    if sink is not None:
        in_specs.append(pl.BlockSpec(memory_space=pltpu.SMEM))
        args.append(sink)
    out_shape = [jax.ShapeDtypeStruct((n, length, WIDTH), BF16)]
    out_specs = [pl.BlockSpec((None, tq, WIDTH), lambda b, i: (b, i, 0))]
    if want_lse:
        out_shape.append(jax.ShapeDtypeStruct((n, length, WIDTH), F32))
        out_specs.append(pl.BlockSpec((None, tq, WIDTH), lambda b, i: (b, i, 0)))
    kern = functools.partial(_banded_kernel, max_dist=max_dist, qb=qb, kv_shared=kv_shared,
                             has_sink=sink is not None, want_lse=want_lse)
    return pl.pallas_call(
        kern,
        out_shape=tuple(out_shape),
        grid=(n, length // tq),
        in_specs=in_specs,
        out_specs=tuple(out_specs),
        scratch_shapes=[pltpu.VMEM((tq + BLOCK, kvw), BF16), pltpu.VMEM((tq + BLOCK, kvw), BF16)],
        compiler_params=_params(("parallel", "parallel")),
        name="banded_attention",
    )(*args)


def _combine_kernel(o1_ref, l1_ref, *rest):
    n_cls = len(A_DILATIONS) - 1
    o_refs, l_refs = rest[:n_cls], rest[n_cls:2 * n_cls]
    out_ref = rest[2 * n_cls]
    o_s, l_s = rest[2 * n_cls + 1:3 * n_cls + 1], rest[3 * n_cls + 1:]
    tm = o1_ref.shape[0]
    for dil, o_ref, l_ref, os_, ls_ in zip(A_DILATIONS[1:], o_refs, l_refs, o_s, l_s):
        for c in range(dil):
            for p in range(N_PAIRS):
                cols = slice(p * LANE, (p + 1) * LANE)
                os_[p, pl.ds(c, tm // dil, stride=dil), :] = o_ref[c, :, cols].astype(F32)
                ls_[p, pl.ds(c, tm // dil, stride=dil), :] = l_ref[c, :, cols]
    for p in range(N_PAIRS):
        cols = slice(p * LANE, (p + 1) * LANE)
        lses = [l1_ref[:, cols]] + [ls_[p] for ls_ in l_s]
        outs = [o1_ref[:, cols].astype(F32)] + [os_[p] for os_ in o_s]
        m = functools.reduce(jnp.maximum, lses)
        ws = [jnp.exp(l - m) for l in lses]
        num = functools.reduce(lambda a, b: a + b, [w * o for w, o in zip(ws, outs)])
        out_ref[:, cols] = (num / functools.reduce(lambda a, b: a + b, ws)).astype(BF16)


def _combine(o1, l1, class_outs, class_lses, batch, tm=COMBINE_ROWS):
    t_tokens = o1.shape[0]
    tiles = t_tokens // batch // tm
    tok = pl.BlockSpec((tm, WIDTH), lambda i: (i, 0))
    cls = [pl.BlockSpec((None, dil, tm // dil, WIDTH), lambda i: (i // tiles, 0, i % tiles, 0))
           for dil in A_DILATIONS[1:]]
    n_cls = len(cls)
    return pl.pallas_call(
        _combine_kernel,
        out_shape=jax.ShapeDtypeStruct((t_tokens, WIDTH), BF16),
        grid=(t_tokens // tm,),
        in_specs=[tok, tok] + cls + cls,
        out_specs=tok,
        scratch_shapes=[pltpu.VMEM((N_PAIRS, tm, LANE), F32)] * (2 * n_cls),
        compiler_params=_params(("parallel",)),
        name="combine_patterns",
    )(o1, l1, *class_outs, *class_lses)


CUM_BLOCK = 256
BF16_ROWS = 16
UNDERFLOW = 104.0
NORM_SLACK = 1.02


def _split3(x):
    p1 = x.astype(BF16)
    rest = x - p1.astype(F32)
    p2 = rest.astype(BF16)
    p3 = (rest - p2.astype(F32)).astype(BF16)
    return p1, p2, p3


def _cum_kernel(f_ref, bf_ref, augq_ref, augk_ref, carry_s):
    seq = f_ref.shape[0]

    @pl.when(pl.program_id(1) == 0)
    def _():
        carry_s[...] = jnp.zeros_like(carry_s)

    rr = lax.broadcasted_iota(jnp.int32, (CUM_BLOCK, CUM_BLOCK), 0)
    cc = lax.broadcasted_iota(jnp.int32, (CUM_BLOCK, CUM_BLOCK), 1)
    tril = jnp.where(rr >= cc, 1.0, 0.0).astype(BF16)
    n_out = 2 * N_HEADS * LANE
    pr = lax.broadcasted_iota(jnp.int32, (LANE, n_out), 0)
    pc = lax.broadcasted_iota(jnp.int32, (LANE, n_out), 1)
    same_head = pr == pc // (2 * LANE)
    k_side = (pc // LANE) % 2 == 1
    lane64 = pc % HEAD_DIM
    places = [jnp.where(same_head & jnp.logical_not(k_side) & (lane64 == i), 1.0,
                        jnp.where(same_head & k_side & (lane64 == 3 + i), -1.0, 0.0)).astype(BF16)
              for i in range(3)]
    c1d = lax.broadcasted_iota(jnp.int32, (1, n_out), 1)
    c_k = (c1d // LANE) % 2 == 1
    c_l = c1d % HEAD_DIM
    ones_at = (c_k & (c_l < 3)) | (jnp.logical_not(c_k) & (c_l >= 3) & (c_l < 6))
    const = jnp.where(ones_at, 1.0, 0.0).astype(F32)
    bias = bf_ref[...]

    def body(blk, carry):
        off = pl.multiple_of(blk * CUM_BLOCK, CUM_BLOCK)
        x = f_ref[pl.ds(off, CUM_BLOCK), :] + bias
        log_f = jnp.minimum(x, 0.0) - jnp.log1p(jnp.exp(-jnp.abs(x)))
        cum = sum(_dot(tril, piece) for piece in _split3(log_f)) + carry
        aug = sum(_dot(piece, place) for piece, place in zip(_split3(cum), places)) + const
        for head in range(N_HEADS):
            base = head * 2 * LANE
            augq_ref[head, pl.ds(off, CUM_BLOCK), :] = aug[:, base:base + LANE].astype(BF16)
            augk_ref[head, pl.ds(off, CUM_BLOCK), :] = aug[:, base + LANE:base + 2 * LANE].astype(BF16)
        return cum[CUM_BLOCK - 1:CUM_BLOCK, :]

    carry_s[...] = lax.fori_loop(0, seq // CUM_BLOCK, body, carry_s[...])


def _cum_aug(f, b_forget_row, ts=CUM_ROWS):
    b, seq, _ = f.shape
    ts = min(ts, seq)
    out = jax.ShapeDtypeStruct((b, N_HEADS, seq, LANE), BF16)
    ospec = pl.BlockSpec((None, N_HEADS, ts, LANE), lambda bi, si: (bi, 0, si, 0))
    return pl.pallas_call(
        _cum_kernel,
        out_shape=(out, out),
        grid=(b, seq // ts),
        in_specs=[pl.BlockSpec((None, ts, LANE), lambda bi, si: (bi, si, 0)),
                  pl.BlockSpec((1, LANE), lambda bi, si: (0, 0))],
        out_specs=(ospec, ospec),
        scratch_shapes=[pltpu.VMEM((1, LANE), F32)],
        compiler_params=_params(("parallel", "arbitrary")),
        name="forget_cumsum",
    )(f, b_forget_row)


def _fox_kernel(q_ref, k_ref, v_ref, aq_ref, ak_ref, o_ref, kaug, qaug, kstat, *, tq, tk):
    qi = pl.program_id(2)
    n_kb = k_ref.shape[0] // tk
    n_sub = tq // tk
    lane = lax.broadcasted_iota(jnp.int32, (1, LANE), 1)
    lower = lane < HEAD_DIM
    hr = lax.broadcasted_iota(jnp.int32, (LANE, LANE), 0) // HEAD_DIM
    hc = lax.broadcasted_iota(jnp.int32, (LANE, LANE), 1) // HEAD_DIM
    same_head = jnp.where(hr == hc, 1.0, 0.0).astype(BF16)

    def max_sq_norm(x):
        xf = x.astype(F32)
        return jnp.max(_dot((xf * xf).astype(BF16), same_head), axis=0, keepdims=True)

    def lane_sum(row_vec, lo, hi):
        return jnp.sum(jnp.where((lane >= lo) & (lane < hi), row_vec.astype(F32), 0.0), axis=1, keepdims=True)

    @pl.when(qi == 0)
    def _():
        k = k_ref[...]
        kaug[0] = jnp.where(lower, k, ak_ref[0])
        kaug[1] = jnp.where(lower, ak_ref[1], k)

        def stats(kb, carry):
            kn0, kn1, nc0, nc1 = carry
            off = pl.multiple_of(kb * tk, tk)
            n2 = max_sq_norm(k_ref[pl.ds(off, tk), :])
            tail = pl.multiple_of(off + tk - BF16_ROWS, BF16_ROWS)
            here = lane == kb
            kn0 = jnp.where(here, n2[:, 0:1], kn0)
            kn1 = jnp.where(here, n2[:, HEAD_DIM:HEAD_DIM + 1], kn1)
            last0 = ak_ref[0, pl.ds(tail, BF16_ROWS), :][BF16_ROWS - 1:BF16_ROWS, :]
            last1 = ak_ref[1, pl.ds(tail, BF16_ROWS), :][BF16_ROWS - 1:BF16_ROWS, :]
            nc0 = jnp.where(here, lane_sum(last0, 3, 6), nc0)
            nc1 = jnp.where(here, lane_sum(last1, 3, 6), nc1)
            return kn0, kn1, nc0, nc1

        zero = jnp.zeros((1, LANE), F32)
        kn0, kn1, nc0, nc1 = lax.fori_loop(0, n_kb, stats, (zero, zero, zero, zero))
        kstat[0:1, :] = kn0
        kstat[1:2, :] = kn1
        kstat[2:3, :] = nc0
        kstat[3:4, :] = nc1

    q = q_ref[...]
    qaug[0] = jnp.where(lower, q, aq_ref[0])
    qaug[1] = jnp.where(lower, aq_ref[1], q)
    row = lax.broadcasted_iota(jnp.int32, (tq, tk), 0)
    col = lax.broadcasted_iota(jnp.int32, (tq, tk), 1)

    qn = max_sq_norm(q)
    first = None
    for hf in range(2):
        qn2 = qn[:, hf * HEAD_DIM:hf * HEAD_DIM + 1]
        kn2 = kstat[hf:hf + 1, :]
        kn2_all = jnp.max(kn2, axis=1, keepdims=True)
        cum_q = lane_sum(aq_ref[hf, 0:1, :], 0, 3)
        upper = jnp.sqrt(qn2 * kn2) * NORM_SLACK + cum_q + kstat[2 + hf:3 + hf, :] + 1.0
        floor = -jnp.sqrt(qn2 * kn2_all) * NORM_SLACK
        needed = (upper >= floor - UNDERFLOW) & (lane < n_kb)
        first_h = jnp.min(jnp.where(needed, lane.astype(F32), float(n_kb)))
        first = first_h if first is None else jnp.minimum(first, first_h)
    kb_start = jnp.minimum(first.astype(jnp.int32), qi * n_sub)

    def step(kb, carry, mask_shift):
        off = pl.multiple_of(kb * tk, tk)
        v = v_ref[pl.ds(off, tk), :]
        new = []
        for hf in range(2):
            m, l, acc = carry[hf]
            s = _dot_nt(qaug[hf], kaug[hf, pl.ds(off, tk), :])
            if mask_shift is not None:
                s = jnp.where(col + mask_shift <= row, s, NEG_INF)
            m_new = jnp.maximum(m, jnp.max(s, axis=-1, keepdims=True))
            alpha = jnp.exp(m - m_new)
            pr = jnp.exp(s - m_new)
            l = alpha * l + jnp.sum(pr, axis=-1, keepdims=True)
            acc = alpha * acc + _dot(pr.astype(BF16), v)
            new.append((m_new, l, acc))
        return tuple(new)

    one = (jnp.full((tq, 1), NEG_INF, F32), jnp.zeros((tq, 1), F32), jnp.zeros((tq, LANE), F32))
    carry = lax.fori_loop(kb_start, qi * n_sub, functools.partial(step, mask_shift=None), (one, one))
    for d in range(n_sub):
        carry = step(qi * n_sub + d, carry, d * tk)
    (_, l0, acc0), (_, l1, acc1) = carry
    o_ref[...] = jnp.where(lower, acc0 / l0, acc1 / l1).astype(BF16)


def _fox(qkv, augq, augk, tq=FOX_QUERIES, tk=FOX_KEYS):
    b, seq, _ = qkv.shape
    return pl.pallas_call(
        functools.partial(_fox_kernel, tq=tq, tk=tk),
        out_shape=jax.ShapeDtypeStruct((b, seq, WIDTH), BF16),
        grid=(b, N_PAIRS, seq // tq),
        in_specs=[pl.BlockSpec((None, tq, LANE), lambda bi, p, i: (bi, i, COL_QB + p)),
                  pl.BlockSpec((None, seq, LANE), lambda bi, p, i: (bi, 0, COL_KB + p)),
                  pl.BlockSpec((None, seq, LANE), lambda bi, p, i: (bi, 0, COL_VB + p)),
                  pl.BlockSpec((None, 2, tq, LANE), lambda bi, p, i: (bi, p, i, 0)),
                  pl.BlockSpec((None, 2, seq, LANE), lambda bi, p, i: (bi, p, 0, 0))],
        out_specs=pl.BlockSpec((None, tq, LANE), lambda bi, p, i: (bi, i, p)),
        scratch_shapes=[pltpu.VMEM((2, seq, LANE), BF16), pltpu.VMEM((2, tq, LANE), BF16),
                        pltpu.VMEM((8, LANE), F32)],
        compiler_params=_params(("parallel", "parallel", "arbitrary")),
        name="forgetting_attention",
    )(qkv, qkv, qkv, augq, augk)


def _merge_kernel(h_ref, g_ref, oa_ref, ob_ref, oc_ref, wg_ref, bg_ref, wa_ref, wb_ref, wc_ref, wo_ref, out_ref):
    h = h_ref[...]
    xn = _rms(h, g_ref[...]).astype(BF16)
    merged = jnp.zeros(h.shape, F32)
    for idx, (o_ref, w_ref) in enumerate(((oa_ref, wa_ref), (ob_ref, wb_ref), (oc_ref, wc_ref))):
        cols = slice(idx * D_MODEL, (idx + 1) * D_MODEL)
        gate = jax.nn.sigmoid(_dot(xn, wg_ref[:, cols]) + bg_ref[:, cols])
        merged = merged + gate * _dot(o_ref[...], w_ref[...])
    out_ref[...] = h + _dot(merged.astype(BF16), wo_ref[...])


def _merge(h, g, oa, ob, oc, w_gate, b_gate, w_a, w_b, w_c, w_out, tm=MERGE_ROWS):
    t_tokens = h.shape[0]
    full = lambda shape: pl.BlockSpec(shape, lambda i: (0, 0))
    ospec = pl.BlockSpec((tm, WIDTH), lambda i: (i, 0))
    return pl.pallas_call(
        _merge_kernel,
        out_shape=jax.ShapeDtypeStruct((t_tokens, D_MODEL), F32),
        grid=(t_tokens // tm,),
        in_specs=[pl.BlockSpec((tm, D_MODEL), lambda i: (i, 0)), full((1, D_MODEL)),
                  ospec, ospec, ospec,
                  full((D_MODEL, 3 * D_MODEL)), full((1, 3 * D_MODEL)),
                  full((WIDTH, D_MODEL)), full((WIDTH, D_MODEL)), full((WIDTH, D_MODEL)),
                  full((D_MODEL, D_MODEL))],
        out_specs=pl.BlockSpec((tm, D_MODEL), lambda i: (i, 0)),
        compiler_params=_params(("parallel",)),
        name="gated_merge",
    )(h, g, oa, ob, oc, w_gate, b_gate, w_a, w_b, w_c, w_out)


_CAND_COUNTS = tuple(PEER_TOPK // (i + 1) for i in range(PEER_TOPK))
_CAND_ROWS = 56
PIECE = 256


def _top16_rows(src, dst_ref):
    cur = src
    for r in range(PEER_TOPK):
        mx = jnp.max(cur, axis=0, keepdims=True)
        dst_ref[r:r + 1, :] = mx
        if r + 1 < PEER_TOPK:
            cur = jnp.where(cur == mx, NEG_INF, cur)


def _sort16_pairs():
    n, out, p = PEER_TOPK, [], 1
    while p < n:
        k = p
        while k >= 1:
            for j in range(k % p, n - k, 2 * k):
                for i in range(min(k, n - j - k)):
                    if (i + j) // (2 * p) == (i + j + k) // (2 * p):
                        out.append((i + j, i + j + k))
            k //= 2
        p *= 2
    return tuple(out)


_SORT16 = _sort16_pairs()
SUBLANES = 8


def _top16_of_keys(src, dst_ref):
    rows = [src[SUBLANES * k:SUBLANES * (k + 1), :] for k in range(N_KEYS // SUBLANES)]

    def exchange(i, j):
        rows[i], rows[j] = jnp.maximum(rows[i], rows[j]), jnp.minimum(rows[i], rows[j])

    for i, j in _SORT16:
        exchange(i, j)
    for shift in (4, 2, 1):
        other = [pltpu.roll(r, shift, 0) for r in rows]
        rows = [jnp.maximum(rows[k], other[PEER_TOPK - 1 - k]) for k in range(PEER_TOPK)]
        for d in (8, 4, 2, 1):
            for k in range(PEER_TOPK):
                if k & d == 0:
                    exchange(k, k + d)
    for r in range(PEER_TOPK):
        dst_ref[r:r + 1, :] = rows[r][0:1, :]


def _peer_kernel_plain(h_ref, g_ref, wq_ref, k1_ref, k2_ref, u_ref, vt_ref, out_ref, *scratch, ce, th):
    _peer_body(h_ref, g_ref, wq_ref, k1_ref, k2_ref, u_ref, vt_ref, None, out_ref, *scratch, ce=ce, th=th)


def _peer_kernel_final(h_ref, g_ref, wq_ref, k1_ref, k2_ref, u_ref, vt_ref, fg_ref, out_ref, *scratch, ce, th):
    _peer_body(h_ref, g_ref, wq_ref, k1_ref, k2_ref, u_ref, vt_ref, fg_ref, out_ref, *scratch, ce=ce, th=th)


def _peer_body(h_ref, g_ref, wq_ref, k1_ref, k2_ref, u_ref, vt_ref, fg_ref, out_ref,
               xn_s, s1_s, s2_s, tau_s, v1_s, v2_s, cand_s, top_s, y_s, w_s, *, ce, th):
    j = pl.program_id(1)
    nj = pl.num_programs(1)
    tm = h_ref.shape[0]
    a_per_step = ce // N_KEYS
    halves = [slice(t0, t0 + th) for t0 in range(0, tm, th)]

    def fill_candidates():
        off = 0
        for i, cnt in enumerate(_CAND_COUNTS):
            cand_s[off:off + cnt, :] = v1_s[i:i + 1, :] + v2_s[0:cnt, :]
            off += cnt

    @pl.when(j == 0)
    def _route():
        xn_s[...] = _rms(h_ref[...], g_ref[...]).astype(BF16)
        y_s[...] = jnp.zeros_like(y_s)
        cand_s[...] = jnp.full(cand_s.shape, NEG_INF, F32)
        for tok in halves:
            xn = xn_s[tok, :]
            for hd in range(PEER_HEADS):
                q1 = _dot(xn, wq_ref[:, (2 * hd) * N_KEYS:(2 * hd + 1) * N_KEYS]).astype(BF16)
                q2 = _dot(xn, wq_ref[:, (2 * hd + 1) * N_KEYS:(2 * hd + 2) * N_KEYS]).astype(BF16)
                s1 = _dot_nt(k1_ref[...], q1)
                s2 = _dot_nt(k2_ref[...], q2)
                _top16_of_keys(s1, v1_s)
                _top16_of_keys(s2, v2_s)
                fill_candidates()
                _top16_rows(cand_s[...], top_s)
                top = top_s[...]
                mx = top[0:1, :]
                shift = mx + jnp.log(jnp.sum(jnp.exp(top - mx), axis=0, keepdims=True))
                s1_s[hd, :, tok] = (s1 - shift) * LOG2E - 1.0
                s2_s[hd, :, tok] = s2 * LOG2E
                v1_s[...] = (v1_s[...] - shift) * LOG2E - 1.0
                v2_s[...] = v2_s[...] * LOG2E
                fill_candidates()
                _top16_rows(cand_s[...], top_s)
                tau_s[hd:hd + 1, tok] = top_s[PEER_TOPK - 1:PEER_TOPK, :]

    n_pieces = ce // PIECE

    def act_piece(tok, p):
        return _dot_nt(u_ref[p * PIECE:(p + 1) * PIECE, :], xn_s[tok, :])

    def mix_piece(tok, p, act):
        for aa in range(PIECE // N_KEYS):
            a = j * a_per_step + p * (PIECE // N_KEYS) + aa
            coef = jnp.zeros((N_KEYS, th), F32)
            for hd in range(PEER_HEADS):
                s = s2_s[hd, :, tok] + s1_s[hd, pl.ds(a, 1), tok]
                coef = coef + jnp.where(s >= tau_s[hd:hd + 1, tok], jnp.exp2(s), 0.0)
            x = act[aa * N_KEYS:(aa + 1) * N_KEYS, :]
            gelu2 = x * (1.0 + lax.erf(x * math.sqrt(0.5)))
            rows = slice(p * PIECE + aa * N_KEYS, p * PIECE + (aa + 1) * N_KEYS)
            w_s[rows, tok] = (coef * gelu2).astype(BF16)

    def out_piece(tok, p):
        return _dot(vt_ref[:, p * PIECE:(p + 1) * PIECE], w_s[p * PIECE:(p + 1) * PIECE, tok])

    acts = [act_piece(halves[0], p) for p in range(n_pieces)]
    for hi, tok in enumerate(halves):
        nxt = halves[hi + 1] if hi + 1 < len(halves) else None
        prv = halves[hi - 1] if hi > 0 else None
        nxt_acts, y_prev = [], None
        for p in range(n_pieces):
            mix_piece(tok, p, acts[p])
            if nxt is not None:
                nxt_acts.append(act_piece(nxt, p))
            if prv is not None:
                yp = out_piece(prv, p)
                y_prev = yp if y_prev is None else y_prev + yp
        if prv is not None:
            y_s[:, prv] += y_prev
        acts = nxt_acts
    y_last = out_piece(halves[-1], 0)
    for p in range(1, n_pieces):
        y_last = y_last + out_piece(halves[-1], p)
    y_s[:, halves[-1]] += y_last

    @pl.when(j == nj - 1)
    def _finish():
        y = h_ref[...] + y_s[...].T
        if fg_ref is not None:
            y = _rms(y, fg_ref[...])
        out_ref[...] = y


def _peer(h, g, wq, k1, k2, u, vt, final_g=None, tm=PEER_ROWS, ce=PEER_EXPERTS, th=PEER_HALF):
    t_tokens = h.shape[0]
    tm = min(tm, t_tokens)
    full = lambda shape: pl.BlockSpec(shape, lambda i, j: (0, 0))
    in_specs = [pl.BlockSpec((tm, D_MODEL), lambda i, j: (i, 0)), full((1, D_MODEL)),
                full((D_MODEL, 2 * PEER_HEADS * N_KEYS)), full((N_KEYS, N_KEYS)), full((N_KEYS, N_KEYS)),
                pl.BlockSpec((ce, D_MODEL), lambda i, j: (j, 0)),
                pl.BlockSpec((D_MODEL, ce), lambda i, j: (0, j))]
    args = [h, g, wq, k1, k2, u, vt]
    kern = _peer_kernel_plain
    if final_g is not None:
        in_specs.append(full((1, D_MODEL)))
        args.append(final_g)
        kern = _peer_kernel_final
    head_buf = pltpu.VMEM((PEER_HEADS, N_KEYS, tm), F32)
    return pl.pallas_call(
        functools.partial(kern, ce=ce, th=th),
        out_shape=jax.ShapeDtypeStruct((t_tokens, D_MODEL), F32),
        grid=(t_tokens // tm, N_EXPERTS // ce),
        in_specs=in_specs,
        out_specs=pl.BlockSpec((tm, D_MODEL), lambda i, j: (i, 0)),
        scratch_shapes=[pltpu.VMEM((tm, D_MODEL), BF16), head_buf, head_buf,
                        pltpu.VMEM((PEER_HEADS, tm), F32),
                        pltpu.VMEM((PEER_TOPK, th), F32), pltpu.VMEM((PEER_TOPK, th), F32),
                        pltpu.VMEM((_CAND_ROWS, th), F32), pltpu.VMEM((PEER_TOPK, th), F32),
                        pltpu.VMEM((D_MODEL, tm), F32), pltpu.VMEM((ce, tm), BF16)],
        compiler_params=_params(("parallel", "arbitrary")),
        name="peer_ffn",
    )(*args)


def _transpose_cast_kernel(x_ref, o_ref):
    o_ref[...] = x_ref[...].T.astype(BF16)


def _transpose_cast(x, layer, tr=TABLE_ROWS):
    _, rows, cols = x.shape
    return pl.pallas_call(
        _transpose_cast_kernel,
        out_shape=jax.ShapeDtypeStruct((cols, rows), BF16),
        grid=(rows // tr,),
        in_specs=[pl.BlockSpec((None, tr, cols), lambda i: (layer, i, 0))],
        out_specs=pl.BlockSpec((cols, tr), lambda i: (0, i)),
        compiler_params=_params(("parallel",)),
        name="transpose_cast",
    )(x)


def _cast_kernel(x_ref, o_ref):
    o_ref[...] = x_ref[...].astype(BF16)


def _cast_layer(x, layer, tr=2 * TABLE_ROWS):
    _, rows, cols = x.shape
    tr = min(tr, rows)
    return pl.pallas_call(
        _cast_kernel,
        out_shape=jax.ShapeDtypeStruct((rows, cols), BF16),
        grid=(rows // tr,),
        in_specs=[pl.BlockSpec((None, tr, cols), lambda i: (layer, i, 0))],
        out_specs=pl.BlockSpec((tr, cols), lambda i: (i, 0)),
        compiler_params=_params(("parallel",)),
        name="cast_layer",
    )(x)


_IN_SIZES = (WIDTH,) * 6 + (N_HEADS, WIDTH, C_KV_HEADS * HEAD_DIM, C_KV_HEADS * HEAD_DIM, 3 * D_MODEL)
_IN_STARTS = tuple(sum(_IN_SIZES[:k]) for k in range(len(_IN_SIZES)))
_C_HEAD_ORDER = (0, 3, 1, 4, 2, 5)


def _w_in_kernel(w_ref, watt_ref, wgate_ref):
    def piece(idx, offset=0, size=None):
        start = _IN_STARTS[idx] + offset
        return w_ref[:, start:start + (_IN_SIZES[idx] if size is None else size)].astype(BF16)

    col = 0
    for idx in (0, 1, 2, 3, 4, 5):
        watt_ref[:, col:col + WIDTH] = piece(idx)
        col += WIDTH
    for hd in _C_HEAD_ORDER:
        watt_ref[:, col:col + HEAD_DIM] = piece(7, hd * HEAD_DIM, HEAD_DIM)
        col += HEAD_DIM
    for idx in (8, 9):
        watt_ref[:, col:col + LANE] = piece(idx)
        col += LANE
    forget = piece(6)
    watt_ref[:, col:col + LANE] = jnp.concatenate(
        [forget, jnp.zeros((forget.shape[0], LANE - N_HEADS), BF16)], axis=1)
    wgate_ref[...] = piece(10)


def _w_in_layout(w_in, layer, tr=TABLE_ROWS // 2):
    _, rows, cols = w_in.shape
    att_cols = (N_QKV_BLOCKS + 1) * LANE
    return pl.pallas_call(
        _w_in_kernel,
        out_shape=(jax.ShapeDtypeStruct((rows, att_cols), BF16), jax.ShapeDtypeStruct((rows, 3 * D_MODEL), BF16)),
        grid=(rows // tr,),
        in_specs=[pl.BlockSpec((None, tr, cols), lambda i: (layer, i, 0))],
        out_specs=(pl.BlockSpec((tr, att_cols), lambda i: (i, 0)), pl.BlockSpec((tr, 3 * D_MODEL), lambda i: (i, 0))),
        compiler_params=_params(("parallel",)),
        name="w_in_layout",
    )(w_in)


def _rope_tables(positions):
    inv_freq = ROPE_THETA ** (-jnp.arange(0, HEAD_DIM, 2, dtype=F32) / HEAD_DIM)
    ang = positions.astype(F32).reshape(-1, 1) * inv_freq
    cos, sin = jnp.cos(ang), jnp.sin(ang)
    return jnp.concatenate([cos] * 4, axis=-1), jnp.concatenate([-sin, sin, -sin, sin], axis=-1)


def _permute_heads(w, axis):
    heads = [lax.slice_in_dim(w, hd * HEAD_DIM, (hd + 1) * HEAD_DIM, axis=axis) for hd in _C_HEAD_ORDER]
    return jnp.concatenate(heads, axis=axis)


def _mixer(h, cos4, sin4, batch, norm_g, w_att, wg, b_forget, sinks, b_gate, w_br_a, w_br_b, w_br_c, w_out):
    t_tokens = h.shape[0]
    seq = t_tokens // batch
    qkv, f, *qkv_classes = _in_proj(h, norm_g.reshape(1, -1), w_att, cos4, sin4, batch)
    qkv3 = qkv.reshape(batch, seq, -1)

    o1, l1 = _banded(qkv3, COL_QA, COL_KA, COL_VA, A_MAX_DIST, kv_shared=False)
    class_outs, class_lses = [], []
    for dil, x in zip(A_DILATIONS[1:], qkv_classes):
        o, lse = _banded(x.reshape(batch * dil, seq // dil, -1), COL_QA, COL_KA, COL_VA, A_MAX_DIST, kv_shared=False)
        class_outs.append(o.reshape(batch, dil, seq // dil, WIDTH))
        class_lses.append(lse.reshape(batch, dil, seq // dil, WIDTH))
    oa = _combine(o1.reshape(t_tokens, WIDTH), l1.reshape(t_tokens, WIDTH), class_outs, class_lses, batch)

    sink_tab = jnp.stack([sinks[hd] for hd in _C_HEAD_ORDER]).astype(F32)
    (oc,) = _banded(qkv3, COL_QC, COL_KC, COL_VC, C_MAX_DIST, kv_shared=True, sink=sink_tab, want_lse=False)
    oc = oc.reshape(t_tokens, WIDTH)

    bf_row = jnp.pad(b_forget.astype(F32), (0, LANE - N_HEADS)).reshape(1, LANE)
    augq, augk = _cum_aug(f.reshape(batch, seq, LANE), bf_row)
    ob = _fox(qkv3, augq, augk).reshape(t_tokens, WIDTH)

    return _merge(h, norm_g.reshape(1, -1), oa, ob, oc, wg, b_gate.reshape(1, -1).astype(F32),
                  w_br_a.astype(BF16), w_br_b.astype(BF16), _permute_heads(w_br_c, 0).astype(BF16),
                  w_out.astype(BF16))


def kernel(x, positions, norm1_g, w_in, b_forget, sinks, b_gate, w_br_a, w_br_b, w_br_c, w_out, norm2_g,
           peer_wq, peer_k1, peer_k2, peer_u, peer_v, final_g):
    batch, seq, d_model = x.shape
    depth = w_in.shape[0]
    assert d_model == D_MODEL and seq % max(COMBINE_ROWS, CUM_ROWS) == 0 and seq // A_DILATIONS[-1] >= BLOCK
    assert w_in.shape[2] == sum(_IN_SIZES) and peer_u.shape[1:] == (N_EXPERTS, D_MODEL)
    cos4, sin4 = _rope_tables(positions)
    h = x.reshape(batch * seq, D_MODEL)
    for l in range(depth):
        w_att, w_gate = _w_in_layout(w_in, l)
        h = _mixer(h, cos4, sin4, batch, norm1_g[l], w_att, w_gate, b_forget[l], sinks[l], b_gate[l],
                   w_br_a[l], w_br_b[l], w_br_c[l], w_out[l])
        h = _peer(h, norm2_g[l].reshape(1, -1), _cast_layer(peer_wq, l), peer_k1[l].astype(BF16),
                  peer_k2[l].astype(BF16), _cast_layer(peer_u, l), _transpose_cast(peer_v, l),
                  final_g=final_g.reshape(1, -1) if l == depth - 1 else None)
    return h.reshape(batch, seq, D_MODEL)
```

```python
import functools
import math

import jax
import jax.numpy as jnp
from jax import lax
from jax.experimental import pallas as pl
from jax.experimental.pallas import tpu as pltpu

F32 = jnp.float32
BF16 = jnp.bfloat16

D_MODEL = 1024
HEAD_DIM = 64
N_HEADS = 6
N_PAIRS = N_HEADS // 2
C_KV_HEADS = 2
A_DILATIONS = (1, 4, 16)
A_MAX_DIST = 128
C_MAX_DIST = 127
BLOCK = 128
LANE = 128
ROPE_THETA = 10000.0
RMS_EPS = 1e-6
NEG_INF = -1e30
PEER_HEADS = 8
N_KEYS = 128
N_EXPERTS = N_KEYS * N_KEYS
PEER_TOPK = 16
WIDTH = N_HEADS * HEAD_DIM

COL_QA, COL_KA, COL_VA = 0, 3, 6
COL_QB, COL_KB, COL_VB = 9, 12, 15
COL_QC, COL_KC, COL_VC = 18, 21, 22
N_QKV_BLOCKS = 23
COL_F = 23
A_BLOCKS = 9
ROPE_BLOCKS = (0, 1, 2, 3, 4, 5, 18, 19, 20, 21)
QUERY_BLOCKS = (0, 1, 2, 9, 10, 11, 18, 19, 20)
QK_SCALE = HEAD_DIM ** -0.5
LOG2E = math.log2(math.e)

VMEM_LIMIT = 56 * 1024 * 1024

IN_PROJ_ROWS = 512
BANDED_QUERY_BLOCKS = 8
COMBINE_ROWS = 1024
CUM_ROWS = 2048
FOX_QUERIES = 512
FOX_KEYS = 512
MERGE_ROWS = 512
PEER_ROWS = 512
PEER_EXPERTS = 2048
PEER_HALF = 256
TABLE_ROWS = 512


def _params(sem):
    return pltpu.CompilerParams(dimension_semantics=sem, vmem_limit_bytes=VMEM_LIMIT)


def _dot(a, b):
    return jnp.dot(a, b, preferred_element_type=F32)


def _dot_nt(a, b):
    return lax.dot_general(a, b, (((1,), (1,)), ((), ())), preferred_element_type=F32)


def _rms(x, g):
    var = jnp.mean(x * x, axis=-1, keepdims=True)
    return x * lax.rsqrt(var + RMS_EPS) * g


def _in_proj_kernel(h_ref, g_ref, w_ref, cos_ref, sin_ref, qkv_ref, f_ref, *rest):
    class_refs, proj_s, fin_s = rest[:-2], rest[-2], rest[-1]
    tm = h_ref.shape[0]
    xn = _rms(h_ref[...], g_ref[...]).astype(BF16)
    proj_s[...] = _dot(xn, w_ref[...])
    cos = cos_ref[...]
    sin = sin_ref[...]
    lane = lax.broadcasted_iota(jnp.int32, cos.shape, 1)
    first_half = (lane % HEAD_DIM) < (HEAD_DIM // 2)
    for c in range(N_QKV_BLOCKS):
        t = proj_s[:, c * LANE:(c + 1) * LANE]
        if c in ROPE_BLOCKS:
            rot = jnp.where(first_half, pltpu.roll(t, LANE - HEAD_DIM // 2, 1), pltpu.roll(t, HEAD_DIM // 2, 1))
            t = t * cos + rot * sin
        if c in QUERY_BLOCKS:
            t = t * QK_SCALE
        if c < A_BLOCKS:
            fin_s[c] = t
        qkv_ref[:, c * LANE:(c + 1) * LANE] = t.astype(BF16)
    f_ref[...] = proj_s[:, COL_F * LANE:(COL_F + 1) * LANE]
    for dil, ref in zip(A_DILATIONS[1:], class_refs):
        for c in range(dil):
            for blk in range(A_BLOCKS):
                rows = fin_s[blk, pl.ds(c, tm // dil, stride=dil), :]
                ref[c, :, blk * LANE:(blk + 1) * LANE] = rows.astype(BF16)


def _in_proj(h, g, w_att, cos4, sin4, batch, tm=IN_PROJ_ROWS):
    t_tokens = h.shape[0]
    seq = t_tokens // batch
    tiles = seq // tm
    n_cols = w_att.shape[1]
    a_cols = A_BLOCKS * LANE
    class_shapes = tuple(jax.ShapeDtypeStruct((batch, dil, seq // dil, a_cols), BF16) for dil in A_DILATIONS[1:])
    class_specs = tuple(pl.BlockSpec((None, dil, tm // dil, a_cols), lambda i: (i // tiles, 0, i % tiles, 0))
                        for dil in A_DILATIONS[1:])
    return pl.pallas_call(
        _in_proj_kernel,
        out_shape=(jax.ShapeDtypeStruct((t_tokens, N_QKV_BLOCKS * LANE), BF16),
                   jax.ShapeDtypeStruct((t_tokens, LANE), F32)) + class_shapes,
        grid=(t_tokens // tm,),
        in_specs=[pl.BlockSpec((tm, D_MODEL), lambda i: (i, 0)),
                  pl.BlockSpec((1, D_MODEL), lambda i: (0, 0)),
                  pl.BlockSpec((D_MODEL, n_cols), lambda i: (0, 0)),
                  pl.BlockSpec((tm, LANE), lambda i: (i, 0)),
                  pl.BlockSpec((tm, LANE), lambda i: (i, 0))],
        out_specs=(pl.BlockSpec((tm, N_QKV_BLOCKS * LANE), lambda i: (i, 0)),
                   pl.BlockSpec((tm, LANE), lambda i: (i, 0))) + class_specs,
        scratch_shapes=[pltpu.VMEM((tm, n_cols), F32), pltpu.VMEM((A_BLOCKS, tm, LANE), F32)],
        compiler_params=_params(("parallel",)),
        name="in_proj",
    )(h, g, w_att, cos4, sin4)


def _banded_kernel(*refs, max_dist, qb, kv_shared, has_sink, want_lse):
    refs = list(refs)
    q_ref, kp_ref, kc_ref, vp_ref, vc_ref = refs[:5]
    pos = 5
    sink_ref = None
    if has_sink:
        sink_ref = refs[pos]
        pos += 1
    o_ref = refs[pos]
    pos += 1
    lse_ref = None
    if want_lse:
        lse_ref = refs[pos]
        pos += 1
    kwin, vwin = refs[pos], refs[pos + 1]

    i = pl.program_id(1)
    kwin[0:BLOCK, :] = kp_ref[...]
    kwin[BLOCK:, :] = kc_ref[...]
    vwin[0:BLOCK, :] = vp_ref[...]
    vwin[BLOCK:, :] = vc_ref[...]

    row = lax.broadcasted_iota(jnp.int32, (BLOCK, 2 * BLOCK), 0)
    col = lax.broadcasted_iota(jnp.int32, (BLOCK, 2 * BLOCK), 1)
    dist = row + BLOCK - col
    band = (dist >= 0) & (dist <= max_dist)
    lane = lax.broadcasted_iota(jnp.int32, (1, LANE), 1)
    lower = lane < HEAD_DIM

    def body(sb, carry):
        off = pl.multiple_of(sb * BLOCK, BLOCK)
        valid = band & ((col >= BLOCK) | (i * qb + sb > 0))
        for p in range(N_PAIRS):
            q = q_ref[pl.ds(off, BLOCK), p * LANE:(p + 1) * LANE]
            kv_cols = slice(0, LANE) if kv_shared else slice(p * LANE, (p + 1) * LANE)
            k = kwin[pl.ds(off, 2 * BLOCK), kv_cols]
            v = vwin[pl.ds(off, 2 * BLOCK), kv_cols]
            o_pair = jnp.zeros((BLOCK, LANE), F32)
            lse_pair = jnp.zeros((BLOCK, LANE), F32)
            for hf in range(2):
                sel = lower if hf == 0 else jnp.logical_not(lower)
                qm = jnp.where(sel, q, jnp.zeros_like(q))
                s = _dot_nt(qm, k)
                s = jnp.where(valid, s, NEG_INF)
                m = jnp.max(s, axis=-1, keepdims=True)
                if has_sink:
                    sk = sink_ref[2 * p + hf]
                    m = jnp.maximum(m, sk)
                pr = jnp.exp(s - m)
                den = jnp.sum(pr, axis=-1, keepdims=True)
                if has_sink:
                    den = den + jnp.exp(sk - m)
                o = _dot(pr.astype(BF16), v) / den
                o_pair = jnp.where(sel, o, o_pair)
                if want_lse:
                    lse_pair = jnp.where(sel, m + jnp.log(den), lse_pair)
            o_ref[pl.ds(off, BLOCK), p * LANE:(p + 1) * LANE] = o_pair.astype(BF16)
            if want_lse:
                lse_ref[pl.ds(off, BLOCK), p * LANE:(p + 1) * LANE] = lse_pair
        return carry

    lax.fori_loop(0, qb, body, 0, unroll=True)


def _banded(x, qcol, kcol, vcol, max_dist, kv_shared, sink=None, want_lse=True, qb=BANDED_QUERY_BLOCKS):
    n, length, _ = x.shape
    qb = min(qb, length // BLOCK)
    tq = qb * BLOCK
    kvw = LANE if kv_shared else WIDTH
    kblk = kcol if kv_shared else kcol // N_PAIRS
    vblk = vcol if kv_shared else vcol // N_PAIRS
    in_specs = [
        pl.BlockSpec((None, tq, WIDTH), lambda b, i: (b, i, qcol // N_PAIRS)),
        pl.BlockSpec((None, BLOCK, kvw), lambda b, i: (b, jnp.maximum(i * qb - 1, 0), kblk)),
        pl.BlockSpec((None, tq, kvw), lambda b, i: (b, i, kblk)),
        pl.BlockSpec((None, BLOCK, kvw), lambda b, i: (b, jnp.maximum(i * qb - 1, 0), vblk)),
        pl.BlockSpec((None, tq, kvw), lambda b, i: (b, i, vblk)),
    ]
    args = [x, x, x, x, x]
    if sink is not None:
        in_specs.append(pl.BlockSpec(memory_space=pltpu.SMEM))
        args.append(sink)
    out_shape = [jax.ShapeDtypeStruct((n, length, WIDTH), BF16)]
    out_specs = [pl.BlockSpec((None, tq, WIDTH), lambda b, i: (b, i, 0))]
    if want_lse:
        out_shape.append(jax.ShapeDtypeStruct((n, length, WIDTH), F32))
        out_specs.append(pl.BlockSpec((None, tq, WIDTH), lambda b, i: (b, i, 0)))
    kern = functools.partial(_banded_kernel, max_dist=max_dist, qb=qb, kv_shared=kv_shared,
                             has_sink=sink is not None, want_lse=want_lse)
    return pl.pallas_call(
        kern,
        out_shape=tuple(out_shape),
        grid=(n, length // tq),
        in_specs=in_specs,
        out_specs=tuple(out_specs),
        scratch_shapes=[pltpu.VMEM((tq + BLOCK, kvw), BF16), pltpu.VMEM((tq + BLOCK, kvw), BF16)],
        compiler_params=_params(("parallel", "parallel")),
        name="banded_attention",
    )(*args)


def _combine_kernel(o1_ref, l1_ref, *rest):
    n_cls = len(A_DILATIONS) - 1
    o_refs, l_refs = rest[:n_cls], rest[n_cls:2 * n_cls]
    out_ref = rest[2 * n_cls]
    o_s, l_s = rest[2 * n_cls + 1:3 * n_cls + 1], rest[3 * n_cls + 1:]
    tm = o1_ref.shape[0]
    for dil, o_ref, l_ref, os_, ls_ in zip(A_DILATIONS[1:], o_refs, l_refs, o_s, l_s):
        for c in range(dil):
            for p in range(N_PAIRS):
                cols = slice(p * LANE, (p + 1) * LANE)
                os_[p, pl.ds(c, tm // dil, stride=dil), :] = o_ref[c, :, cols].astype(F32)
                ls_[p, pl.ds(c, tm // dil, stride=dil), :] = l_ref[c, :, cols]
    for p in range(N_PAIRS):
        cols = slice(p * LANE, (p + 1) * LANE)
        lses = [l1_ref[:, cols]] + [ls_[p] for ls_ in l_s]
        outs = [o1_ref[:, cols].astype(F32)] + [os_[p] for os_ in o_s]
        m = functools.reduce(jnp.maximum, lses)
        ws = [jnp.exp(l - m) for l in lses]
        num = functools.reduce(lambda a, b: a + b, [w * o for w, o in zip(ws, outs)])
        out_ref[:, cols] = (num / functools.reduce(lambda a, b: a + b, ws)).astype(BF16)


def _combine(o1, l1, class_outs, class_lses, batch, tm=COMBINE_ROWS):
    t_tokens = o1.shape[0]
    tiles = t_tokens // batch // tm
    tok = pl.BlockSpec((tm, WIDTH), lambda i: (i, 0))
    cls = [pl.BlockSpec((None, dil, tm // dil, WIDTH), lambda i: (i // tiles, 0, i % tiles, 0))
           for dil in A_DILATIONS[1:]]
    n_cls = len(cls)
    return pl.pallas_call(
        _combine_kernel,
        out_shape=jax.ShapeDtypeStruct((t_tokens, WIDTH), BF16),
        grid=(t_tokens // tm,),
        in_specs=[tok, tok] + cls + cls,
        out_specs=tok,
        scratch_shapes=[pltpu.VMEM((N_PAIRS, tm, LANE), F32)] * (2 * n_cls),
        compiler_params=_params(("parallel",)),
        name="combine_patterns",
    )(o1, l1, *class_outs, *class_lses)


CUM_BLOCK = 256
BF16_ROWS = 16
UNDERFLOW = 104.0
NORM_SLACK = 1.02


def _split3(x):
    p1 = x.astype(BF16)
    rest = x - p1.astype(F32)
    p2 = rest.astype(BF16)
    p3 = (rest - p2.astype(F32)).astype(BF16)
    return p1, p2, p3


def _cum_kernel(f_ref, bf_ref, augq_ref, augk_ref, carry_s):
    seq = f_ref.shape[0]

    @pl.when(pl.program_id(1) == 0)
    def _():
        carry_s[...] = jnp.zeros_like(carry_s)

    rr = lax.broadcasted_iota(jnp.int32, (CUM_BLOCK, CUM_BLOCK), 0)
    cc = lax.broadcasted_iota(jnp.int32, (CUM_BLOCK, CUM_BLOCK), 1)
    tril = jnp.where(rr >= cc, 1.0, 0.0).astype(BF16)
    n_out = 2 * N_HEADS * LANE
    pr = lax.broadcasted_iota(jnp.int32, (LANE, n_out), 0)
    pc = lax.broadcasted_iota(jnp.int32, (LANE, n_out), 1)
    same_head = pr == pc // (2 * LANE)
    k_side = (pc // LANE) % 2 == 1
    lane64 = pc % HEAD_DIM
    places = [jnp.where(same_head & jnp.logical_not(k_side) & (lane64 == i), 1.0,
                        jnp.where(same_head & k_side & (lane64 == 3 + i), -1.0, 0.0)).astype(BF16)
              for i in range(3)]
    c1d = lax.broadcasted_iota(jnp.int32, (1, n_out), 1)
    c_k = (c1d // LANE) % 2 == 1
    c_l = c1d % HEAD_DIM
    ones_at = (c_k & (c_l < 3)) | (jnp.logical_not(c_k) & (c_l >= 3) & (c_l < 6))
    const = jnp.where(ones_at, 1.0, 0.0).astype(F32)
    bias = bf_ref[...]

    def body(blk, carry):
        off = pl.multiple_of(blk * CUM_BLOCK, CUM_BLOCK)
        x = f_ref[pl.ds(off, CUM_BLOCK), :] + bias
        log_f = jnp.minimum(x, 0.0) - jnp.log1p(jnp.exp(-jnp.abs(x)))
        cum = sum(_dot(tril, piece) for piece in _split3(log_f)) + carry
        aug = sum(_dot(piece, place) for piece, place in zip(_split3(cum), places)) + const
        for head in range(N_HEADS):
            base = head * 2 * LANE
            augq_ref[head, pl.ds(off, CUM_BLOCK), :] = aug[:, base:base + LANE].astype(BF16)
            augk_ref[head, pl.ds(off, CUM_BLOCK), :] = aug[:, base + LANE:base + 2 * LANE].astype(BF16)
        return cum[CUM_BLOCK - 1:CUM_BLOCK, :]

    carry_s[...] = lax.fori_loop(0, seq // CUM_BLOCK, body, carry_s[...])


def _cum_aug(f, b_forget_row, ts=CUM_ROWS):
    b, seq, _ = f.shape
    ts = min(ts, seq)
    out = jax.ShapeDtypeStruct((b, N_HEADS, seq, LANE), BF16)
    ospec = pl.BlockSpec((None, N_HEADS, ts, LANE), lambda bi, si: (bi, 0, si, 0))
    return pl.pallas_call(
        _cum_kernel,
        out_shape=(out, out),
        grid=(b, seq // ts),
        in_specs=[pl.BlockSpec((None, ts, LANE), lambda bi, si: (bi, si, 0)),
                  pl.BlockSpec((1, LANE), lambda bi, si: (0, 0))],
        out_specs=(ospec, ospec),
        scratch_shapes=[pltpu.VMEM((1, LANE), F32)],
        compiler_params=_params(("parallel", "arbitrary")),
        name="forget_cumsum",
    )(f, b_forget_row)


def _fox_kernel(q_ref, k_ref, v_ref, aq_ref, ak_ref, o_ref, kaug, qaug, kstat, *, tq, tk):
    qi = pl.program_id(2)
    n_kb = k_ref.shape[0] // tk
    n_sub = tq // tk
    lane = lax.broadcasted_iota(jnp.int32, (1, LANE), 1)
    lower = lane < HEAD_DIM
    hr = lax.broadcasted_iota(jnp.int32, (LANE, LANE), 0) // HEAD_DIM
    hc = lax.broadcasted_iota(jnp.int32, (LANE, LANE), 1) // HEAD_DIM
    same_head = jnp.where(hr == hc, 1.0, 0.0).astype(BF16)

    def max_sq_norm(x):
        xf = x.astype(F32)
        return jnp.max(_dot((xf * xf).astype(BF16), same_head), axis=0, keepdims=True)

    def lane_sum(row_vec, lo, hi):
        return jnp.sum(jnp.where((lane >= lo) & (lane < hi), row_vec.astype(F32), 0.0), axis=1, keepdims=True)

    @pl.when(qi == 0)
    def _():
        k = k_ref[...]
        kaug[0] = jnp.where(lower, k, ak_ref[0])
        kaug[1] = jnp.where(lower, ak_ref[1], k)

        def stats(kb, carry):
            kn0, kn1, nc0, nc1 = carry
            off = pl.multiple_of(kb * tk, tk)
            n2 = max_sq_norm(k_ref[pl.ds(off, tk), :])
            tail = pl.multiple_of(off + tk - BF16_ROWS, BF16_ROWS)
            here = lane == kb
            kn0 = jnp.where(here, n2[:, 0:1], kn0)
            kn1 = jnp.where(here, n2[:, HEAD_DIM:HEAD_DIM + 1], kn1)
            last0 = ak_ref[0, pl.ds(tail, BF16_ROWS), :][BF16_ROWS - 1:BF16_ROWS, :]
            last1 = ak_ref[1, pl.ds(tail, BF16_ROWS), :][BF16_ROWS - 1:BF16_ROWS, :]
            nc0 = jnp.where(here, lane_sum(last0, 3, 6), nc0)
            nc1 = jnp.where(here, lane_sum(last1, 3, 6), nc1)
            return kn0, kn1, nc0, nc1

        zero = jnp.zeros((1, LANE), F32)
        kn0, kn1, nc0, nc1 = lax.fori_loop(0, n_kb, stats, (zero, zero, zero, zero))
        kstat[0:1, :] = kn0
        kstat[1:2, :] = kn1
        kstat[2:3, :] = nc0
        kstat[3:4, :] = nc1

    q = q_ref[...]
    qaug[0] = jnp.where(lower, q, aq_ref[0])
    qaug[1] = jnp.where(lower, aq_ref[1], q)
    row = lax.broadcasted_iota(jnp.int32, (tq, tk), 0)
    col = lax.broadcasted_iota(jnp.int32, (tq, tk), 1)

    qn = max_sq_norm(q)
    first = None
    for hf in range(2):
        qn2 = qn[:, hf * HEAD_DIM:hf * HEAD_DIM + 1]
        kn2 = kstat[hf:hf + 1, :]
        kn2_all = jnp.max(kn2, axis=1, keepdims=True)
        cum_q = lane_sum(aq_ref[hf, 0:1, :], 0, 3)
        upper = jnp.sqrt(qn2 * kn2) * NORM_SLACK + cum_q + kstat[2 + hf:3 + hf, :] + 1.0
        floor = -jnp.sqrt(qn2 * kn2_all) * NORM_SLACK
        needed = (upper >= floor - UNDERFLOW) & (lane < n_kb)
        first_h = jnp.min(jnp.where(needed, lane.astype(F32), float(n_kb)))
        first = first_h if first is None else jnp.minimum(first, first_h)
    kb_start = jnp.minimum(first.astype(jnp.int32), qi * n_sub)

    def step(kb, carry, mask_shift):
        off = pl.multiple_of(kb * tk, tk)
        v = v_ref[pl.ds(off, tk), :]
        new = []
        for hf in range(2):
            m, l, acc = carry[hf]
            s = _dot_nt(qaug[hf], kaug[hf, pl.ds(off, tk), :])
            if mask_shift is not None:
                s = jnp.where(col + mask_shift <= row, s, NEG_INF)
            m_new = jnp.maximum(m, jnp.max(s, axis=-1, keepdims=True))
            alpha = jnp.exp(m - m_new)
            pr = jnp.exp(s - m_new)
            l = alpha * l + jnp.sum(pr, axis=-1, keepdims=True)
            acc = alpha * acc + _dot(pr.astype(BF16), v)
            new.append((m_new, l, acc))
        return tuple(new)

    one = (jnp.full((tq, 1), NEG_INF, F32), jnp.zeros((tq, 1), F32), jnp.zeros((tq, LANE), F32))
    carry = lax.fori_loop(kb_start, qi * n_sub, functools.partial(step, mask_shift=None), (one, one))
    for d in range(n_sub):
        carry = step(qi * n_sub + d, carry, d * tk)
    (_, l0, acc0), (_, l1, acc1) = carry
    o_ref[...] = jnp.where(lower, acc0 / l0, acc1 / l1).astype(BF16)


def _fox(qkv, augq, augk, tq=FOX_QUERIES, tk=FOX_KEYS):
    b, seq, _ = qkv.shape
    return pl.pallas_call(
        functools.partial(_fox_kernel, tq=tq, tk=tk),
        out_shape=jax.ShapeDtypeStruct((b, seq, WIDTH), BF16),
        grid=(b, N_PAIRS, seq // tq),
        in_specs=[pl.BlockSpec((None, tq, LANE), lambda bi, p, i: (bi, i, COL_QB + p)),
                  pl.BlockSpec((None, seq, LANE), lambda bi, p, i: (bi, 0, COL_KB + p)),
                  pl.BlockSpec((None, seq, LANE), lambda bi, p, i: (bi, 0, COL_VB + p)),
                  pl.BlockSpec((None, 2, tq, LANE), lambda bi, p, i: (bi, p, i, 0)),
                  pl.BlockSpec((None, 2, seq, LANE), lambda bi, p, i: (bi, p, 0, 0))],
        out_specs=pl.BlockSpec((None, tq, LANE), lambda bi, p, i: (bi, i, p)),
        scratch_shapes=[pltpu.VMEM((2, seq, LANE), BF16), pltpu.VMEM((2, tq, LANE), BF16),
                        pltpu.VMEM((8, LANE), F32)],
        compiler_params=_params(("parallel", "parallel", "arbitrary")),
        name="forgetting_attention",
    )(qkv, qkv, qkv, augq, augk)


def _merge_kernel(h_ref, g_ref, oa_ref, ob_ref, oc_ref, wg_ref, bg_ref, wa_ref, wb_ref, wc_ref, wo_ref, out_ref):
    h = h_ref[...]
    xn = _rms(h, g_ref[...]).astype(BF16)
    merged = jnp.zeros(h.shape, F32)
    for idx, (o_ref, w_ref) in enumerate(((oa_ref, wa_ref), (ob_ref, wb_ref), (oc_ref, wc_ref))):
        cols = slice(idx * D_MODEL, (idx + 1) * D_MODEL)
        gate = jax.nn.sigmoid(_dot(xn, wg_ref[:, cols]) + bg_ref[:, cols])
        merged = merged + gate * _dot(o_ref[...], w_ref[...])
    out_ref[...] = h + _dot(merged.astype(BF16), wo_ref[...])


def _merge(h, g, oa, ob, oc, w_gate, b_gate, w_a, w_b, w_c, w_out, tm=MERGE_ROWS):
    t_tokens = h.shape[0]
    full = lambda shape: pl.BlockSpec(shape, lambda i: (0, 0))
    ospec = pl.BlockSpec((tm, WIDTH), lambda i: (i, 0))
    return pl.pallas_call(
        _merge_kernel,
        out_shape=jax.ShapeDtypeStruct((t_tokens, D_MODEL), F32),
        grid=(t_tokens // tm,),
        in_specs=[pl.BlockSpec((tm, D_MODEL), lambda i: (i, 0)), full((1, D_MODEL)),
                  ospec, ospec, ospec,
                  full((D_MODEL, 3 * D_MODEL)), full((1, 3 * D_MODEL)),
                  full((WIDTH, D_MODEL)), full((WIDTH, D_MODEL)), full((WIDTH, D_MODEL)),
                  full((D_MODEL, D_MODEL))],
        out_specs=pl.BlockSpec((tm, D_MODEL), lambda i: (i, 0)),
        compiler_params=_params(("parallel",)),
        name="gated_merge",
    )(h, g, oa, ob, oc, w_gate, b_gate, w_a, w_b, w_c, w_out)


_CAND_COUNTS = tuple(PEER_TOPK // (i + 1) for i in range(PEER_TOPK))
_CAND_ROWS = 64
PIECE = 256


def _sort_pairs(n):
    out, p = [], 1
    while p < n:
        k = p
        while k >= 1:
            for j in range(k % p, n - k, 2 * k):
                for i in range(min(k, n - j - k)):
                    if (i + j) // (2 * p) == (i + j + k) // (2 * p):
                        out.append((i + j, i + j + k))
            k //= 2
        p *= 2
    return tuple(out)


SUBLANES = 8


def _top16_by_networks(src, dst_ref):
    n = src.shape[0] // SUBLANES
    rows = [src[SUBLANES * k:SUBLANES * (k + 1), :] for k in range(n)]

    def exchange(i, j):
        rows[i], rows[j] = jnp.maximum(rows[i], rows[j]), jnp.minimum(rows[i], rows[j])

    def sort_bitonic16():
        for d in (8, 4, 2, 1):
            for k in range(PEER_TOPK):
                if k & d == 0:
                    exchange(k, k + d)

    for i, j in _sort_pairs(n):
        exchange(i, j)
    for shift in (4, 2, 1):
        other = [pltpu.roll(r, shift, 0) for r in rows]
        if len(rows) < PEER_TOPK:
            rows = rows + other[::-1]
        else:
            rows = [jnp.maximum(rows[k], other[PEER_TOPK - 1 - k]) for k in range(PEER_TOPK)]
        sort_bitonic16()
    for r in range(PEER_TOPK):
        dst_ref[r:r + 1, :] = rows[r][0:1, :]


def _peer_kernel_plain(h_ref, g_ref, wq_ref, k1_ref, k2_ref, u_ref, vt_ref, out_ref, *scratch, ce, th):
    _peer_body(h_ref, g_ref, wq_ref, k1_ref, k2_ref, u_ref, vt_ref, None, out_ref, *scratch, ce=ce, th=th)


def _peer_kernel_final(h_ref, g_ref, wq_ref, k1_ref, k2_ref, u_ref, vt_ref, fg_ref, out_ref, *scratch, ce, th):
    _peer_body(h_ref, g_ref, wq_ref, k1_ref, k2_ref, u_ref, vt_ref, fg_ref, out_ref, *scratch, ce=ce, th=th)


def _peer_body(h_ref, g_ref, wq_ref, k1_ref, k2_ref, u_ref, vt_ref, fg_ref, out_ref,
               xn_s, s1_s, s2_s, tau_s, v1_s, v2_s, cand_s, top_s, y_s, w_s, *, ce, th):
    j = pl.program_id(1)
    nj = pl.num_programs(1)
    tm = h_ref.shape[0]
    a_per_step = ce // N_KEYS
    halves = [slice(t0, t0 + th) for t0 in range(0, tm, th)]

    def fill_candidates():
        off = 0
        for i, cnt in enumerate(_CAND_COUNTS):
            cand_s[off:off + cnt, :] = v1_s[i:i + 1, :] + v2_s[0:cnt, :]
            off += cnt

    @pl.when(j == 0)
    def _route():
        xn_s[...] = _rms(h_ref[...], g_ref[...]).astype(BF16)
        y_s[...] = jnp.zeros_like(y_s)
        cand_s[...] = jnp.full(cand_s.shape, NEG_INF, F32)
        for tok in halves:
            xn = xn_s[tok, :]
            for hd in range(PEER_HEADS):
                q1 = _dot(xn, wq_ref[:, (2 * hd) * N_KEYS:(2 * hd + 1) * N_KEYS]).astype(BF16)
                q2 = _dot(xn, wq_ref[:, (2 * hd + 1) * N_KEYS:(2 * hd + 2) * N_KEYS]).astype(BF16)
                s1 = _dot_nt(k1_ref[...], q1)
                s2 = _dot_nt(k2_ref[...], q2)
                _top16_by_networks(s1, v1_s)
                _top16_by_networks(s2, v2_s)
                fill_candidates()
                _top16_by_networks(cand_s[...], top_s)
                top = top_s[...]
                mx = top[0:1, :]
                shift = mx + jnp.log(jnp.sum(jnp.exp(top - mx), axis=0, keepdims=True))
                s1_s[hd, :, tok] = (s1 - shift) * LOG2E - 1.0
                s2_s[hd, :, tok] = s2 * LOG2E
                v1_s[...] = (v1_s[...] - shift) * LOG2E - 1.0
                v2_s[...] = v2_s[...] * LOG2E
                fill_candidates()
                _top16_by_networks(cand_s[...], top_s)
                tau_s[hd:hd + 1, tok] = top_s[PEER_TOPK - 1:PEER_TOPK, :]

    n_pieces = ce // PIECE

    def act_piece(tok, p):
        return _dot_nt(u_ref[p * PIECE:(p + 1) * PIECE, :], xn_s[tok, :])

    def mix_piece(tok, p, act):
        for aa in range(PIECE // N_KEYS):
            a = j * a_per_step + p * (PIECE // N_KEYS) + aa
            coef = jnp.zeros((N_KEYS, th), F32)
            for hd in range(PEER_HEADS):
                s = s2_s[hd, :, tok] + s1_s[hd, pl.ds(a, 1), tok]
                coef = coef + jnp.where(s >= tau_s[hd:hd + 1, tok], jnp.exp2(s), 0.0)
            x = act[aa * N_KEYS:(aa + 1) * N_KEYS, :]
            gelu2 = x * (1.0 + lax.erf(x * math.sqrt(0.5)))
            rows = slice(p * PIECE + aa * N_KEYS, p * PIECE + (aa + 1) * N_KEYS)
            w_s[rows, tok] = (coef * gelu2).astype(BF16)

    def out_piece(tok, p):
        return _dot(vt_ref[:, p * PIECE:(p + 1) * PIECE], w_s[p * PIECE:(p + 1) * PIECE, tok])

    acts = [act_piece(halves[0], p) for p in range(n_pieces)]
    for hi, tok in enumerate(halves):
        nxt = halves[hi + 1] if hi + 1 < len(halves) else None
        prv = halves[hi - 1] if hi > 0 else None
        nxt_acts, y_prev = [], None
        for p in range(n_pieces):
            mix_piece(tok, p, acts[p])
            if nxt is not None:
                nxt_acts.append(act_piece(nxt, p))
            if prv is not None:
                yp = out_piece(prv, p)
                y_prev = yp if y_prev is None else y_prev + yp
        if prv is not None:
            y_s[:, prv] += y_prev
        acts = nxt_acts
    y_last = out_piece(halves[-1], 0)
    for p in range(1, n_pieces):
        y_last = y_last + out_piece(halves[-1], p)
    y_s[:, halves[-1]] += y_last

    @pl.when(j == nj - 1)
    def _finish():
        y = h_ref[...] + y_s[...].T
        if fg_ref is not None:
            y = _rms(y, fg_ref[...])
        out_ref[...] = y


def _peer(h, g, wq, k1, k2, u, vt, final_g=None, tm=PEER_ROWS, ce=PEER_EXPERTS, th=PEER_HALF):
    t_tokens = h.shape[0]
    tm = min(tm, t_tokens)
    full = lambda shape: pl.BlockSpec(shape, lambda i, j: (0, 0))
    in_specs = [pl.BlockSpec((tm, D_MODEL), lambda i, j: (i, 0)), full((1, D_MODEL)),
                full((D_MODEL, 2 * PEER_HEADS * N_KEYS)), full((N_KEYS, N_KEYS)), full((N_KEYS, N_KEYS)),
                pl.BlockSpec((ce, D_MODEL), lambda i, j: (j, 0)),
                pl.BlockSpec((D_MODEL, ce), lambda i, j: (0, j))]
    args = [h, g, wq, k1, k2, u, vt]
    kern = _peer_kernel_plain
    if final_g is not None:
        in_specs.append(full((1, D_MODEL)))
        args.append(final_g)
        kern = _peer_kernel_final
    head_buf = pltpu.VMEM((PEER_HEADS, N_KEYS, tm), F32)
    return pl.pallas_call(
        functools.partial(kern, ce=ce, th=th),
        out_shape=jax.ShapeDtypeStruct((t_tokens, D_MODEL), F32),
        grid=(t_tokens // tm, N_EXPERTS // ce),
        in_specs=in_specs,
        out_specs=pl.BlockSpec((tm, D_MODEL), lambda i, j: (i, 0)),
        scratch_shapes=[pltpu.VMEM((tm, D_MODEL), BF16), head_buf, head_buf,
                        pltpu.VMEM((PEER_HEADS, tm), F32),
                        pltpu.VMEM((PEER_TOPK, th), F32), pltpu.VMEM((PEER_TOPK, th), F32),
                        pltpu.VMEM((_CAND_ROWS, th), F32), pltpu.VMEM((PEER_TOPK, th), F32),
                        pltpu.VMEM((D_MODEL, tm), F32), pltpu.VMEM((ce, tm), BF16)],
        compiler_params=_params(("parallel", "arbitrary")),
        name="peer_ffn",
    )(*args)


def _transpose_cast_kernel(x_ref, o_ref):
    o_ref[...] = x_ref[...].T.astype(BF16)


def _transpose_cast(x, layer, tr=TABLE_ROWS):
    _, rows, cols = x.shape
    return pl.pallas_call(
        _transpose_cast_kernel,
        out_shape=jax.ShapeDtypeStruct((cols, rows), BF16),
        grid=(rows // tr,),
        in_specs=[pl.BlockSpec((None, tr, cols), lambda i: (layer, i, 0))],
        out_specs=pl.BlockSpec((cols, tr), lambda i: (0, i)),
        compiler_params=_params(("parallel",)),
        name="transpose_cast",
    )(x)


def _cast_kernel(x_ref, o_ref):
    o_ref[...] = x_ref[...].astype(BF16)


def _cast_layer(x, layer, tr=2 * TABLE_ROWS):
    _, rows, cols = x.shape
    tr = min(tr, rows)
    return pl.pallas_call(
        _cast_kernel,
        out_shape=jax.ShapeDtypeStruct((rows, cols), BF16),
        grid=(rows // tr,),
        in_specs=[pl.BlockSpec((None, tr, cols), lambda i: (layer, i, 0))],
        out_specs=pl.BlockSpec((tr, cols), lambda i: (i, 0)),
        compiler_params=_params(("parallel",)),
        name="cast_layer",
    )(x)


_IN_SIZES = (WIDTH,) * 6 + (N_HEADS, WIDTH, C_KV_HEADS * HEAD_DIM, C_KV_HEADS * HEAD_DIM, 3 * D_MODEL)
_IN_STARTS = tuple(sum(_IN_SIZES[:k]) for k in range(len(_IN_SIZES)))
_C_HEAD_ORDER = (0, 3, 1, 4, 2, 5)


def _w_in_kernel(w_ref, watt_ref, wgate_ref):
    def piece(idx, offset=0, size=None):
        start = _IN_STARTS[idx] + offset
        return w_ref[:, start:start + (_IN_SIZES[idx] if size is None else size)].astype(BF16)

    col = 0
    for idx in (0, 1, 2, 3, 4, 5):
        watt_ref[:, col:col + WIDTH] = piece(idx)
        col += WIDTH
    for hd in _C_HEAD_ORDER:
        watt_ref[:, col:col + HEAD_DIM] = piece(7, hd * HEAD_DIM, HEAD_DIM)
        col += HEAD_DIM
    for idx in (8, 9):
        watt_ref[:, col:col + LANE] = piece(idx)
        col += LANE
    forget = piece(6)
    watt_ref[:, col:col + LANE] = jnp.concatenate(
        [forget, jnp.zeros((forget.shape[0], LANE - N_HEADS), BF16)], axis=1)
    wgate_ref[...] = piece(10)


def _w_in_layout(w_in, layer, tr=TABLE_ROWS // 2):
    _, rows, cols = w_in.shape
    att_cols = (N_QKV_BLOCKS + 1) * LANE
    return pl.pallas_call(
        _w_in_kernel,
        out_shape=(jax.ShapeDtypeStruct((rows, att_cols), BF16), jax.ShapeDtypeStruct((rows, 3 * D_MODEL), BF16)),
        grid=(rows // tr,),
        in_specs=[pl.BlockSpec((None, tr, cols), lambda i: (layer, i, 0))],
        out_specs=(pl.BlockSpec((tr, att_cols), lambda i: (i, 0)), pl.BlockSpec((tr, 3 * D_MODEL), lambda i: (i, 0))),
        compiler_params=_params(("parallel",)),
        name="w_in_layout",
    )(w_in)


def _rope_tables(positions):
    inv_freq = ROPE_THETA ** (-jnp.arange(0, HEAD_DIM, 2, dtype=F32) / HEAD_DIM)
    ang = positions.astype(F32).reshape(-1, 1) * inv_freq
    cos, sin = jnp.cos(ang), jnp.sin(ang)
    return jnp.concatenate([cos] * 4, axis=-1), jnp.concatenate([-sin, sin, -sin, sin], axis=-1)


def _permute_heads(w, axis):
    heads = [lax.slice_in_dim(w, hd * HEAD_DIM, (hd + 1) * HEAD_DIM, axis=axis) for hd in _C_HEAD_ORDER]
    return jnp.concatenate(heads, axis=axis)


def _mixer(h, cos4, sin4, batch, norm_g, w_att, wg, b_forget, sinks, b_gate, w_br_a, w_br_b, w_br_c, w_out):
    t_tokens = h.shape[0]
    seq = t_tokens // batch
    qkv, f, *qkv_classes = _in_proj(h, norm_g.reshape(1, -1), w_att, cos4, sin4, batch)
    qkv3 = qkv.reshape(batch, seq, -1)

    o1, l1 = _banded(qkv3, COL_QA, COL_KA, COL_VA, A_MAX_DIST, kv_shared=False)
    class_outs, class_lses = [], []
    for dil, x in zip(A_DILATIONS[1:], qkv_classes):
        o, lse = _banded(x.reshape(batch * dil, seq // dil, -1), COL_QA, COL_KA, COL_VA, A_MAX_DIST, kv_shared=False)
        class_outs.append(o.reshape(batch, dil, seq // dil, WIDTH))
        class_lses.append(lse.reshape(batch, dil, seq // dil, WIDTH))
    oa = _combine(o1.reshape(t_tokens, WIDTH), l1.reshape(t_tokens, WIDTH), class_outs, class_lses, batch)

    sink_tab = jnp.stack([sinks[hd] for hd in _C_HEAD_ORDER]).astype(F32)
    (oc,) = _banded(qkv3, COL_QC, COL_KC, COL_VC, C_MAX_DIST, kv_shared=True, sink=sink_tab, want_lse=False)
    oc = oc.reshape(t_tokens, WIDTH)

    bf_row = jnp.pad(b_forget.astype(F32), (0, LANE - N_HEADS)).reshape(1, LANE)
    augq, augk = _cum_aug(f.reshape(batch, seq, LANE), bf_row)
    ob = _fox(qkv3, augq, augk).reshape(t_tokens, WIDTH)

    return _merge(h, norm_g.reshape(1, -1), oa, ob, oc, wg, b_gate.reshape(1, -1).astype(F32),
                  w_br_a.astype(BF16), w_br_b.astype(BF16), _permute_heads(w_br_c, 0).astype(BF16),
                  w_out.astype(BF16))


def kernel(x, positions, norm1_g, w_in, b_forget, sinks, b_gate, w_br_a, w_br_b, w_br_c, w_out, norm2_g,
           peer_wq, peer_k1, peer_k2, peer_u, peer_v, final_g):
    batch, seq, d_model = x.shape
    depth = w_in.shape[0]
    assert d_model == D_MODEL and seq % max(COMBINE_ROWS, CUM_ROWS) == 0 and seq // A_DILATIONS[-1] >= BLOCK
    assert w_in.shape[2] == sum(_IN_SIZES) and peer_u.shape[1:] == (N_EXPERTS, D_MODEL)
    cos4, sin4 = _rope_tables(positions)
    h = x.reshape(batch * seq, D_MODEL)
    for l in range(depth):
        w_att, w_gate = _w_in_layout(w_in, l)
        h = _mixer(h, cos4, sin4, batch, norm1_g[l], w_att, w_gate, b_forget[l], sinks[l], b_gate[l],
                   w_br_a[l], w_br_b[l], w_br_c[l], w_out[l])
        h = _peer(h, norm2_g[l].reshape(1, -1), _cast_layer(peer_wq, l), peer_k1[l].astype(BF16),
                  peer_k2[l].astype(BF16), _cast_layer(peer_u, l), _transpose_cast(peer_v, l),
                  final_g=final_g.reshape(1, -1) if l == depth - 1 else None)
    return h.reshape(batch, seq, D_MODEL)
```

```python
import functools
import math

import jax
import jax.numpy as jnp
from jax import lax
from jax.experimental import pallas as pl
from jax.experimental.pallas import tpu as pltpu

F32 = jnp.float32
BF16 = jnp.bfloat16

D_MODEL = 1024
HEAD_DIM = 64
N_HEADS = 6
N_PAIRS = N_HEADS // 2
C_KV_HEADS = 2
A_DILATIONS = (1, 4, 16)
A_MAX_DIST = 128
C_MAX_DIST = 127
BLOCK = 128
LANE = 128
ROPE_THETA = 10000.0
RMS_EPS = 1e-6
NEG_INF = -1e30
PEER_HEADS = 8
N_KEYS = 128
N_EXPERTS = N_KEYS * N_KEYS
PEER_TOPK = 16
WIDTH = N_HEADS * HEAD_DIM

COL_QA, COL_KA, COL_VA = 0, 3, 6
COL_QB, COL_KB, COL_VB = 9, 12, 15
COL_QC, COL_KC, COL_VC = 18, 21, 22
N_QKV_BLOCKS = 23
COL_F = 23
A_BLOCKS = 9
ROPE_BLOCKS = (0, 1, 2, 3, 4, 5, 18, 19, 20, 21)
QUERY_BLOCKS = (0, 1, 2, 9, 10, 11, 18, 19, 20)
QK_SCALE = HEAD_DIM ** -0.5
LOG2E = math.log2(math.e)
GELU_ARG_SCALE = math.sqrt(0.5)

VMEM_LIMIT = 56 * 1024 * 1024

IN_PROJ_ROWS = 512
BANDED_QUERY_BLOCKS = 8
COMBINE_ROWS = 1024
CUM_ROWS = 2048
FOX_QUERIES = 512
FOX_KEYS = 512
MERGE_ROWS = 512
PEER_ROWS = 512
PEER_EXPERTS = 2048
PEER_HALF = 256
TABLE_ROWS = 512


def _params(sem):
    return pltpu.CompilerParams(dimension_semantics=sem, vmem_limit_bytes=VMEM_LIMIT)


def _dot(a, b):
    return jnp.dot(a, b, preferred_element_type=F32)


def _dot_nt(a, b):
    return lax.dot_general(a, b, (((1,), (1,)), ((), ())), preferred_element_type=F32)


def _rms(x, g):
    var = jnp.mean(x * x, axis=-1, keepdims=True)
    return x * lax.rsqrt(var + RMS_EPS) * g


def _in_proj_kernel(h_ref, g_ref, w_ref, cos_ref, sin_ref, qkv_ref, f_ref, *rest):
    class_refs, proj_s, fin_s = rest[:-2], rest[-2], rest[-1]
    tm = h_ref.shape[0]
    xn = _rms(h_ref[...], g_ref[...]).astype(BF16)
    proj_s[...] = _dot(xn, w_ref[...])
    cos = cos_ref[...]
    sin = sin_ref[...]
    lane = lax.broadcasted_iota(jnp.int32, cos.shape, 1)
    first_half = (lane % HEAD_DIM) < (HEAD_DIM // 2)
    for c in range(N_QKV_BLOCKS):
        t = proj_s[:, c * LANE:(c + 1) * LANE]
        if c in ROPE_BLOCKS:
            rot = jnp.where(first_half, pltpu.roll(t, LANE - HEAD_DIM // 2, 1), pltpu.roll(t, HEAD_DIM // 2, 1))
            t = t * cos + rot * sin
        if c in QUERY_BLOCKS:
            t = t * QK_SCALE
        if c < A_BLOCKS:
            fin_s[c] = t
        qkv_ref[:, c * LANE:(c + 1) * LANE] = t.astype(BF16)
    f_ref[...] = proj_s[:, COL_F * LANE:(COL_F + 1) * LANE]
    for dil, ref in zip(A_DILATIONS[1:], class_refs):
        for c in range(dil):
            for blk in range(A_BLOCKS):
                rows = fin_s[blk, pl.ds(c, tm // dil, stride=dil), :]
                ref[c, :, blk * LANE:(blk + 1) * LANE] = rows.astype(BF16)


def _in_proj(h, g, w_att, cos4, sin4, batch, tm=IN_PROJ_ROWS):
    t_tokens = h.shape[0]
    seq = t_tokens // batch
    tiles = seq // tm
    n_cols = w_att.shape[1]
    a_cols = A_BLOCKS * LANE
    class_shapes = tuple(jax.ShapeDtypeStruct((batch, dil, seq // dil, a_cols), BF16) for dil in A_DILATIONS[1:])
    class_specs = tuple(pl.BlockSpec((None, dil, tm // dil, a_cols), lambda i: (i // tiles, 0, i % tiles, 0))
                        for dil in A_DILATIONS[1:])
    return pl.pallas_call(
        _in_proj_kernel,
        out_shape=(jax.ShapeDtypeStruct((t_tokens, N_QKV_BLOCKS * LANE), BF16),
                   jax.ShapeDtypeStruct((t_tokens, LANE), F32)) + class_shapes,
        grid=(t_tokens // tm,),
        in_specs=[pl.BlockSpec((tm, D_MODEL), lambda i: (i, 0)),
                  pl.BlockSpec((1, D_MODEL), lambda i: (0, 0)),
                  pl.BlockSpec((D_MODEL, n_cols), lambda i: (0, 0)),
                  pl.BlockSpec((tm, LANE), lambda i: (i, 0)),
                  pl.BlockSpec((tm, LANE), lambda i: (i, 0))],
        out_specs=(pl.BlockSpec((tm, N_QKV_BLOCKS * LANE), lambda i: (i, 0)),
                   pl.BlockSpec((tm, LANE), lambda i: (i, 0))) + class_specs,
        scratch_shapes=[pltpu.VMEM((tm, n_cols), F32), pltpu.VMEM((A_BLOCKS, tm, LANE), F32)],
        compiler_params=_params(("parallel",)),
        name="in_proj",
    )(h, g, w_att, cos4, sin4)


def _banded_kernel(*refs, max_dist, qb, kv_shared, has_sink, want_lse):
    refs = list(refs)
    q_ref, kp_ref, kc_ref, vp_ref, vc_ref = refs[:5]
    pos = 5
    sink_ref = None
    if has_sink:
        sink_ref = refs[pos]
        pos += 1
    o_ref = refs[pos]
    pos += 1
    lse_ref = None
    if want_lse:
        lse_ref = refs[pos]
        pos += 1
    kwin, vwin = refs[pos], refs[pos + 1]

    i = pl.program_id(1)
    kwin[0:BLOCK, :] = kp_ref[...]
    kwin[BLOCK:, :] = kc_ref[...]
    vwin[0:BLOCK, :] = vp_ref[...]
    vwin[BLOCK:, :] = vc_ref[...]

    row = lax.broadcasted_iota(jnp.int32, (BLOCK, 2 * BLOCK), 0)
    col = lax.broadcasted_iota(jnp.int32, (BLOCK, 2 * BLOCK), 1)
    dist = row + BLOCK - col
    band = (dist >= 0) & (dist <= max_dist)
    lane = lax.broadcasted_iota(jnp.int32, (1, LANE), 1)
    lower = lane < HEAD_DIM

    def body(sb, carry):
        off = pl.multiple_of(sb * BLOCK, BLOCK)
        valid = band & ((col >= BLOCK) | (i * qb + sb > 0))
        for p in range(N_PAIRS):
            q = q_ref[pl.ds(off, BLOCK), p * LANE:(p + 1) * LANE]
            kv_cols = slice(0, LANE) if kv_shared else slice(p * LANE, (p + 1) * LANE)
            k = kwin[pl.ds(off, 2 * BLOCK), kv_cols]
            v = vwin[pl.ds(off, 2 * BLOCK), kv_cols]
            o_pair = jnp.zeros((BLOCK, LANE), F32)
            lse_pair = jnp.zeros((BLOCK, LANE), F32)
            for hf in range(2):
                sel = lower if hf == 0 else jnp.logical_not(lower)
                qm = jnp.where(sel, q, jnp.zeros_like(q))
                s = _dot_nt(qm, k)
                s = jnp.where(valid, s, NEG_INF)
                m = jnp.max(s, axis=-1, keepdims=True)
                if has_sink:
                    sk = sink_ref[2 * p + hf]
                    m = jnp.maximum(m, sk)
                pr = jnp.exp(s - m)
                den = jnp.sum(pr, axis=-1, keepdims=True)
                if has_sink:
                    den = den + jnp.exp(sk - m)
                o = _dot(pr.astype(BF16), v) / den
                o_pair = jnp.where(sel, o, o_pair)
                if want_lse:
                    lse_pair = jnp.where(sel, m + jnp.log(den), lse_pair)
            o_ref[pl.ds(off, BLOCK), p * LANE:(p + 1) * LANE] = o_pair.astype(BF16)
            if want_lse:
                lse_ref[pl.ds(off, BLOCK), p * LANE:(p + 1) * LANE] = lse_pair
        return carry

    lax.fori_loop(0, qb, body, 0, unroll=True)


def _banded(x, qcol, kcol, vcol, max_dist, kv_shared, sink=None, want_lse=True, qb=BANDED_QUERY_BLOCKS):
    n, length, _ = x.shape
    qb = min(qb, length // BLOCK)
    tq = qb * BLOCK
    kvw = LANE if kv_shared else WIDTH
    kblk = kcol if kv_shared else kcol // N_PAIRS
    vblk = vcol if kv_shared else vcol // N_PAIRS
    in_specs = [
        pl.BlockSpec((None, tq, WIDTH), lambda b, i: (b, i, qcol // N_PAIRS)),
        pl.BlockSpec((None, BLOCK, kvw), lambda b, i: (b, jnp.maximum(i * qb - 1, 0), kblk)),
        pl.BlockSpec((None, tq, kvw), lambda b, i: (b, i, kblk)),
        pl.BlockSpec((None, BLOCK, kvw), lambda b, i: (b, jnp.maximum(i * qb - 1, 0), vblk)),
        pl.BlockSpec((None, tq, kvw), lambda b, i: (b, i, vblk)),
    ]
    args = [x, x, x, x, x]
    if sink is not None:
        in_specs.append(pl.BlockSpec(memory_space=pltpu.SMEM))
        args.append(sink)
    out_shape = [jax.ShapeDtypeStruct((n, length, WIDTH), BF16)]
    out_specs = [pl.BlockSpec((None, tq, WIDTH), lambda b, i: (b, i, 0))]
    if want_lse:
        out_shape.append(jax.ShapeDtypeStruct((n, length, WIDTH), F32))
        out_specs.append(pl.BlockSpec((None, tq, WIDTH), lambda b, i: (b, i, 0)))
    kern = functools.partial(_banded_kernel, max_dist=max_dist, qb=qb, kv_shared=kv_shared,
                             has_sink=sink is not None, want_lse=want_lse)
    return pl.pallas_call(
        kern,
        out_shape=tuple(out_shape),
        grid=(n, length // tq),
        in_specs=in_specs,
        out_specs=tuple(out_specs),
        scratch_shapes=[pltpu.VMEM((tq + BLOCK, kvw), BF16), pltpu.VMEM((tq + BLOCK, kvw), BF16)],
        compiler_params=_params(("parallel", "parallel")),
        name="banded_attention",
    )(*args)


def _combine_kernel(o1_ref, l1_ref, *rest):
    n_cls = len(A_DILATIONS) - 1
    o_refs, l_refs = rest[:n_cls], rest[n_cls:2 * n_cls]
    out_ref = rest[2 * n_cls]
    o_s, l_s = rest[2 * n_cls + 1:3 * n_cls + 1], rest[3 * n_cls + 1:]
    tm = o1_ref.shape[0]
    for dil, o_ref, l_ref, os_, ls_ in zip(A_DILATIONS[1:], o_refs, l_refs, o_s, l_s):
        for c in range(dil):
            for p in range(N_PAIRS):
                cols = slice(p * LANE, (p + 1) * LANE)
                os_[p, pl.ds(c, tm // dil, stride=dil), :] = o_ref[c, :, cols].astype(F32)
                ls_[p, pl.ds(c, tm // dil, stride=dil), :] = l_ref[c, :, cols]
    for p in range(N_PAIRS):
        cols = slice(p * LANE, (p + 1) * LANE)
        lses = [l1_ref[:, cols]] + [ls_[p] for ls_ in l_s]
        outs = [o1_ref[:, cols].astype(F32)] + [os_[p] for os_ in o_s]
        m = functools.reduce(jnp.maximum, lses)
        ws = [jnp.exp(l - m) for l in lses]
        num = functools.reduce(lambda a, b: a + b, [w * o for w, o in zip(ws, outs)])
        out_ref[:, cols] = (num / functools.reduce(lambda a, b: a + b, ws)).astype(BF16)


def _combine(o1, l1, class_outs, class_lses, batch, tm=COMBINE_ROWS):
    t_tokens = o1.shape[0]
    tiles = t_tokens // batch // tm
    tok = pl.BlockSpec((tm, WIDTH), lambda i: (i, 0))
    cls = [pl.BlockSpec((None, dil, tm // dil, WIDTH), lambda i: (i // tiles, 0, i % tiles, 0))
           for dil in A_DILATIONS[1:]]
    n_cls = len(cls)
    return pl.pallas_call(
        _combine_kernel,
        out_shape=jax.ShapeDtypeStruct((t_tokens, WIDTH), BF16),
        grid=(t_tokens // tm,),
        in_specs=[tok, tok] + cls + cls,
        out_specs=tok,
        scratch_shapes=[pltpu.VMEM((N_PAIRS, tm, LANE), F32)] * (2 * n_cls),
        compiler_params=_params(("parallel",)),
        name="combine_patterns",
    )(o1, l1, *class_outs, *class_lses)


CUM_BLOCK = 256
BF16_ROWS = 16
UNDERFLOW = 104.0
NORM_SLACK = 1.02


def _split3(x):
    p1 = x.astype(BF16)
    rest = x - p1.astype(F32)
    p2 = rest.astype(BF16)
    p3 = (rest - p2.astype(F32)).astype(BF16)
    return p1, p2, p3


def _cum_kernel(f_ref, bf_ref, augq_ref, augk_ref, carry_s):
    seq = f_ref.shape[0]

    @pl.when(pl.program_id(1) == 0)
    def _():
        carry_s[...] = jnp.zeros_like(carry_s)

    rr = lax.broadcasted_iota(jnp.int32, (CUM_BLOCK, CUM_BLOCK), 0)
    cc = lax.broadcasted_iota(jnp.int32, (CUM_BLOCK, CUM_BLOCK), 1)
    tril = jnp.where(rr >= cc, 1.0, 0.0).astype(BF16)
    n_out = 2 * N_HEADS * LANE
    pr = lax.broadcasted_iota(jnp.int32, (LANE, n_out), 0)
    pc = lax.broadcasted_iota(jnp.int32, (LANE, n_out), 1)
    same_head = pr == pc // (2 * LANE)
    k_side = (pc // LANE) % 2 == 1
    lane64 = pc % HEAD_DIM
    places = [jnp.where(same_head & jnp.logical_not(k_side) & (lane64 == i), 1.0,
                        jnp.where(same_head & k_side & (lane64 == 3 + i), -1.0, 0.0)).astype(BF16)
              for i in range(3)]
    c1d = lax.broadcasted_iota(jnp.int32, (1, n_out), 1)
    c_k = (c1d // LANE) % 2 == 1
    c_l = c1d % HEAD_DIM
    ones_at = (c_k & (c_l < 3)) | (jnp.logical_not(c_k) & (c_l >= 3) & (c_l < 6))
    const = jnp.where(ones_at, 1.0, 0.0).astype(F32)
    bias = bf_ref[...]

    def body(blk, carry):
        off = pl.multiple_of(blk * CUM_BLOCK, CUM_BLOCK)
        x = f_ref[pl.ds(off, CUM_BLOCK), :] + bias
        log_f = jnp.minimum(x, 0.0) - jnp.log1p(jnp.exp(-jnp.abs(x)))
        cum = sum(_dot(tril, piece) for piece in _split3(log_f)) + carry
        aug = sum(_dot(piece, place) for piece, place in zip(_split3(cum), places)) + const
        for head in range(N_HEADS):
            base = head * 2 * LANE
            augq_ref[head, pl.ds(off, CUM_BLOCK), :] = aug[:, base:base + LANE].astype(BF16)
            augk_ref[head, pl.ds(off, CUM_BLOCK), :] = aug[:, base + LANE:base + 2 * LANE].astype(BF16)
        return cum[CUM_BLOCK - 1:CUM_BLOCK, :]

    carry_s[...] = lax.fori_loop(0, seq // CUM_BLOCK, body, carry_s[...])


def _cum_aug(f, b_forget_row, ts=CUM_ROWS):
    b, seq, _ = f.shape
    ts = min(ts, seq)
    out = jax.ShapeDtypeStruct((b, N_HEADS, seq, LANE), BF16)
    ospec = pl.BlockSpec((None, N_HEADS, ts, LANE), lambda bi, si: (bi, 0, si, 0))
    return pl.pallas_call(
        _cum_kernel,
        out_shape=(out, out),
        grid=(b, seq // ts),
        in_specs=[pl.BlockSpec((None, ts, LANE), lambda bi, si: (bi, si, 0)),
                  pl.BlockSpec((1, LANE), lambda bi, si: (0, 0))],
        out_specs=(ospec, ospec),
        scratch_shapes=[pltpu.VMEM((1, LANE), F32)],
        compiler_params=_params(("parallel", "arbitrary")),
        name="forget_cumsum",
    )(f, b_forget_row)


def _fox_kernel(q_ref, k_ref, v_ref, aq_ref, ak_ref, o_ref, kaug, qaug, kstat, *, tq, tk):
    qi = pl.program_id(2)
    n_kb = k_ref.shape[0] // tk
    n_sub = tq // tk
    lane = lax.broadcasted_iota(jnp.int32, (1, LANE), 1)
    lower = lane < HEAD_DIM
    hr = lax.broadcasted_iota(jnp.int32, (LANE, LANE), 0) // HEAD_DIM
    hc = lax.broadcasted_iota(jnp.int32, (LANE, LANE), 1) // HEAD_DIM
    same_head = jnp.where(hr == hc, 1.0, 0.0).astype(BF16)

    def max_sq_norm(x):
        xf = x.astype(F32)
        return jnp.max(_dot((xf * xf).astype(BF16), same_head), axis=0, keepdims=True)

    def lane_sum(row_vec, lo, hi):
        return jnp.sum(jnp.where((lane >= lo) & (lane < hi), row_vec.astype(F32), 0.0), axis=1, keepdims=True)

    @pl.when(qi == 0)
    def _():
        k = k_ref[...]
        kaug[0] = jnp.where(lower, k, ak_ref[0])
        kaug[1] = jnp.where(lower, ak_ref[1], k)

        def stats(kb, carry):
            kn0, kn1, nc0, nc1 = carry
            off = pl.multiple_of(kb * tk, tk)
            n2 = max_sq_norm(k_ref[pl.ds(off, tk), :])
            tail = pl.multiple_of(off + tk - BF16_ROWS, BF16_ROWS)
            here = lane == kb
            kn0 = jnp.where(here, n2[:, 0:1], kn0)
            kn1 = jnp.where(here, n2[:, HEAD_DIM:HEAD_DIM + 1], kn1)
            last0 = ak_ref[0, pl.ds(tail, BF16_ROWS), :][BF16_ROWS - 1:BF16_ROWS, :]
            last1 = ak_ref[1, pl.ds(tail, BF16_ROWS), :][BF16_ROWS - 1:BF16_ROWS, :]
            nc0 = jnp.where(here, lane_sum(last0, 3, 6), nc0)
            nc1 = jnp.where(here, lane_sum(last1, 3, 6), nc1)
            return kn0, kn1, nc0, nc1

        zero = jnp.zeros((1, LANE), F32)
        kn0, kn1, nc0, nc1 = lax.fori_loop(0, n_kb, stats, (zero, zero, zero, zero))
        kstat[0:1, :] = kn0
        kstat[1:2, :] = kn1
        kstat[2:3, :] = nc0
        kstat[3:4, :] = nc1

    q = q_ref[...]
    qaug[0] = jnp.where(lower, q, aq_ref[0])
    qaug[1] = jnp.where(lower, aq_ref[1], q)
    row = lax.broadcasted_iota(jnp.int32, (tq, tk), 0)
    col = lax.broadcasted_iota(jnp.int32, (tq, tk), 1)

    qn = max_sq_norm(q)
    first = None
    for hf in range(2):
        qn2 = qn[:, hf * HEAD_DIM:hf * HEAD_DIM + 1]
        kn2 = kstat[hf:hf + 1, :]
        kn2_all = jnp.max(kn2, axis=1, keepdims=True)
        cum_q = lane_sum(aq_ref[hf, 0:1, :], 0, 3)
        upper = jnp.sqrt(qn2 * kn2) * NORM_SLACK + cum_q + kstat[2 + hf:3 + hf, :] + 1.0
        floor = -jnp.sqrt(qn2 * kn2_all) * NORM_SLACK
        needed = (upper >= floor - UNDERFLOW) & (lane < n_kb)
        first_h = jnp.min(jnp.where(needed, lane.astype(F32), float(n_kb)))
        first = first_h if first is None else jnp.minimum(first, first_h)
    kb_start = jnp.minimum(first.astype(jnp.int32), qi * n_sub)

    def step(kb, carry, mask_shift):
        off = pl.multiple_of(kb * tk, tk)
        v = v_ref[pl.ds(off, tk), :]
        new = []
        for hf in range(2):
            m, l, acc = carry[hf]
            s = _dot_nt(qaug[hf], kaug[hf, pl.ds(off, tk), :])
            if mask_shift is not None:
                s = jnp.where(col + mask_shift <= row, s, NEG_INF)
            m_new = jnp.maximum(m, jnp.max(s, axis=-1, keepdims=True))
            alpha = jnp.exp(m - m_new)
            pr = jnp.exp(s - m_new)
            l = alpha * l + jnp.sum(pr, axis=-1, keepdims=True)
            acc = alpha * acc + _dot(pr.astype(BF16), v)
            new.append((m_new, l, acc))
        return tuple(new)

    one = (jnp.full((tq, 1), NEG_INF, F32), jnp.zeros((tq, 1), F32), jnp.zeros((tq, LANE), F32))
    carry = lax.fori_loop(kb_start, qi * n_sub, functools.partial(step, mask_shift=None), (one, one))
    for d in range(n_sub):
        carry = step(qi * n_sub + d, carry, d * tk)
    (_, l0, acc0), (_, l1, acc1) = carry
    o_ref[...] = jnp.where(lower, acc0 / l0, acc1 / l1).astype(BF16)


def _fox(qkv, augq, augk, tq=FOX_QUERIES, tk=FOX_KEYS):
    b, seq, _ = qkv.shape
    return pl.pallas_call(
        functools.partial(_fox_kernel, tq=tq, tk=tk),
        out_shape=jax.ShapeDtypeStruct((b, seq, WIDTH), BF16),
        grid=(b, N_PAIRS, seq // tq),
        in_specs=[pl.BlockSpec((None, tq, LANE), lambda bi, p, i: (bi, i, COL_QB + p)),
                  pl.BlockSpec((None, seq, LANE), lambda bi, p, i: (bi, 0, COL_KB + p)),
                  pl.BlockSpec((None, seq, LANE), lambda bi, p, i: (bi, 0, COL_VB + p)),
                  pl.BlockSpec((None, 2, tq, LANE), lambda bi, p, i: (bi, p, i, 0)),
                  pl.BlockSpec((None, 2, seq, LANE), lambda bi, p, i: (bi, p, 0, 0))],
        out_specs=pl.BlockSpec((None, tq, LANE), lambda bi, p, i: (bi, i, p)),
        scratch_shapes=[pltpu.VMEM((2, seq, LANE), BF16), pltpu.VMEM((2, tq, LANE), BF16),
                        pltpu.VMEM((8, LANE), F32)],
        compiler_params=_params(("parallel", "parallel", "arbitrary")),
        name="forgetting_attention",
    )(qkv, qkv, qkv, augq, augk)


def _merge_kernel(h_ref, g_ref, oa_ref, ob_ref, oc_ref, wg_ref, bg_ref, wa_ref, wb_ref, wc_ref, wo_ref, out_ref):
    h = h_ref[...]
    xn = _rms(h, g_ref[...]).astype(BF16)
    merged = jnp.zeros(h.shape, F32)
    for idx, (o_ref, w_ref) in enumerate(((oa_ref, wa_ref), (ob_ref, wb_ref), (oc_ref, wc_ref))):
        cols = slice(idx * D_MODEL, (idx + 1) * D_MODEL)
        gate = jax.nn.sigmoid(_dot(xn, wg_ref[:, cols]) + bg_ref[:, cols])
        merged = merged + gate * _dot(o_ref[...], w_ref[...])
    out_ref[...] = h + _dot(merged.astype(BF16), wo_ref[...])


def _merge(h, g, oa, ob, oc, w_gate, b_gate, w_a, w_b, w_c, w_out, tm=MERGE_ROWS):
    t_tokens = h.shape[0]
    full = lambda shape: pl.BlockSpec(shape, lambda i: (0, 0))
    ospec = pl.BlockSpec((tm, WIDTH), lambda i: (i, 0))
    return pl.pallas_call(
        _merge_kernel,
        out_shape=jax.ShapeDtypeStruct((t_tokens, D_MODEL), F32),
        grid=(t_tokens // tm,),
        in_specs=[pl.BlockSpec((tm, D_MODEL), lambda i: (i, 0)), full((1, D_MODEL)),
                  ospec, ospec, ospec,
                  full((D_MODEL, 3 * D_MODEL)), full((1, 3 * D_MODEL)),
                  full((WIDTH, D_MODEL)), full((WIDTH, D_MODEL)), full((WIDTH, D_MODEL)),
                  full((D_MODEL, D_MODEL))],
        out_specs=pl.BlockSpec((tm, D_MODEL), lambda i: (i, 0)),
        compiler_params=_params(("parallel",)),
        name="gated_merge",
    )(h, g, oa, ob, oc, w_gate, b_gate, w_a, w_b, w_c, w_out)


_CAND_COUNTS = tuple(PEER_TOPK // (i + 1) for i in range(PEER_TOPK))
_CAND_ROWS = 64
PIECE = 256


def _sort_pairs(n):
    out, p = [], 1
    while p < n:
        k = p
        while k >= 1:
            for j in range(k % p, n - k, 2 * k):
                for i in range(min(k, n - j - k)):
                    if (i + j) // (2 * p) == (i + j + k) // (2 * p):
                        out.append((i + j, i + j + k))
            k //= 2
        p *= 2
    return tuple(out)


SUBLANES = 8


def _top16_by_networks(src, dst_ref):
    n = src.shape[0] // SUBLANES
    rows = [src[SUBLANES * k:SUBLANES * (k + 1), :] for k in range(n)]

    def exchange(i, j):
        rows[i], rows[j] = jnp.maximum(rows[i], rows[j]), jnp.minimum(rows[i], rows[j])

    def sort_bitonic16():
        for d in (8, 4, 2, 1):
            for k in range(PEER_TOPK):
                if k & d == 0:
                    exchange(k, k + d)

    for i, j in _sort_pairs(n):
        exchange(i, j)
    for shift in (4, 2, 1):
        other = [pltpu.roll(r, shift, 0) for r in rows]
        if len(rows) < PEER_TOPK:
            rows = rows + other[::-1]
        else:
            rows = [jnp.maximum(rows[k], other[PEER_TOPK - 1 - k]) for k in range(PEER_TOPK)]
        sort_bitonic16()
    for r in range(PEER_TOPK):
        dst_ref[r:r + 1, :] = rows[r][0:1, :]


def _peer_kernel_plain(h_ref, g_ref, wq_ref, k1_ref, k2_ref, u_ref, vt_ref, out_ref, *scratch, ce, th):
    _peer_body(h_ref, g_ref, wq_ref, k1_ref, k2_ref, u_ref, vt_ref, None, out_ref, *scratch, ce=ce, th=th)


def _peer_kernel_final(h_ref, g_ref, wq_ref, k1_ref, k2_ref, u_ref, vt_ref, fg_ref, out_ref, *scratch, ce, th):
    _peer_body(h_ref, g_ref, wq_ref, k1_ref, k2_ref, u_ref, vt_ref, fg_ref, out_ref, *scratch, ce=ce, th=th)


def _peer_body(h_ref, g_ref, wq_ref, k1_ref, k2_ref, u_ref, vt_ref, fg_ref, out_ref,
               xn_s, s1_s, s2_s, tau_s, v1_s, v2_s, cand_s, top_s, y_s, w_s, *, ce, th):
    j = pl.program_id(1)
    nj = pl.num_programs(1)
    tm = h_ref.shape[0]
    a_per_step = ce // N_KEYS
    halves = [slice(t0, t0 + th) for t0 in range(0, tm, th)]

    def fill_candidates():
        off = 0
        for i, cnt in enumerate(_CAND_COUNTS):
            cand_s[off:off + cnt, :] = v1_s[i:i + 1, :] + v2_s[0:cnt, :]
            off += cnt

    @pl.when(j == 0)
    def _route():
        xn_s[...] = _rms(h_ref[...], g_ref[...]).astype(BF16)
        y_s[...] = jnp.zeros_like(y_s)
        cand_s[...] = jnp.full(cand_s.shape, NEG_INF, F32)
        for tok in halves:
            xn = xn_s[tok, :]
            for hd in range(PEER_HEADS):
                q1 = _dot(xn, wq_ref[:, (2 * hd) * N_KEYS:(2 * hd + 1) * N_KEYS]).astype(BF16)
                q2 = _dot(xn, wq_ref[:, (2 * hd + 1) * N_KEYS:(2 * hd + 2) * N_KEYS]).astype(BF16)
                s1 = _dot_nt(k1_ref[...], q1)
                s2 = _dot_nt(k2_ref[...], q2)
                _top16_by_networks(s1, v1_s)
                _top16_by_networks(s2, v2_s)
                fill_candidates()
                _top16_by_networks(cand_s[...], top_s)
                top = top_s[...]
                mx = top[0:1, :]
                shift = mx + jnp.log(jnp.sum(jnp.exp(top - mx), axis=0, keepdims=True))
                s1_s[hd, :, tok] = (s1 - shift) * LOG2E - 0.5
                s2_s[hd, :, tok] = s2 * LOG2E
                v1_s[...] = (v1_s[...] - shift) * LOG2E - 0.5
                v2_s[...] = v2_s[...] * LOG2E
                fill_candidates()
                _top16_by_networks(cand_s[...], top_s)
                tau_s[hd:hd + 1, tok] = top_s[PEER_TOPK - 1:PEER_TOPK, :]

    n_pieces = ce // PIECE

    def act_piece(tok, p):
        return _dot_nt(u_ref[p * PIECE:(p + 1) * PIECE, :], xn_s[tok, :])

    def mix_piece(tok, p, act):
        for aa in range(PIECE // N_KEYS):
            a = j * a_per_step + p * (PIECE // N_KEYS) + aa
            coef = jnp.zeros((N_KEYS, th), F32)
            for hd in range(PEER_HEADS):
                s = s2_s[hd, :, tok] + s1_s[hd, pl.ds(a, 1), tok]
                coef = coef + jnp.where(s >= tau_s[hd:hd + 1, tok], jnp.exp2(s), 0.0)
            x = act[aa * N_KEYS:(aa + 1) * N_KEYS, :]
            gelu_scaled = x * (1.0 + lax.erf(x))
            rows = slice(p * PIECE + aa * N_KEYS, p * PIECE + (aa + 1) * N_KEYS)
            w_s[rows, tok] = (coef * gelu_scaled).astype(BF16)

    def out_piece(tok, p):
        return _dot(vt_ref[:, p * PIECE:(p + 1) * PIECE], w_s[p * PIECE:(p + 1) * PIECE, tok])

    acts = [act_piece(halves[0], p) for p in range(n_pieces)]
    for hi, tok in enumerate(halves):
        nxt = halves[hi + 1] if hi + 1 < len(halves) else None
        prv = halves[hi - 1] if hi > 0 else None
        nxt_acts, y_prev = [], None
        for p in range(n_pieces):
            mix_piece(tok, p, acts[p])
            if nxt is not None:
                nxt_acts.append(act_piece(nxt, p))
            if prv is not None:
                yp = out_piece(prv, p)
                y_prev = yp if y_prev is None else y_prev + yp
        if prv is not None:
            y_s[:, prv] += y_prev
        acts = nxt_acts
    y_last = out_piece(halves[-1], 0)
    for p in range(1, n_pieces):
        y_last = y_last + out_piece(halves[-1], p)
    y_s[:, halves[-1]] += y_last

    @pl.when(j == nj - 1)
    def _finish():
        y = h_ref[...] + y_s[...].T
        if fg_ref is not None:
            y = _rms(y, fg_ref[...])
        out_ref[...] = y


def _peer(h, g, wq, k1, k2, u, vt, final_g=None, tm=PEER_ROWS, ce=PEER_EXPERTS, th=PEER_HALF):
    t_tokens = h.shape[0]
    tm = min(tm, t_tokens)
    full = lambda shape: pl.BlockSpec(shape, lambda i, j: (0, 0))
    in_specs = [pl.BlockSpec((tm, D_MODEL), lambda i, j: (i, 0)), full((1, D_MODEL)),
                full((D_MODEL, 2 * PEER_HEADS * N_KEYS)), full((N_KEYS, N_KEYS)), full((N_KEYS, N_KEYS)),
                pl.BlockSpec((ce, D_MODEL), lambda i, j: (j, 0)),
                pl.BlockSpec((D_MODEL, ce), lambda i, j: (0, j))]
    args = [h, g, wq, k1, k2, u, vt]
    kern = _peer_kernel_plain
    if final_g is not None:
        in_specs.append(full((1, D_MODEL)))
        args.append(final_g)
        kern = _peer_kernel_final
    head_buf = pltpu.VMEM((PEER_HEADS, N_KEYS, tm), F32)
    return pl.pallas_call(
        functools.partial(kern, ce=ce, th=th),
        out_shape=jax.ShapeDtypeStruct((t_tokens, D_MODEL), F32),
        grid=(t_tokens // tm, N_EXPERTS // ce),
        in_specs=in_specs,
        out_specs=pl.BlockSpec((tm, D_MODEL), lambda i, j: (i, 0)),
        scratch_shapes=[pltpu.VMEM((tm, D_MODEL), BF16), head_buf, head_buf,
                        pltpu.VMEM((PEER_HEADS, tm), F32),
                        pltpu.VMEM((PEER_TOPK, th), F32), pltpu.VMEM((PEER_TOPK, th), F32),
                        pltpu.VMEM((_CAND_ROWS, th), F32), pltpu.VMEM((PEER_TOPK, th), F32),
                        pltpu.VMEM((D_MODEL, tm), F32), pltpu.VMEM((ce, tm), BF16)],
        compiler_params=_params(("parallel", "arbitrary")),
        name="peer_ffn",
    )(*args)


def _transpose_cast_kernel(x_ref, o_ref):
    o_ref[...] = x_ref[...].T.astype(BF16)


def _transpose_cast(x, layer, tr=TABLE_ROWS):
    _, rows, cols = x.shape
    return pl.pallas_call(
        _transpose_cast_kernel,
        out_shape=jax.ShapeDtypeStruct((cols, rows), BF16),
        grid=(rows // tr,),
        in_specs=[pl.BlockSpec((None, tr, cols), lambda i: (layer, i, 0))],
        out_specs=pl.BlockSpec((cols, tr), lambda i: (0, i)),
        compiler_params=_params(("parallel",)),
        name="transpose_cast",
    )(x)


def _cast_kernel(x_ref, o_ref, *, scale):
    o_ref[...] = (x_ref[...] * scale).astype(BF16)


def _cast_layer(x, layer, scale=1.0, tr=2 * TABLE_ROWS):
    _, rows, cols = x.shape
    tr = min(tr, rows)
    return pl.pallas_call(
        functools.partial(_cast_kernel, scale=scale),
        out_shape=jax.ShapeDtypeStruct((rows, cols), BF16),
        grid=(rows // tr,),
        in_specs=[pl.BlockSpec((None, tr, cols), lambda i: (layer, i, 0))],
        out_specs=pl.BlockSpec((tr, cols), lambda i: (i, 0)),
        compiler_params=_params(("parallel",)),
        name="cast_layer",
    )(x)


_IN_SIZES = (WIDTH,) * 6 + (N_HEADS, WIDTH, C_KV_HEADS * HEAD_DIM, C_KV_HEADS * HEAD_DIM, 3 * D_MODEL)
_IN_STARTS = tuple(sum(_IN_SIZES[:k]) for k in range(len(_IN_SIZES)))
_C_HEAD_ORDER = (0, 3, 1, 4, 2, 5)


def _w_in_kernel(w_ref, watt_ref, wgate_ref):
    def piece(idx, offset=0, size=None):
        start = _IN_STARTS[idx] + offset
        return w_ref[:, start:start + (_IN_SIZES[idx] if size is None else size)].astype(BF16)

    col = 0
    for idx in (0, 1, 2, 3, 4, 5):
        watt_ref[:, col:col + WIDTH] = piece(idx)
        col += WIDTH
    for hd in _C_HEAD_ORDER:
        watt_ref[:, col:col + HEAD_DIM] = piece(7, hd * HEAD_DIM, HEAD_DIM)
        col += HEAD_DIM
    for idx in (8, 9):
        watt_ref[:, col:col + LANE] = piece(idx)
        col += LANE
    forget = piece(6)
    watt_ref[:, col:col + LANE] = jnp.concatenate(
        [forget, jnp.zeros((forget.shape[0], LANE - N_HEADS), BF16)], axis=1)
    wgate_ref[...] = piece(10)


def _w_in_layout(w_in, layer, tr=TABLE_ROWS // 2):
    _, rows, cols = w_in.shape
    att_cols = (N_QKV_BLOCKS + 1) * LANE
    return pl.pallas_call(
        _w_in_kernel,
        out_shape=(jax.ShapeDtypeStruct((rows, att_cols), BF16), jax.ShapeDtypeStruct((rows, 3 * D_MODEL), BF16)),
        grid=(rows // tr,),
        in_specs=[pl.BlockSpec((None, tr, cols), lambda i: (layer, i, 0))],
        out_specs=(pl.BlockSpec((tr, att_cols), lambda i: (i, 0)), pl.BlockSpec((tr, 3 * D_MODEL), lambda i: (i, 0))),
        compiler_params=_params(("parallel",)),
        name="w_in_layout",
    )(w_in)


def _rope_tables(positions):
    inv_freq = ROPE_THETA ** (-jnp.arange(0, HEAD_DIM, 2, dtype=F32) / HEAD_DIM)
    ang = positions.astype(F32).reshape(-1, 1) * inv_freq
    cos, sin = jnp.cos(ang), jnp.sin(ang)
    return jnp.concatenate([cos] * 4, axis=-1), jnp.concatenate([-sin, sin, -sin, sin], axis=-1)


def _permute_heads(w, axis):
    heads = [lax.slice_in_dim(w, hd * HEAD_DIM, (hd + 1) * HEAD_DIM, axis=axis) for hd in _C_HEAD_ORDER]
    return jnp.concatenate(heads, axis=axis)


def _mixer(h, cos4, sin4, batch, norm_g, w_att, wg, b_forget, sinks, b_gate, w_br_a, w_br_b, w_br_c, w_out):
    t_tokens = h.shape[0]
    seq = t_tokens // batch
    qkv, f, *qkv_classes = _in_proj(h, norm_g.reshape(1, -1), w_att, cos4, sin4, batch)
    qkv3 = qkv.reshape(batch, seq, -1)

    o1, l1 = _banded(qkv3, COL_QA, COL_KA, COL_VA, A_MAX_DIST, kv_shared=False)
    class_outs, class_lses = [], []
    for dil, x in zip(A_DILATIONS[1:], qkv_classes):
        o, lse = _banded(x.reshape(batch * dil, seq // dil, -1), COL_QA, COL_KA, COL_VA, A_MAX_DIST, kv_shared=False)
        class_outs.append(o.reshape(batch, dil, seq // dil, WIDTH))
        class_lses.append(lse.reshape(batch, dil, seq // dil, WIDTH))
    oa = _combine(o1.reshape(t_tokens, WIDTH), l1.reshape(t_tokens, WIDTH), class_outs, class_lses, batch)

    sink_tab = jnp.stack([sinks[hd] for hd in _C_HEAD_ORDER]).astype(F32)
    (oc,) = _banded(qkv3, COL_QC, COL_KC, COL_VC, C_MAX_DIST, kv_shared=True, sink=sink_tab, want_lse=False)
    oc = oc.reshape(t_tokens, WIDTH)

    bf_row = jnp.pad(b_forget.astype(F32), (0, LANE - N_HEADS)).reshape(1, LANE)
    augq, augk = _cum_aug(f.reshape(batch, seq, LANE), bf_row)
    ob = _fox(qkv3, augq, augk).reshape(t_tokens, WIDTH)

    return _merge(h, norm_g.reshape(1, -1), oa, ob, oc, wg, b_gate.reshape(1, -1).astype(F32),
                  w_br_a.astype(BF16), w_br_b.astype(BF16), _permute_heads(w_br_c, 0).astype(BF16),
                  w_out.astype(BF16))


def kernel(x, positions, norm1_g, w_in, b_forget, sinks, b_gate, w_br_a, w_br_b, w_br_c, w_out, norm2_g,
           peer_wq, peer_k1, peer_k2, peer_u, peer_v, final_g):
    batch, seq, d_model = x.shape
    depth = w_in.shape[0]
    assert d_model == D_MODEL and seq % max(COMBINE_ROWS, CUM_ROWS) == 0 and seq // A_DILATIONS[-1] >= BLOCK
    assert w_in.shape[2] == sum(_IN_SIZES) and peer_u.shape[1:] == (N_EXPERTS, D_MODEL)
    cos4, sin4 = _rope_tables(positions)
    h = x.reshape(batch * seq, D_MODEL)
    for l in range(depth):
        w_att, w_gate = _w_in_layout(w_in, l)
        h = _mixer(h, cos4, sin4, batch, norm1_g[l], w_att, w_gate, b_forget[l], sinks[l], b_gate[l],
                   w_br_a[l], w_br_b[l], w_br_c[l], w_out[l])
        h = _peer(h, norm2_g[l].reshape(1, -1), _cast_layer(peer_wq, l), peer_k1[l].astype(BF16),
                  peer_k2[l].astype(BF16), _cast_layer(peer_u, l, scale=GELU_ARG_SCALE), _transpose_cast(peer_v, l),
                  final_g=final_g.reshape(1, -1) if l == depth - 1 else None)
    return h.reshape(batch, seq, D_MODEL)
```
